```python
import math
import jax, jax.numpy as jnp
from jax import lax
import numpy as np

D_MODEL = 2048
BATCH = 4
SEQ = 2048
DEPTH = 2

N_MEM = 256
MIX_WIDTH = D_MODEL
DIFF_V_DIM = 128
DIFF_QK_DIM = DIFF_V_DIM // 2
DIFF_HEADS = (MIX_WIDTH // 2) // DIFF_V_DIM
SB_DIM = 128
SB_HEADS = (MIX_WIDTH // 2) // SB_DIM
X_HEADS = 4
X_DIM = D_MODEL // X_HEADS
D_FF = 4 * D_MODEL
Q_BLOCK = 128
EPS = 1e-6
NEG_INF = -1e30

DIFF_Q_COLS = DIFF_HEADS * 2 * DIFF_QK_DIM
DIFF_V_COLS = DIFF_HEADS * DIFF_V_DIM
SB_COLS = SB_HEADS * SB_DIM
IN_COLS = 2 * DIFF_Q_COLS + DIFF_V_COLS + 3 * SB_COLS

kernel_name = "hymba_diff_stickbreak_alibi_memxattn"


def _rmsnorm(x, g):
    xf = x.astype(jnp.float32)
    y = xf * lax.rsqrt(jnp.mean(xf * xf, axis=-1, keepdims=True) + EPS)
    return (y * g.astype(jnp.float32)).astype(x.dtype)


def _alibi_slopes(n):
    return jnp.asarray(np.array([2.0 ** (-8.0 * (i + 1) / n) for i in range(n)], dtype=np.float32))


def _to_blocks(t):
    b, s = t.shape[:2]
    t = t.reshape((b, s // Q_BLOCK, Q_BLOCK) + t.shape[2:])
    return jnp.moveaxis(t, 1, 0)


def _from_blocks(t):
    t = jnp.moveaxis(t, 0, 1)
    return t.reshape((t.shape[0], t.shape[1] * t.shape[2]) + t.shape[3:])


def _diff_attention(q, k, v, pos, lam, lambda_init, subln_g):
    scale = DIFF_QK_DIM ** -0.5
    slopes = _alibi_slopes(DIFF_HEADS)

    def block(args):
        qb, pq = args
        s = jnp.einsum('bqhmd,bkhmd->bhmqk', qb, k,
                       preferred_element_type=jnp.float32) * scale
        dist = (pq[:, :, None] - pos[:, None, :]).astype(jnp.float32)
        bias = -slopes[None, :, None, None, None] * dist[:, None, None]
        s = jnp.where((dist >= 0)[:, None, None], s + bias, NEG_INF)
        p = jax.nn.softmax(s, axis=-1)
        a = p[:, :, 0] - lam * p[:, :, 1]
        return jnp.einsum('bhqk,bkhd->bqhd', a.astype(v.dtype), v)

    o = _from_blocks(lax.map(block, (_to_blocks(q), _to_blocks(pos))))
    o = _rmsnorm(o, subln_g) * (1.0 - lambda_init)
    return o.reshape(o.shape[0], o.shape[1], DIFF_V_COLS)


def _stick_breaking(q, k, v, pos, out_g):
    scale = SB_DIM ** -0.5

    def block(args):
        qb, pq = args
        z = jnp.einsum('bqhd,bkhd->bhqk', qb, k,
                       preferred_element_type=jnp.float32) * scale
        strict = (pq[:, :, None] > pos[:, None, :])[:, None]
        log_keep = jnp.where(strict, jax.nn.log_sigmoid(-z), 0.0)
        between = lax.cumsum(log_keep, axis=3, reverse=True) - log_keep
        a = jnp.where(strict, jnp.exp(jax.nn.log_sigmoid(z) + between), 0.0)
        return jnp.einsum('bhqk,bkhd->bqhd', a.astype(v.dtype), v)

    o = _from_blocks(lax.map(block, (_to_blocks(q), _to_blocks(pos))))
    o = _rmsnorm(o, out_g)
    return o.reshape(o.shape[0], o.shape[1], SB_COLS)


def _cross_attention(h, m, wq, wkv, wo):
    b, s, _ = h.shape
    q = (h @ wq).reshape(b, s, X_HEADS, X_DIM)
    k, v = jnp.split(m @ wkv, 2, axis=-1)
    k = k.reshape(b, m.shape[1], X_HEADS, X_DIM)
    v = v.reshape(b, m.shape[1], X_HEADS, X_DIM)
    sc = jnp.einsum('bqhd,bkhd->bhqk', q, k, preferred_element_type=jnp.float32) * (X_DIM ** -0.5)
    p = jax.nn.softmax(sc, axis=-1)
    o = jnp.einsum('bhqk,bkhd->bqhd', p.astype(v.dtype), v).reshape(b, s, D_MODEL)
    return o @ wo


def setup_inputs(seed: int = 0) -> dict:
    key = jax.random.key(seed)
    ks = jax.random.split(key, 24)
    f32 = jnp.float32

    def w(k, shape, fan_in):
        return jax.random.normal(k, shape, f32) * (fan_in ** -0.5)

    def gain(k, shape):
        return 1.0 + 0.02 * jax.random.normal(k, shape, f32)

    x = jax.random.normal(ks[0], (BATCH, SEQ, D_MODEL), f32)
    mem = jax.random.normal(ks[1], (BATCH, N_MEM, D_MODEL), f32)
    positions = jnp.broadcast_to(jnp.arange(SEQ, dtype=jnp.int32), (BATCH, SEQ))
    return {
        "x": x,
        "mem": mem,
        "positions": positions,
        "norm_mix": gain(ks[2], (DEPTH, D_MODEL)),
        "w_in": w(ks[3], (DEPTH, D_MODEL, IN_COLS), D_MODEL),
        "lam_q1": 0.1 * jax.random.normal(ks[4], (DEPTH, DIFF_QK_DIM), f32),
        "lam_k1": 0.1 * jax.random.normal(ks[5], (DEPTH, DIFF_QK_DIM), f32),
        "lam_q2": 0.1 * jax.random.normal(ks[6], (DEPTH, DIFF_QK_DIM), f32),
        "lam_k2": 0.1 * jax.random.normal(ks[7], (DEPTH, DIFF_QK_DIM), f32),
        "subln_diff": gain(ks[8], (DEPTH, DIFF_V_DIM)),
        "subln_sb": gain(ks[9], (DEPTH, SB_DIM)),
        "w_out": w(ks[10], (DEPTH, MIX_WIDTH, D_MODEL), MIX_WIDTH),
        "norm_x": gain(ks[11], (DEPTH, D_MODEL)),
        "norm_mem": gain(ks[12], (DEPTH, D_MODEL)),
        "wq_x": w(ks[13], (DEPTH, D_MODEL, D_MODEL), D_MODEL),
        "wkv_x": w(ks[14], (DEPTH, D_MODEL, 2 * D_MODEL), D_MODEL),
        "wo_x": w(ks[15], (DEPTH, D_MODEL, D_MODEL), D_MODEL),
        "norm_mlp": gain(ks[16], (DEPTH, D_MODEL)),
        "w_up": w(ks[17], (DEPTH, D_MODEL, D_FF), D_MODEL),
        "w_down": w(ks[18], (DEPTH, D_FF, D_MODEL), D_FF),
        "norm_final": gain(ks[19], (D_MODEL,)),
    }


def reference(x, mem, positions, norm_mix, w_in, lam_q1, lam_k1, lam_q2, lam_k2,
              subln_diff, subln_sb, w_out, norm_x, norm_mem, wq_x, wkv_x, wo_x,
              norm_mlp, w_up, w_down, norm_final):
    b, s, _ = x.shape
    h = x
    split_at = [DIFF_Q_COLS, 2 * DIFF_Q_COLS, 2 * DIFF_Q_COLS + DIFF_V_COLS,
                2 * DIFF_Q_COLS + DIFF_V_COLS + SB_COLS,
                2 * DIFF_Q_COLS + DIFF_V_COLS + 2 * SB_COLS]
    for l in range(DEPTH):
        lambda_init = 0.8 - 0.6 * math.exp(-0.3 * l)
        u = _rmsnorm(h, norm_mix[l])
        dq, dk, dv, sq, sk, sv = jnp.split(u @ w_in[l], split_at, axis=-1)
        lam = (jnp.exp(jnp.sum(lam_q1[l].astype(jnp.float32) * lam_k1[l].astype(jnp.float32)))
               - jnp.exp(jnp.sum(lam_q2[l].astype(jnp.float32) * lam_k2[l].astype(jnp.float32)))
               + lambda_init)
        o_diff = _diff_attention(dq.reshape(b, s, DIFF_HEADS, 2, DIFF_QK_DIM),
                                 dk.reshape(b, s, DIFF_HEADS, 2, DIFF_QK_DIM),
                                 dv.reshape(b, s, DIFF_HEADS, DIFF_V_DIM),
                                 positions, lam, lambda_init, subln_diff[l])
        o_sb = _stick_breaking(sq.reshape(b, s, SB_HEADS, SB_DIM),
                               sk.reshape(b, s, SB_HEADS, SB_DIM),
                               sv.reshape(b, s, SB_HEADS, SB_DIM),
                               positions, subln_sb[l])
        h = h + jnp.concatenate([o_diff, o_sb], axis=-1) @ w_out[l]
        h = h + _cross_attention(_rmsnorm(h, norm_x[l]), _rmsnorm(mem, norm_mem[l]),
                                 wq_x[l], wkv_x[l], wo_x[l])
        up = _rmsnorm(h, norm_mlp[l]) @ w_up[l]
        h = h + jnp.square(jax.nn.relu(up)) @ w_down[l]
    return _rmsnorm(h, norm_final)
```

```python
import functools
import math

import jax
import jax.numpy as jnp
import numpy as np
from jax import lax
from jax.experimental import pallas as pl
from jax.experimental.pallas import tpu as pltpu

D_MODEL = 2048
N_MEM = 256
DIFF_V_DIM = 128
DIFF_QK_DIM = 64
DIFF_HEADS = 8
SB_DIM = 128
SB_HEADS = 8
X_HEADS = 4
X_DIM = D_MODEL // X_HEADS
D_FF = 4 * D_MODEL
EPS = 1e-6
NEG_INF = -1e30

HEAD_COLS = 128
DIFF_Q_BLK = 0
DIFF_K_BLK = DIFF_HEADS
DIFF_V_BLK = 2 * DIFF_HEADS
SB_Q_BLK = 3 * DIFF_HEADS
SB_K_BLK = SB_Q_BLK + SB_HEADS
SB_V_BLK = SB_K_BLK + SB_HEADS
IN_COLS = (3 * DIFF_HEADS + 3 * SB_HEADS) * HEAD_COLS

ATT_TQ = 256
ATT_TK = 256
VMEM_LIMIT = 56 * 1024 * 1024

_BF16 = jnp.bfloat16
_F32 = jnp.float32
_NT = (((1,), (1,)), ((), ()))


def _params(semantics):
    return pltpu.CompilerParams(dimension_semantics=semantics, vmem_limit_bytes=VMEM_LIMIT)


def _rms(x, g):
    return x * lax.rsqrt(jnp.mean(x * x, axis=-1, keepdims=True) + EPS) * g


def _rmsnorm_kernel(x_ref, g_ref, o_ref):
    o_ref[...] = _rms(x_ref[...], g_ref[...]).astype(o_ref.dtype)


def _rmsnorm_bf16(x, g, tm):
    m, d = x.shape
    return pl.pallas_call(
        _rmsnorm_kernel,
        grid=(m // tm,),
        in_specs=[pl.BlockSpec((tm, d), lambda i: (i, 0)), pl.BlockSpec((1, d), lambda i: (0, 0))],
        out_specs=pl.BlockSpec((tm, d), lambda i: (i, 0)),
        out_shape=jax.ShapeDtypeStruct((m, d), _BF16),
        compiler_params=_params(("parallel",)),
        name="rmsnorm_bf16",
    )(x, g.reshape(1, d))


def _proj_kernel(x_ref, w_ref, o_ref):
    o_ref[...] = jnp.dot(x_ref[...], w_ref[...], preferred_element_type=_F32).astype(o_ref.dtype)


def _proj(x, w, tm, tn, name):
    m, k = x.shape
    n = w.shape[1]
    return pl.pallas_call(
        _proj_kernel,
        grid=(m // tm, n // tn),
        in_specs=[pl.BlockSpec((tm, k), lambda i, j: (i, 0)), pl.BlockSpec((k, tn), lambda i, j: (0, j))],
        out_specs=pl.BlockSpec((tm, tn), lambda i, j: (i, j)),
        out_shape=jax.ShapeDtypeStruct((m, n), _BF16),
        compiler_params=_params(("parallel", "arbitrary")),
        name=name,
    )(x, w)


def _proj_res_norm_kernel(*refs, n_in):
    xs = refs[:n_in]
    ws = refs[n_in:2 * n_in]
    h_ref, g_ref, h_out_ref, u_out_ref = refs[2 * n_in:]
    acc = h_ref[...]
    for x_ref, w_ref in zip(xs, ws):
        acc = acc + jnp.dot(x_ref[...], w_ref[...], preferred_element_type=_F32)
    h_out_ref[...] = acc
    u_out_ref[...] = _rms(acc, g_ref[...]).astype(u_out_ref.dtype)


def _proj_res_norm(xs, w, h, g, tm, name):
    m, d = h.shape
    n_in = len(xs)
    kx = xs[0].shape[1]
    in_specs = [pl.BlockSpec((tm, kx), lambda i: (i, 0)) for _ in xs]
    in_specs += [pl.BlockSpec((kx, d), lambda i, r=r: (r, 0)) for r in range(n_in)]
    in_specs += [pl.BlockSpec((tm, d), lambda i: (i, 0)), pl.BlockSpec((1, d), lambda i: (0, 0))]
    return pl.pallas_call(
        functools.partial(_proj_res_norm_kernel, n_in=n_in),
        grid=(m // tm,),
        in_specs=in_specs,
        out_specs=[pl.BlockSpec((tm, d), lambda i: (i, 0)), pl.BlockSpec((tm, d), lambda i: (i, 0))],
        out_shape=[jax.ShapeDtypeStruct((m, d), _F32), jax.ShapeDtypeStruct((m, d), _BF16)],
        compiler_params=_params(("parallel",)),
        name=name,
    )(*xs, *([w] * n_in), h, g.reshape(1, d))


def _mlp_kernel(u_ref, wu_ref, wd_ref, h_ref, g_ref, h_out_ref, y_out_ref):
    f = pl.program_id(1)
    up = jnp.dot(u_ref[...], wu_ref[...], preferred_element_type=_F32)
    a = jnp.square(jnp.maximum(up, 0.0)).astype(_BF16)
    part = jnp.dot(a, wd_ref[...], preferred_element_type=_F32)

    @pl.when(f == 0)
    def _():
        h_out_ref[...] = h_ref[...] + part

    @pl.when(f > 0)
    def _():
        h_out_ref[...] += part

    @pl.when(f == pl.num_programs(1) - 1)
    def _():
        y_out_ref[...] = _rms(h_out_ref[...], g_ref[...]).astype(y_out_ref.dtype)


def _mlp(u, w_up, w_down, h, g, y_dtype, tm, tf):
    m, d = h.shape
    dff = w_up.shape[1]
    return pl.pallas_call(
        _mlp_kernel,
        grid=(m // tm, dff // tf),
        in_specs=[
            pl.BlockSpec((tm, d), lambda i, f: (i, 0)),
            pl.BlockSpec((d, tf), lambda i, f: (0, f)),
            pl.BlockSpec((tf, d), lambda i, f: (f, 0)),
            pl.BlockSpec((tm, d), lambda i, f: (i, 0)),
            pl.BlockSpec((1, d), lambda i, f: (0, 0)),
        ],
        out_specs=[pl.BlockSpec((tm, d), lambda i, f: (i, 0)), pl.BlockSpec((tm, d), lambda i, f: (i, 0))],
        out_shape=[jax.ShapeDtypeStruct((m, d), _F32), jax.ShapeDtypeStruct((m, d), y_dtype)],
        compiler_params=_params(("parallel", "arbitrary")),
        name="mlp_relu2",
    )(u, w_up, w_down, h, g.reshape(1, d))


def _diff_attn_kernel(nfull_ref, nvis_ref, slopes_ref,
                      q_ref, k_ref, v_ref, pq_ref, pk_ref, lq1_ref, lk1_ref, lq2_ref, lk2_ref, g_ref,
                      o_ref, m_s, l_s, acc_s, *, lambda_init, tq, tk):
    b, h, i = pl.program_id(0), pl.program_id(1), pl.program_id(2)
    nq = pl.num_programs(2)
    n_full = nfull_ref[b * nq + i]
    n_vis = nvis_ref[b * nq + i]
    slope = slopes_ref[h]

    q = q_ref[...] * (DIFF_QK_DIM ** -0.5)
    lane = lax.broadcasted_iota(jnp.int32, q.shape, 1)
    zero = jnp.zeros_like(q)
    q_maps = (jnp.where(lane < DIFF_QK_DIM, q, zero), jnp.where(lane >= DIFF_QK_DIM, q, zero))
    pq = pq_ref[...]
    pq0 = pq[0:1, :]

    m_s[...] = jnp.full(m_s.shape, NEG_INF, _F32)
    l_s[...] = jnp.zeros(l_s.shape, _F32)
    acc_s[...] = jnp.zeros(acc_s.shape, _F32)

    def step(j, masked):
        start = pl.multiple_of(j * tk, tk)
        k = k_ref[pl.ds(start, tk), :]
        v = v_ref[pl.ds(start, tk), :]
        pk = pk_ref[0, :, pl.ds(start, tk)]
        col_bias = slope * (pk - pq0)
        if masked:
            visible = pq >= pk
        for mi in range(2):
            s = lax.dot_general(q_maps[mi], k, _NT, preferred_element_type=_F32) + col_bias
            if masked:
                s = jnp.where(visible, s, NEG_INF)
            m_prev = m_s[mi]
            m_new = jnp.maximum(m_prev, jnp.max(s, axis=-1, keepdims=True))
            alpha = jnp.exp(m_prev - m_new)
            p = jnp.exp(s - m_new)
            l_s[mi] = alpha * l_s[mi] + jnp.sum(p, axis=-1, keepdims=True)
            acc_s[mi] = alpha * acc_s[mi] + jnp.dot(p.astype(_BF16), v, preferred_element_type=_F32)
            m_s[mi] = m_new

    def full_body(j, c):
        step(j, False)
        return c

    def masked_body(j, c):
        step(j, True)
        return c

    lax.fori_loop(0, n_full, full_body, 0)
    lax.fori_loop(n_full, n_vis, masked_body, 0)

    lam = (jnp.exp(jnp.sum(lq1_ref[...] * lk1_ref[...], axis=-1, keepdims=True))
           - jnp.exp(jnp.sum(lq2_ref[...] * lk2_ref[...], axis=-1, keepdims=True)) + lambda_init)
    o = acc_s[0] / l_s[0] - lam * (acc_s[1] / l_s[1])
    o_ref[...] = (_rms(o, g_ref[...]) * (1.0 - lambda_init)).astype(o_ref.dtype)


def _diff_attention(qkv, pos_col, pos_row, n_full, n_vis, lam_params, g, lambda_init, batch, seq):
    tq, tk = ATT_TQ, ATT_TK
    nq = seq // tq
    slopes = jnp.asarray(np.array([2.0 ** (-8.0 * (i + 1) / DIFF_HEADS) for i in range(DIFF_HEADS)], np.float32))
    small = lambda b, h, i, *_: (0, 0)
    grid_spec = pltpu.PrefetchScalarGridSpec(
        num_scalar_prefetch=3,
        grid=(batch, DIFF_HEADS, nq),
        in_specs=[
            pl.BlockSpec((tq, HEAD_COLS), lambda b, h, i, *_: (b * nq + i, DIFF_Q_BLK + h)),
            pl.BlockSpec((seq, HEAD_COLS), lambda b, h, i, *_: (b, DIFF_K_BLK + h)),
            pl.BlockSpec((seq, HEAD_COLS), lambda b, h, i, *_: (b, DIFF_V_BLK + h)),
            pl.BlockSpec((tq, 1), lambda b, h, i, *_: (b * nq + i, 0)),
            pl.BlockSpec((1, 1, seq), lambda b, h, i, *_: (b, 0, 0)),
            pl.BlockSpec((1, DIFF_QK_DIM), small), pl.BlockSpec((1, DIFF_QK_DIM), small),
            pl.BlockSpec((1, DIFF_QK_DIM), small), pl.BlockSpec((1, DIFF_QK_DIM), small),
            pl.BlockSpec((1, DIFF_V_DIM), small),
        ],
        out_specs=pl.BlockSpec((tq, HEAD_COLS), lambda b, h, i, *_: (b * nq + i, h)),
        scratch_shapes=[pltpu.VMEM((2, tq, 1), _F32), pltpu.VMEM((2, tq, 1), _F32),
                        pltpu.VMEM((2, tq, DIFF_V_DIM), _F32)],
    )
    return pl.pallas_call(
        functools.partial(_diff_attn_kernel, lambda_init=lambda_init, tq=tq, tk=tk),
        grid_spec=grid_spec,
        out_shape=jax.ShapeDtypeStruct((batch * seq, DIFF_HEADS * DIFF_V_DIM), _BF16),
        compiler_params=_params(("parallel", "parallel", "arbitrary")),
        name="diff_attention",
    )(n_full, n_vis, slopes, qkv, qkv, qkv, pos_col, pos_row,
      *[p.reshape(1, DIFF_QK_DIM) for p in lam_params], g.reshape(1, DIFF_V_DIM))


def _sb_attn_kernel(nfull_ref, nvis_ref, q_ref, k_ref, v_ref, pq_ref, pk_ref, tri_ref, g_ref,
                    o_ref, c_s, acc_s, *, tq, tk):
    b, i = pl.program_id(0), pl.program_id(2)
    nq = pl.num_programs(2)
    n_full = nfull_ref[b * nq + i]
    n_vis = nvis_ref[b * nq + i]
    q = q_ref[...]
    pq = pq_ref[...]
    tri = tri_ref[...]
    scale = SB_DIM ** -0.5

    c_s[...] = jnp.zeros(c_s.shape, _F32)
    acc_s[...] = jnp.zeros(acc_s.shape, _F32)

    def step(j, masked):
        start = pl.multiple_of(j * tk, tk)
        k = k_ref[pl.ds(start, tk), :]
        v = v_ref[pl.ds(start, tk), :]
        z = lax.dot_general(q, k, _NT, preferred_element_type=_F32) * scale
        softplus = jnp.maximum(z, 0.0) + jnp.log(1.0 + jnp.exp(-jnp.abs(z)))
        log_keep = -softplus
        if masked:
            strict = pq > pk_ref[0, :, pl.ds(start, tk)]
            log_keep = jnp.where(strict, log_keep, 0.0)
        hi = log_keep.astype(_BF16)
        lo = (log_keep - hi.astype(_F32)).astype(_BF16)
        later = (jnp.dot(hi, tri, preferred_element_type=_F32) + jnp.dot(lo, tri, preferred_element_type=_F32))
        a = jnp.exp((z - softplus) + (later + c_s[...]))
        if masked:
            a = jnp.where(strict, a, 0.0)
        acc_s[...] += jnp.dot(a.astype(_BF16), v, preferred_element_type=_F32)
        c_s[...] += jnp.sum(log_keep, axis=-1, keepdims=True)

    def rev(lo_j, hi_j, masked):
        def body(t, c):
            step(hi_j - 1 - t, masked)
            return c
        lax.fori_loop(0, hi_j - lo_j, body, 0)

    rev(n_full, n_vis, True)
    rev(0, n_full, False)
    o_ref[...] = _rms(acc_s[...], g_ref[...]).astype(o_ref.dtype)


def _sb_attention(qkv, pos_col, pos_row, n_full, n_vis, g, batch, seq):
    tq, tk = ATT_TQ, ATT_TK
    nq = seq // tq
    idx = np.arange(tk)
    tri = jnp.asarray((idx[:, None] > idx[None, :]).astype(np.float32), _BF16)
    grid_spec = pltpu.PrefetchScalarGridSpec(
        num_scalar_prefetch=2,
        grid=(batch, SB_HEADS, nq),
        in_specs=[
            pl.BlockSpec((tq, HEAD_COLS), lambda b, h, i, *_: (b * nq + i, SB_Q_BLK + h)),
            pl.BlockSpec((seq, HEAD_COLS), lambda b, h, i, *_: (b, SB_K_BLK + h)),
            pl.BlockSpec((seq, HEAD_COLS), lambda b, h, i, *_: (b, SB_V_BLK + h)),
            pl.BlockSpec((tq, 1), lambda b, h, i, *_: (b * nq + i, 0)),
            pl.BlockSpec((1, 1, seq), lambda b, h, i, *_: (b, 0, 0)),
            pl.BlockSpec((tk, tk), lambda b, h, i, *_: (0, 0)),
            pl.BlockSpec((1, SB_DIM), lambda b, h, i, *_: (0, 0)),
        ],
        out_specs=pl.BlockSpec((tq, HEAD_COLS), lambda b, h, i, *_: (b * nq + i, h)),
        scratch_shapes=[pltpu.VMEM((tq, 1), _F32), pltpu.VMEM((tq, SB_DIM), _F32)],
    )
    return pl.pallas_call(
        functools.partial(_sb_attn_kernel, tq=tq, tk=tk),
        grid_spec=grid_spec,
        out_shape=jax.ShapeDtypeStruct((batch * seq, SB_HEADS * SB_DIM), _BF16),
        compiler_params=_params(("parallel", "parallel", "arbitrary")),
        name="stick_breaking_attention",
    )(n_full, n_vis, qkv, qkv, qkv, pos_col, pos_row, tri, g.reshape(1, SB_DIM))


def _xattn_kernel(q_ref, k_ref, v_ref, o_ref):
    scale = X_DIM ** -0.5
    for hd in range(X_HEADS):
        cols = slice(hd * X_DIM, (hd + 1) * X_DIM)
        s = lax.dot_general(q_ref[:, cols], k_ref[:, cols], _NT, preferred_element_type=_F32) * scale
        p = jnp.exp(s - jnp.max(s, axis=-1, keepdims=True))
        o = jnp.dot(p.astype(_BF16), v_ref[:, cols], preferred_element_type=_F32)
        o_ref[:, cols] = (o / jnp.sum(p, axis=-1, keepdims=True)).astype(o_ref.dtype)


def _cross_attention(q, kv, batch, seq, tq):
    nq = seq // tq
    return pl.pallas_call(
        _xattn_kernel,
        grid=(batch, nq),
        in_specs=[
            pl.BlockSpec((tq, D_MODEL), lambda b, i: (b * nq + i, 0)),
            pl.BlockSpec((N_MEM, D_MODEL), lambda b, i: (b, 0)),
            pl.BlockSpec((N_MEM, D_MODEL), lambda b, i: (b, 1)),
        ],
        out_specs=pl.BlockSpec((tq, D_MODEL), lambda b, i: (b * nq + i, 0)),
        out_shape=jax.ShapeDtypeStruct((batch * seq, D_MODEL), _BF16),
        compiler_params=_params(("parallel", "arbitrary")),
        name="cross_attention",
    )(q, kv, kv)


def _visible_block_counts(positions, tq, tk, strict):
    batch, seq = positions.shape
    qmin = positions.reshape(batch, seq // tq, tq).min(-1)[:, :, None]
    qmax = positions.reshape(batch, seq // tq, tq).max(-1)[:, :, None]
    kmin = positions.reshape(batch, seq // tk, tk).min(-1)[:, None, :]
    kmax = positions.reshape(batch, seq // tk, tk).max(-1)[:, None, :]
    full = (kmax < qmin) if strict else (kmax <= qmin)
    some = (kmin < qmax) if strict else (kmin <= qmax)
    n_full = jnp.sum(full, axis=-1).astype(jnp.int32).reshape(-1)
    n_vis = jnp.sum(some, axis=-1).astype(jnp.int32).reshape(-1)
    return n_full, jnp.maximum(n_vis, n_full)


def kernel(x, mem, positions, norm_mix, w_in, lam_q1, lam_k1, lam_q2, lam_k2, subln_diff, subln_sb, w_out,
           norm_x, norm_mem, wq_x, wkv_x, wo_x, norm_mlp, w_up, w_down, norm_final):
    batch, seq, d = x.shape
    depth = w_in.shape[0]
    tokens = batch * seq
    assert d == D_MODEL and w_in.shape[2] == IN_COLS and seq % ATT_TQ == 0 and seq % ATT_TK == 0

    pos_f = positions.astype(_F32)
    pos_col = pos_f.reshape(tokens, 1)
    pos_row = pos_f.reshape(batch, 1, seq)
    diff_counts = _visible_block_counts(positions, ATT_TQ, ATT_TK, strict=False)
    sb_counts = _visible_block_counts(positions, ATT_TQ, ATT_TK, strict=True)

    h = x.reshape(tokens, d)
    mem2 = mem.reshape(batch * N_MEM, d)
    u = _rmsnorm_bf16(h, norm_mix[0], tm=512)
    out = None
    for l in range(depth):
        lambda_init = 0.8 - 0.6 * math.exp(-0.3 * l)
        qkv = _proj(u, w_in[l].astype(_BF16), tm=1024, tn=1024, name="in_proj")
        o_diff = _diff_attention(qkv, pos_col, pos_row, *diff_counts,
                                 (lam_q1[l], lam_k1[l], lam_q2[l], lam_k2[l]), subln_diff[l], lambda_init,
                                 batch, seq)
        o_sb = _sb_attention(qkv, pos_col, pos_row, *sb_counts, subln_sb[l], batch, seq)
        h, ux = _proj_res_norm([o_diff, o_sb], w_out[l].astype(_BF16), h, norm_x[l], tm=512, name="out_proj")

        mem_n = _rmsnorm_bf16(mem2, norm_mem[l], tm=256)
        kv = _proj(mem_n, wkv_x[l].astype(_BF16), tm=1024, tn=1024, name="xattn_kv_proj")
        qx = _proj(ux, wq_x[l].astype(_BF16), tm=1024, tn=1024, name="xattn_q_proj")
        ox = _cross_attention(qx, kv, batch, seq, tq=512)
        h, um = _proj_res_norm([ox], wo_x[l].astype(_BF16), h, norm_mlp[l], tm=512, name="xattn_o_proj")

        last = l == depth - 1
        g_next = norm_final if last else norm_mix[l + 1]
        h, y = _mlp(um, w_up[l].astype(_BF16), w_down[l].astype(_BF16), h, g_next,
                    _F32 if last else _BF16, tm=512, tf=1024)
        if last:
            out = y
        else:
            u = y
    return out.reshape(batch, seq, d)
```

```python
import functools
import math

import jax
import jax.numpy as jnp
import numpy as np
from jax import lax
from jax.experimental import pallas as pl
from jax.experimental.pallas import tpu as pltpu

D_MODEL = 2048
N_MEM = 256
DIFF_V_DIM = 128
DIFF_QK_DIM = 64
DIFF_HEADS = 8
SB_DIM = 128
SB_HEADS = 8
X_HEADS = 4
X_DIM = D_MODEL // X_HEADS
D_FF = 4 * D_MODEL
EPS = 1e-6
NEG_INF = -1e30
LOG2E = math.log2(math.e)

LANES = 128
HEAD_COLS = 128
DIFF_Q_BLK = 0
DIFF_K_BLK = DIFF_HEADS
DIFF_V_BLK = 2 * DIFF_HEADS
SB_Q_BLK = 3 * DIFF_HEADS
SB_K_BLK = SB_Q_BLK + SB_HEADS
SB_V_BLK = SB_K_BLK + SB_HEADS
IN_COLS = (3 * DIFF_HEADS + 3 * SB_HEADS) * HEAD_COLS

DIFF_TQ, DIFF_TK = 512, 512
SB_TQ, SB_TK = 512, 256
VMEM_LIMIT = 56 * 1024 * 1024

_BF16 = jnp.bfloat16
_F32 = jnp.float32
_NT = (((1,), (1,)), ((), ()))


def _params(semantics):
    return pltpu.CompilerParams(dimension_semantics=semantics, vmem_limit_bytes=VMEM_LIMIT)


def _rms(x, g):
    return x * lax.rsqrt(jnp.mean(x * x, axis=-1, keepdims=True) + EPS) * g


def _lane_tiles(x):
    return [x[:, c * LANES:(c + 1) * LANES] for c in range(x.shape[1] // LANES)]


def _rmsnorm_kernel(x_ref, g_ref, o_ref):
    o_ref[...] = _rms(x_ref[...], g_ref[...]).astype(o_ref.dtype)


def _rmsnorm_bf16(x, g, tm):
    m, d = x.shape
    return pl.pallas_call(
        _rmsnorm_kernel,
        grid=(m // tm,),
        in_specs=[pl.BlockSpec((tm, d), lambda i: (i, 0)), pl.BlockSpec((1, d), lambda i: (0, 0))],
        out_specs=pl.BlockSpec((tm, d), lambda i: (i, 0)),
        out_shape=jax.ShapeDtypeStruct((m, d), _BF16),
        compiler_params=_params(("parallel",)),
        name="rmsnorm_bf16",
    )(x, g.reshape(1, d))


def _proj_kernel(x_ref, w_ref, *rest):
    o_ref = rest[-1]
    acc = jnp.dot(x_ref[...], w_ref[...], preferred_element_type=_F32)
    if len(rest) == 2:
        acc = acc * rest[0][...]
    o_ref[...] = acc.astype(o_ref.dtype)


def _proj(x, w, col_scale, tm, tn, name):
    m, k = x.shape
    n = w.shape[1]
    in_specs = [pl.BlockSpec((tm, k), lambda i, j: (i, 0)), pl.BlockSpec((k, tn), lambda i, j: (0, j))]
    args = [x, w]
    if col_scale is not None:
        in_specs.append(pl.BlockSpec((1, tn), lambda i, j: (0, j)))
        args.append(col_scale)
    return pl.pallas_call(
        _proj_kernel,
        grid=(m // tm, n // tn),
        in_specs=in_specs,
        out_specs=pl.BlockSpec((tm, tn), lambda i, j: (i, j)),
        out_shape=jax.ShapeDtypeStruct((m, n), _BF16),
        compiler_params=_params(("parallel", "arbitrary")),
        name=name,
    )(*args)


def _proj_res_norm_kernel(*refs, n_in):
    xs = refs[:n_in]
    ws = refs[n_in:2 * n_in]
    h_ref, g_ref, h_out_ref, u_out_ref = refs[2 * n_in:]
    acc = h_ref[...]
    for x_ref, w_ref in zip(xs, ws):
        acc = acc + jnp.dot(x_ref[...], w_ref[...], preferred_element_type=_F32)
    h_out_ref[...] = acc
    u_out_ref[...] = _rms(acc, g_ref[...]).astype(u_out_ref.dtype)


def _proj_res_norm(xs, w, h, g, tm, name):
    m, d = h.shape
    n_in = len(xs)
    kx = xs[0].shape[1]
    in_specs = [pl.BlockSpec((tm, kx), lambda i: (i, 0)) for _ in xs]
    in_specs += [pl.BlockSpec((kx, d), lambda i, r=r: (r, 0)) for r in range(n_in)]
    in_specs += [pl.BlockSpec((tm, d), lambda i: (i, 0)), pl.BlockSpec((1, d), lambda i: (0, 0))]
    return pl.pallas_call(
        functools.partial(_proj_res_norm_kernel, n_in=n_in),
        grid=(m // tm,),
        in_specs=in_specs,
        out_specs=[pl.BlockSpec((tm, d), lambda i: (i, 0)), pl.BlockSpec((tm, d), lambda i: (i, 0))],
        out_shape=[jax.ShapeDtypeStruct((m, d), _F32), jax.ShapeDtypeStruct((m, d), _BF16)],
        compiler_params=_params(("parallel",)),
        name=name,
    )(*xs, *([w] * n_in), h, g.reshape(1, d))


def _mlp_kernel(u_ref, wu_ref, wd_ref, h_ref, g_ref, h_out_ref, y_out_ref):
    f = pl.program_id(1)
    up = jnp.dot(u_ref[...], wu_ref[...], preferred_element_type=_F32)
    a = jnp.square(jnp.maximum(up, 0.0)).astype(_BF16)
    part = jnp.dot(a, wd_ref[...], preferred_element_type=_F32)

    @pl.when(f == 0)
    def _():
        h_out_ref[...] = h_ref[...] + part

    @pl.when(f > 0)
    def _():
        h_out_ref[...] += part

    @pl.when(f == pl.num_programs(1) - 1)
    def _():
        y_out_ref[...] = _rms(h_out_ref[...], g_ref[...]).astype(y_out_ref.dtype)


def _mlp(u, w_up, w_down, h, g, y_dtype, tm, tf):
    m, d = h.shape
    dff = w_up.shape[1]
    return pl.pallas_call(
        _mlp_kernel,
        grid=(m // tm, dff // tf),
        in_specs=[
            pl.BlockSpec((tm, d), lambda i, f: (i, 0)),
            pl.BlockSpec((d, tf), lambda i, f: (0, f)),
            pl.BlockSpec((tf, d), lambda i, f: (f, 0)),
            pl.BlockSpec((tm, d), lambda i, f: (i, 0)),
            pl.BlockSpec((1, d), lambda i, f: (0, 0)),
        ],
        out_specs=[pl.BlockSpec((tm, d), lambda i, f: (i, 0)), pl.BlockSpec((tm, d), lambda i, f: (i, 0))],
        out_shape=[jax.ShapeDtypeStruct((m, d), _F32), jax.ShapeDtypeStruct((m, d), y_dtype)],
        compiler_params=_params(("parallel", "arbitrary")),
        name="mlp_relu2",
    )(u, w_up, w_down, h, g.reshape(1, d))


def _diff_attn_kernel(nfull_ref, nvis_ref, slopes_ref,
                      q_ref, k_ref, v_ref, pq_ref, pk_ref, lq1_ref, lk1_ref, lq2_ref, lk2_ref, g_ref,
                      o_ref, m_s, l_s, acc_s, *, lambda_init, tq, tk):
    b, h, i = pl.program_id(0), pl.program_id(1), pl.program_id(2)
    nq = pl.num_programs(2)
    n_full = nfull_ref[b * nq + i]
    n_vis = nvis_ref[b * nq + i]
    slope2 = slopes_ref[h] * LOG2E

    q = q_ref[...]
    lane = lax.broadcasted_iota(jnp.int32, q.shape, 1)
    zero = jnp.zeros_like(q)
    q_maps = (jnp.where(lane < DIFF_QK_DIM, q, zero), jnp.where(lane >= DIFF_QK_DIM, q, zero))
    pq = pq_ref[...]
    pq0 = pq[0:1, :]

    m_s[...] = jnp.full(m_s.shape, NEG_INF, _F32)
    l_s[...] = jnp.zeros(l_s.shape, _F32)
    acc_s[...] = jnp.zeros(acc_s.shape, _F32)

    def step(j, masked):
        start = pl.multiple_of(j * tk, tk)
        k = k_ref[pl.ds(start, tk), :]
        v = v_ref[pl.ds(start, tk), :]
        pk = pk_ref[0, :, pl.ds(start, tk)]
        col_bias = slope2 * (pk - pq0)
        if masked:
            visible = pq >= pk
        for mi in range(2):
            s = lax.dot_general(q_maps[mi], k, _NT, preferred_element_type=_F32) + col_bias
            if masked:
                s = jnp.where(visible, s, NEG_INF)
            m_prev = m_s[mi]
            m_new = jnp.maximum(m_prev, jnp.max(s, axis=-1, keepdims=True))
            alpha = jnp.exp2(m_prev - m_new)
            p_tiles = [jnp.exp2(s_c - m_new) for s_c in _lane_tiles(s)]
            l_s[mi] = alpha * l_s[mi] + functools.reduce(jnp.add, p_tiles)
            p = jnp.concatenate(p_tiles, axis=1).astype(_BF16)
            acc_s[mi] = alpha * acc_s[mi] + jnp.dot(p, v, preferred_element_type=_F32)
            m_s[mi] = m_new

    def full_body(j, c):
        step(j, False)
        return c

    def masked_body(j, c):
        step(j, True)
        return c

    lax.fori_loop(0, n_full, full_body, 0)
    lax.fori_loop(n_full, n_vis, masked_body, 0)

    lam = (jnp.exp(jnp.sum(lq1_ref[...] * lk1_ref[...], axis=-1, keepdims=True))
           - jnp.exp(jnp.sum(lq2_ref[...] * lk2_ref[...], axis=-1, keepdims=True)) + lambda_init)
    l1 = jnp.sum(l_s[0], axis=-1, keepdims=True)
    l2 = jnp.sum(l_s[1], axis=-1, keepdims=True)
    o = acc_s[0] / l1 - lam * (acc_s[1] / l2)
    o_ref[...] = (_rms(o, g_ref[...]) * (1.0 - lambda_init)).astype(o_ref.dtype)


def _diff_attention(qkv, pos_col, pos_row, n_full, n_vis, lam_params, g, lambda_init, batch, seq):
    tq, tk = DIFF_TQ, DIFF_TK
    nq = seq // tq
    slopes = jnp.asarray(np.array([2.0 ** (-8.0 * (i + 1) / DIFF_HEADS) for i in range(DIFF_HEADS)], np.float32))
    small = lambda b, h, i, *_: (0, 0)
    grid_spec = pltpu.PrefetchScalarGridSpec(
        num_scalar_prefetch=3,
        grid=(batch, DIFF_HEADS, nq),
        in_specs=[
            pl.BlockSpec((tq, HEAD_COLS), lambda b, h, i, *_: (b * nq + i, DIFF_Q_BLK + h)),
            pl.BlockSpec((seq, HEAD_COLS), lambda b, h, i, *_: (b, DIFF_K_BLK + h)),
            pl.BlockSpec((seq, HEAD_COLS), lambda b, h, i, *_: (b, DIFF_V_BLK + h)),
            pl.BlockSpec((tq, 1), lambda b, h, i, *_: (b * nq + i, 0)),
            pl.BlockSpec((1, 1, seq), lambda b, h, i, *_: (b, 0, 0)),
            pl.BlockSpec((1, DIFF_QK_DIM), small), pl.BlockSpec((1, DIFF_QK_DIM), small),
            pl.BlockSpec((1, DIFF_QK_DIM), small), pl.BlockSpec((1, DIFF_QK_DIM), small),
            pl.BlockSpec((1, DIFF_V_DIM), small),
        ],
        out_specs=pl.BlockSpec((tq, HEAD_COLS), lambda b, h, i, *_: (b * nq + i, h)),
        scratch_shapes=[pltpu.VMEM((2, tq, LANES), _F32), pltpu.VMEM((2, tq, LANES), _F32),
                        pltpu.VMEM((2, tq, DIFF_V_DIM), _F32)],
    )
    return pl.pallas_call(
        functools.partial(_diff_attn_kernel, lambda_init=lambda_init, tq=tq, tk=tk),
        grid_spec=grid_spec,
        out_shape=jax.ShapeDtypeStruct((batch * seq, DIFF_HEADS * DIFF_V_DIM), _BF16),
        compiler_params=_params(("parallel", "parallel", "arbitrary")),
        name="diff_attention",
    )(n_full, n_vis, slopes, qkv, qkv, qkv, pos_col, pos_row,
      *[p.reshape(1, DIFF_QK_DIM) for p in lam_params], g.reshape(1, DIFF_V_DIM))


def _sb_attn_kernel(nfull_ref, nvis_ref, q_ref, k_ref, v_ref, pq_ref, pk_ref, ntri_ref, g_ref,
                    o_ref, c_s, acc_s, *, tq, tk):
    b, i = pl.program_id(0), pl.program_id(2)
    nq = pl.num_programs(2)
    n_full = nfull_ref[b * nq + i]
    n_vis = nvis_ref[b * nq + i]
    q = q_ref[...]
    pq = pq_ref[...]
    ntri = ntri_ref[...]

    c_s[...] = jnp.zeros(c_s.shape, _F32)
    acc_s[...] = jnp.zeros(acc_s.shape, _F32)

    def step(j, masked):
        start = pl.multiple_of(j * tk, tk)
        k = k_ref[pl.ds(start, tk), :]
        v = v_ref[pl.ds(start, tk), :]
        z = lax.dot_general(q, k, _NT, preferred_element_type=_F32)
        softplus = jnp.maximum(z, 0.0) + jnp.log2(1.0 + jnp.exp2(-jnp.abs(z)))
        log_sig = z - softplus
        if masked:
            strict = pq > pk_ref[0, :, pl.ds(start, tk)]
            softplus = jnp.where(strict, softplus, 0.0)
        hi = softplus.astype(_BF16)
        lo = (softplus - hi.astype(_F32)).astype(_BF16)
        later = jnp.dot(hi, ntri, preferred_element_type=_F32) + jnp.dot(lo, ntri, preferred_element_type=_F32)
        c = c_s[...]
        a_tiles = [jnp.exp2(ls_c + (lt_c + c)) for ls_c, lt_c in zip(_lane_tiles(log_sig), _lane_tiles(later))]
        a = jnp.concatenate(a_tiles, axis=1)
        if masked:
            a = jnp.where(strict, a, 0.0)
        acc_s[...] += jnp.dot(a.astype(_BF16), v, preferred_element_type=_F32)
        c_s[...] = c - jnp.sum(softplus, axis=-1, keepdims=True)

    def rev(lo_j, hi_j, masked):
        def body(t, c):
            step(hi_j - 1 - t, masked)
            return c
        lax.fori_loop(0, hi_j - lo_j, body, 0)

    rev(n_full, n_vis, True)
    rev(0, n_full, False)
    o_ref[...] = _rms(acc_s[...], g_ref[...]).astype(o_ref.dtype)


def _sb_attention(qkv, pos_col, pos_row, n_full, n_vis, g, batch, seq):
    tq, tk = SB_TQ, SB_TK
    nq = seq // tq
    idx = np.arange(tk)
    ntri = jnp.asarray(-(idx[:, None] > idx[None, :]).astype(np.float32), _BF16)
    grid_spec = pltpu.PrefetchScalarGridSpec(
        num_scalar_prefetch=2,
        grid=(batch, SB_HEADS, nq),
        in_specs=[
            pl.BlockSpec((tq, HEAD_COLS), lambda b, h, i, *_: (b * nq + i, SB_Q_BLK + h)),
            pl.BlockSpec((seq, HEAD_COLS), lambda b, h, i, *_: (b, SB_K_BLK + h)),
            pl.BlockSpec((seq, HEAD_COLS), lambda b, h, i, *_: (b, SB_V_BLK + h)),
            pl.BlockSpec((tq, 1), lambda b, h, i, *_: (b * nq + i, 0)),
            pl.BlockSpec((1, 1, seq), lambda b, h, i, *_: (b, 0, 0)),
            pl.BlockSpec((tk, tk), lambda b, h, i, *_: (0, 0)),
            pl.BlockSpec((1, SB_DIM), lambda b, h, i, *_: (0, 0)),
        ],
        out_specs=pl.BlockSpec((tq, HEAD_COLS), lambda b, h, i, *_: (b * nq + i, h)),
        scratch_shapes=[pltpu.VMEM((tq, LANES), _F32), pltpu.VMEM((tq, SB_DIM), _F32)],
    )
    return pl.pallas_call(
        functools.partial(_sb_attn_kernel, tq=tq, tk=tk),
        grid_spec=grid_spec,
        out_shape=jax.ShapeDtypeStruct((batch * seq, SB_HEADS * SB_DIM), _BF16),
        compiler_params=_params(("parallel", "parallel", "arbitrary")),
        name="stick_breaking_attention",
    )(n_full, n_vis, qkv, qkv, qkv, pos_col, pos_row, ntri, g.reshape(1, SB_DIM))


def _xattn_kernel(q_ref, k_ref, v_ref, o_ref):
    for hd in range(X_HEADS):
        cols = slice(hd * X_DIM, (hd + 1) * X_DIM)
        s = lax.dot_general(q_ref[:, cols], k_ref[:, cols], _NT, preferred_element_type=_F32)
        p = jnp.exp2(s - jnp.max(s, axis=-1, keepdims=True))
        o = jnp.dot(p.astype(_BF16), v_ref[:, cols], preferred_element_type=_F32)
        o_ref[:, cols] = (o / jnp.sum(p, axis=-1, keepdims=True)).astype(o_ref.dtype)


def _cross_attention(q, kv, batch, seq, tq):
    nq = seq // tq
    return pl.pallas_call(
        _xattn_kernel,
        grid=(batch, nq),
        in_specs=[
            pl.BlockSpec((tq, D_MODEL), lambda b, i: (b * nq + i, 0)),
            pl.BlockSpec((N_MEM, D_MODEL), lambda b, i: (b, 0)),
            pl.BlockSpec((N_MEM, D_MODEL), lambda b, i: (b, 1)),
        ],
        out_specs=pl.BlockSpec((tq, D_MODEL), lambda b, i: (b * nq + i, 0)),
        out_shape=jax.ShapeDtypeStruct((batch * seq, D_MODEL), _BF16),
        compiler_params=_params(("parallel", "arbitrary")),
        name="cross_attention",
    )(q, kv, kv)


def _visible_block_counts(positions, tq, tk, strict):
    batch, seq = positions.shape
    qmin = positions.reshape(batch, seq // tq, tq).min(-1)[:, :, None]
    qmax = positions.reshape(batch, seq // tq, tq).max(-1)[:, :, None]
    kmin = positions.reshape(batch, seq // tk, tk).min(-1)[:, None, :]
    kmax = positions.reshape(batch, seq // tk, tk).max(-1)[:, None, :]
    full = (kmax < qmin) if strict else (kmax <= qmin)
    some = (kmin < qmax) if strict else (kmin <= qmax)
    n_full = jnp.sum(full, axis=-1).astype(jnp.int32).reshape(-1)
    n_vis = jnp.sum(some, axis=-1).astype(jnp.int32).reshape(-1)
    return n_full, jnp.maximum(n_vis, n_full)


def _in_proj_col_scale():
    scale = np.ones((1, IN_COLS), np.float32)
    scale[:, DIFF_Q_BLK * HEAD_COLS:(DIFF_Q_BLK + DIFF_HEADS) * HEAD_COLS] = DIFF_QK_DIM ** -0.5 * LOG2E
    scale[:, SB_Q_BLK * HEAD_COLS:(SB_Q_BLK + SB_HEADS) * HEAD_COLS] = SB_DIM ** -0.5 * LOG2E
    return jnp.asarray(scale)


def kernel(x, mem, positions, norm_mix, w_in, lam_q1, lam_k1, lam_q2, lam_k2, subln_diff, subln_sb, w_out,
           norm_x, norm_mem, wq_x, wkv_x, wo_x, norm_mlp, w_up, w_down, norm_final):
    batch, seq, d = x.shape
    depth = w_in.shape[0]
    tokens = batch * seq
    assert d == D_MODEL and w_in.shape[2] == IN_COLS
    assert seq % DIFF_TQ == 0 and seq % DIFF_TK == 0 and seq % SB_TQ == 0 and seq % SB_TK == 0

    pos_f = positions.astype(_F32)
    pos_col = pos_f.reshape(tokens, 1)
    pos_row = pos_f.reshape(batch, 1, seq)
    diff_counts = _visible_block_counts(positions, DIFF_TQ, DIFF_TK, strict=False)
    sb_counts = _visible_block_counts(positions, SB_TQ, SB_TK, strict=True)
    in_scale = _in_proj_col_scale()
    xq_scale = jnp.full((1, d), X_DIM ** -0.5 * LOG2E, _F32)

    h = x.reshape(tokens, d)
    mem2 = mem.reshape(batch * N_MEM, d)
    u = _rmsnorm_bf16(h, norm_mix[0], tm=512)
    out = None
    for l in range(depth):
        lambda_init = 0.8 - 0.6 * math.exp(-0.3 * l)
        qkv = _proj(u, w_in[l].astype(_BF16), in_scale, tm=1024, tn=1024, name="in_proj")
        o_diff = _diff_attention(qkv, pos_col, pos_row, *diff_counts,
                                 (lam_q1[l], lam_k1[l], lam_q2[l], lam_k2[l]), subln_diff[l], lambda_init,
                                 batch, seq)
        o_sb = _sb_attention(qkv, pos_col, pos_row, *sb_counts, subln_sb[l], batch, seq)
        h, ux = _proj_res_norm([o_diff, o_sb], w_out[l].astype(_BF16), h, norm_x[l], tm=512, name="out_proj")

        mem_n = _rmsnorm_bf16(mem2, norm_mem[l], tm=256)
        kv = _proj(mem_n, wkv_x[l].astype(_BF16), None, tm=1024, tn=1024, name="xattn_kv_proj")
        qx = _proj(ux, wq_x[l].astype(_BF16), xq_scale, tm=1024, tn=1024, name="xattn_q_proj")
        ox = _cross_attention(qx, kv, batch, seq, tq=512)
        h, um = _proj_res_norm([ox], wo_x[l].astype(_BF16), h, norm_mlp[l], tm=512, name="xattn_o_proj")

        last = l == depth - 1
        g_next = norm_final if last else norm_mix[l + 1]
        h, y = _mlp(um, w_up[l].astype(_BF16), w_down[l].astype(_BF16), h, g_next,
                    _F32 if last else _BF16, tm=512, tf=1024)
        if last:
            out = y
        else:
            u = y
    return out.reshape(batch, seq, d)
```

```python
import functools
import math

import jax
import jax.numpy as jnp
import numpy as np
from jax import lax
from jax.experimental import pallas as pl
from jax.experimental.pallas import tpu as pltpu

D_MODEL = 2048
N_MEM = 256
DIFF_V_DIM = 128
DIFF_QK_DIM = 64
DIFF_HEADS = 8
SB_DIM = 128
SB_HEADS = 8
X_HEADS = 4
X_DIM = D_MODEL // X_HEADS
D_FF = 4 * D_MODEL
EPS = 1e-6
NEG_INF = -1e30
LOG2E = math.log2(math.e)

LANES = 128
HEAD_COLS = 128
DIFF_Q_BLK = 0
DIFF_K_BLK = DIFF_HEADS
DIFF_V_BLK = 2 * DIFF_HEADS
SB_Q_BLK = 3 * DIFF_HEADS
SB_K_BLK = SB_Q_BLK + SB_HEADS
SB_V_BLK = SB_K_BLK + SB_HEADS
IN_COLS = (3 * DIFF_HEADS + 3 * SB_HEADS) * HEAD_COLS

DIFF_TQ, DIFF_TK = 512, 512
SB_TQ, SB_TK = 512, 256
DIFF_HEADS_PER_STEP = 4
SB_HEADS_PER_STEP = 4
VMEM_LIMIT = 56 * 1024 * 1024

_BF16 = jnp.bfloat16
_F32 = jnp.float32
_NT = (((1,), (1,)), ((), ()))
_TN = (((0,), (0,)), ((), ()))


def _params(semantics):
    return pltpu.CompilerParams(dimension_semantics=semantics, vmem_limit_bytes=VMEM_LIMIT)


def _rms(x, g):
    return x * lax.rsqrt(jnp.mean(x * x, axis=-1, keepdims=True) + EPS) * g


def _lane_tiles(x):
    return [x[:, c * LANES:(c + 1) * LANES] for c in range(x.shape[1] // LANES)]


def _software_pipeline(stages, n_chains):
    for t in range(n_chains + len(stages) - 1):
        for s, stage in enumerate(stages):
            if 0 <= t - s < n_chains:
                stage(t - s)


def _rmsnorm_kernel(x_ref, g_ref, o_ref):
    o_ref[...] = _rms(x_ref[...], g_ref[...]).astype(o_ref.dtype)


def _rmsnorm_bf16(x, g, tm):
    m, d = x.shape
    return pl.pallas_call(
        _rmsnorm_kernel,
        grid=(m // tm,),
        in_specs=[pl.BlockSpec((tm, d), lambda i: (i, 0)), pl.BlockSpec((1, d), lambda i: (0, 0))],
        out_specs=pl.BlockSpec((tm, d), lambda i: (i, 0)),
        out_shape=jax.ShapeDtypeStruct((m, d), _BF16),
        compiler_params=_params(("parallel",)),
        name="rmsnorm_bf16",
    )(x, g.reshape(1, d))


def _proj_kernel(x_ref, w_ref, *rest):
    o_ref = rest[-1]
    acc = jnp.dot(x_ref[...], w_ref[...], preferred_element_type=_F32)
    if len(rest) == 2:
        acc = acc * rest[0][...]
    o_ref[...] = acc.astype(o_ref.dtype)


def _proj(x, w, col_scale, tm, tn, name):
    m, k = x.shape
    n = w.shape[1]
    in_specs = [pl.BlockSpec((tm, k), lambda i, j: (i, 0)), pl.BlockSpec((k, tn), lambda i, j: (0, j))]
    args = [x, w]
    if col_scale is not None:
        in_specs.append(pl.BlockSpec((1, tn), lambda i, j: (0, j)))
        args.append(col_scale)
    return pl.pallas_call(
        _proj_kernel,
        grid=(m // tm, n // tn),
        in_specs=in_specs,
        out_specs=pl.BlockSpec((tm, tn), lambda i, j: (i, j)),
        out_shape=jax.ShapeDtypeStruct((m, n), _BF16),
        compiler_params=_params(("parallel", "arbitrary")),
        name=name,
    )(*args)


def _proj_res_norm_kernel(*refs, n_in):
    xs = refs[:n_in]
    ws = refs[n_in:2 * n_in]
    h_ref, g_ref, h_out_ref, u_out_ref = refs[2 * n_in:]
    acc = h_ref[...]
    for x_ref, w_ref in zip(xs, ws):
        acc = acc + jnp.dot(x_ref[...], w_ref[...], preferred_element_type=_F32)
    h_out_ref[...] = acc
    u_out_ref[...] = _rms(acc, g_ref[...]).astype(u_out_ref.dtype)


def _proj_res_norm(xs, w, h, g, tm, name):
    m, d = h.shape
    n_in = len(xs)
    kx = xs[0].shape[1]
    in_specs = [pl.BlockSpec((tm, kx), lambda i: (i, 0)) for _ in xs]
    in_specs += [pl.BlockSpec((kx, d), lambda i, r=r: (r, 0)) for r in range(n_in)]
    in_specs += [pl.BlockSpec((tm, d), lambda i: (i, 0)), pl.BlockSpec((1, d), lambda i: (0, 0))]
    return pl.pallas_call(
        functools.partial(_proj_res_norm_kernel, n_in=n_in),
        grid=(m // tm,),
        in_specs=in_specs,
        out_specs=[pl.BlockSpec((tm, d), lambda i: (i, 0)), pl.BlockSpec((tm, d), lambda i: (i, 0))],
        out_shape=[jax.ShapeDtypeStruct((m, d), _F32), jax.ShapeDtypeStruct((m, d), _BF16)],
        compiler_params=_params(("parallel",)),
        name=name,
    )(*xs, *([w] * n_in), h, g.reshape(1, d))


def _mlp_kernel(u_ref, wu_ref, wd_ref, h_ref, g_ref, h_out_ref, y_out_ref):
    f = pl.program_id(1)
    up = jnp.dot(u_ref[...], wu_ref[...], preferred_element_type=_F32)
    a = jnp.square(jnp.maximum(up, 0.0)).astype(_BF16)
    part = jnp.dot(a, wd_ref[...], preferred_element_type=_F32)

    @pl.when(f == 0)
    def _():
        h_out_ref[...] = h_ref[...] + part

    @pl.when(f > 0)
    def _():
        h_out_ref[...] += part

    @pl.when(f == pl.num_programs(1) - 1)
    def _():
        y_out_ref[...] = _rms(h_out_ref[...], g_ref[...]).astype(y_out_ref.dtype)


def _mlp(u, w_up, w_down, h, g, y_dtype, tm, tf):
    m, d = h.shape
    dff = w_up.shape[1]
    return pl.pallas_call(
        _mlp_kernel,
        grid=(m // tm, dff // tf),
        in_specs=[
            pl.BlockSpec((tm, d), lambda i, f: (i, 0)),
            pl.BlockSpec((d, tf), lambda i, f: (0, f)),
            pl.BlockSpec((tf, d), lambda i, f: (f, 0)),
            pl.BlockSpec((tm, d), lambda i, f: (i, 0)),
            pl.BlockSpec((1, d), lambda i, f: (0, 0)),
        ],
        out_specs=[pl.BlockSpec((tm, d), lambda i, f: (i, 0)), pl.BlockSpec((tm, d), lambda i, f: (i, 0))],
        out_shape=[jax.ShapeDtypeStruct((m, d), _F32), jax.ShapeDtypeStruct((m, d), y_dtype)],
        compiler_params=_params(("parallel", "arbitrary")),
        name="mlp_relu2",
    )(u, w_up, w_down, h, g.reshape(1, d))


def _diff_attn_kernel(nfull_ref, nvis_ref, slopes_ref,
                      q_ref, k_ref, v_ref, pq_ref, pk_ref, lq1_ref, lk1_ref, lq2_ref, lk2_ref, g_ref,
                      o_ref, m_s, l_s, acc_s, s_s, *, lambda_init, tq, tk, hp):
    b, hg, i = pl.program_id(0), pl.program_id(1), pl.program_id(2)
    nq = pl.num_programs(2)
    n_full = nfull_ref[b * nq + i]
    n_vis = nvis_ref[b * nq + i]
    n_lane_tiles = tq // LANES
    heads = [slice(hh * HEAD_COLS, (hh + 1) * HEAD_COLS) for hh in range(hp)]

    lane = lax.broadcasted_iota(jnp.int32, (tq, HEAD_COLS), 1)
    zero = jnp.zeros((tq, HEAD_COLS), _BF16)
    q_maps = []
    for cols in heads:
        q = q_ref[:, cols]
        q_maps += [jnp.where(lane < DIFF_QK_DIM, q, zero), jnp.where(lane >= DIFF_QK_DIM, q, zero)]
    slope2 = [slopes_ref[hg * hp + hh] * LOG2E for hh in range(hp)]
    pq = pq_ref[0]
    pq0 = pq[:, 0:1]

    m_s[...] = jnp.full(m_s.shape, NEG_INF, _F32)
    l_s[...] = jnp.zeros(l_s.shape, _F32)
    acc_s[...] = jnp.zeros(acc_s.shape, _F32)

    def step(j, masked):
        start = pl.multiple_of(j * tk, tk)
        pk = pk_ref[pl.ds(start, tk), :]
        rel = pk - pq0
        if masked:
            visible = jnp.concatenate([pq_c >= pk for pq_c in _lane_tiles(pq)], axis=1)

        def scores(ci):
            cols = heads[ci // 2]
            k = k_ref[pl.ds(start, tk), cols]
            key_bias = jnp.concatenate([slope2[ci // 2] * rel] * n_lane_tiles, axis=1)
            s = lax.dot_general(k, q_maps[ci], _NT, preferred_element_type=_F32) + key_bias
            if masked:
                s = jnp.where(visible, s, NEG_INF)
            s_s[ci] = s

        def softmax_pv(ci):
            v = v_ref[pl.ds(start, tk), heads[ci // 2]]
            s = s_s[ci]
            m_prev = m_s[ci]
            m_new = jnp.maximum(m_prev, jnp.max(s, axis=0, keepdims=True))
            alpha = jnp.exp2(m_prev - m_new)
            p = jnp.exp2(s - m_new)
            l_s[ci] = alpha * l_s[ci] + jnp.sum(p, axis=0, keepdims=True)
            pv = lax.dot_general(v, p.astype(_BF16), _TN, preferred_element_type=_F32)
            acc_s[ci] = alpha * acc_s[ci] + pv
            m_s[ci] = m_new

        _software_pipeline((scores, softmax_pv), 2 * hp)

    def full_body(j, c):
        step(j, False)
        return c

    def masked_body(j, c):
        step(j, True)
        return c

    lax.fori_loop(0, n_full, full_body, 0)
    lax.fori_loop(n_full, n_vis, masked_body, 0)

    lam = (jnp.exp(jnp.sum(lq1_ref[...] * lk1_ref[...], axis=-1, keepdims=True))
           - jnp.exp(jnp.sum(lq2_ref[...] * lk2_ref[...], axis=-1, keepdims=True)) + lambda_init)
    for hh, cols in enumerate(heads):
        o_t = acc_s[2 * hh] / l_s[2 * hh] - lam * (acc_s[2 * hh + 1] / l_s[2 * hh + 1])
        o_ref[:, cols] = (_rms(o_t.T, g_ref[...]) * (1.0 - lambda_init)).astype(o_ref.dtype)


def _diff_attention(qkv, pos_row, pos_rep, n_full, n_vis, lam_params, g, lambda_init, batch, seq):
    tq, tk, hp = DIFF_TQ, DIFF_TK, DIFF_HEADS_PER_STEP
    nq = seq // tq
    slopes = jnp.asarray(np.array([2.0 ** (-8.0 * (i + 1) / DIFF_HEADS) for i in range(DIFF_HEADS)], np.float32))
    small = lambda b, h, i, *_: (0, 0)
    grid_spec = pltpu.PrefetchScalarGridSpec(
        num_scalar_prefetch=3,
        grid=(batch, DIFF_HEADS // hp, nq),
        in_specs=[
            pl.BlockSpec((tq, hp * HEAD_COLS), lambda b, h, i, *_: (b * nq + i, DIFF_Q_BLK // hp + h)),
            pl.BlockSpec((seq, hp * HEAD_COLS), lambda b, h, i, *_: (b, DIFF_K_BLK // hp + h)),
            pl.BlockSpec((seq, hp * HEAD_COLS), lambda b, h, i, *_: (b, DIFF_V_BLK // hp + h)),
            pl.BlockSpec((1, 1, tq), lambda b, h, i, *_: (b * nq + i, 0, 0)),
            pl.BlockSpec((seq, LANES), lambda b, h, i, *_: (b, 0)),
            pl.BlockSpec((1, DIFF_QK_DIM), small), pl.BlockSpec((1, DIFF_QK_DIM), small),
            pl.BlockSpec((1, DIFF_QK_DIM), small), pl.BlockSpec((1, DIFF_QK_DIM), small),
            pl.BlockSpec((1, DIFF_V_DIM), small),
        ],
        out_specs=pl.BlockSpec((tq, hp * HEAD_COLS), lambda b, h, i, *_: (b * nq + i, h)),
        scratch_shapes=[pltpu.VMEM((2 * hp, 1, tq), _F32), pltpu.VMEM((2 * hp, 1, tq), _F32),
                        pltpu.VMEM((2 * hp, DIFF_V_DIM, tq), _F32), pltpu.VMEM((2 * hp, tk, tq), _F32)],
    )
    return pl.pallas_call(
        functools.partial(_diff_attn_kernel, lambda_init=lambda_init, tq=tq, tk=tk, hp=hp),
        grid_spec=grid_spec,
        out_shape=jax.ShapeDtypeStruct((batch * seq, DIFF_HEADS * DIFF_V_DIM), _BF16),
        compiler_params=_params(("parallel", "parallel", "arbitrary")),
        name="diff_attention",
    )(n_full, n_vis, slopes, qkv, qkv, qkv, pos_row.reshape(batch * nq, 1, tq), pos_rep,
      *[p.reshape(1, DIFF_QK_DIM) for p in lam_params], g.reshape(1, DIFF_V_DIM))


def _sb_attn_kernel(nfull_ref, nvis_ref, q_ref, k_ref, v_ref, pq_ref, pk_ref, ntri_ref, g_ref,
                    o_ref, c_s, acc_s, z_s, e_s, *, tq, tk, hp):
    b, i = pl.program_id(0), pl.program_id(2)
    nq = pl.num_programs(2)
    n_full = nfull_ref[b * nq + i]
    n_vis = nvis_ref[b * nq + i]
    heads = [slice(hh * HEAD_COLS, (hh + 1) * HEAD_COLS) for hh in range(hp)]
    qs = [q_ref[:, cols] for cols in heads]
    pq = pq_ref[0]
    ntri = ntri_ref[...]

    c_s[...] = jnp.zeros(c_s.shape, _F32)
    acc_s[...] = jnp.zeros(acc_s.shape, _F32)

    def step(j, masked):
        start = pl.multiple_of(j * tk, tk)
        if masked:
            pk = pk_ref[pl.ds(start, tk), :]
            strict = jnp.concatenate([pq_c > pk for pq_c in _lane_tiles(pq)], axis=1)

        def logits(hh):
            k = k_ref[pl.ds(start, tk), heads[hh]]
            z_s[hh] = lax.dot_general(k, qs[hh], _NT, preferred_element_type=_F32)

        def log_weights(hh):
            z = z_s[hh]
            softplus = jnp.maximum(z, 0.0) + jnp.log2(1.0 + jnp.exp2(-jnp.abs(z)))
            log_sig = z - softplus
            if masked:
                softplus = jnp.where(strict, softplus, 0.0)
            hi = softplus.astype(_BF16)
            lo = (softplus - hi.astype(_F32)).astype(_BF16)
            later = (jnp.dot(ntri, hi, preferred_element_type=_F32)
                     + jnp.dot(ntri, lo, preferred_element_type=_F32))
            c = c_s[hh]
            e = log_sig + (later + c)
            if masked:
                e = jnp.where(strict, e, NEG_INF)
            e_s[hh] = e
            c_s[hh] = c - jnp.sum(softplus, axis=0, keepdims=True)

        def weighted_values(hh):
            v = v_ref[pl.ds(start, tk), heads[hh]]
            a = jnp.exp2(e_s[hh]).astype(_BF16)
            acc_s[hh] += lax.dot_general(v, a, _TN, preferred_element_type=_F32)

        _software_pipeline((logits, log_weights, weighted_values), hp)

    def rev(lo_j, hi_j, masked):
        def body(t, c):
            step(hi_j - 1 - t, masked)
            return c
        lax.fori_loop(0, hi_j - lo_j, body, 0)

    rev(n_full, n_vis, True)
    rev(0, n_full, False)
    for hh, cols in enumerate(heads):
        o_ref[:, cols] = _rms(acc_s[hh].T, g_ref[...]).astype(o_ref.dtype)


def _sb_attention(qkv, pos_row, pos_rep, n_full, n_vis, g, batch, seq):
    tq, tk, hp = SB_TQ, SB_TK, SB_HEADS_PER_STEP
    nq = seq // tq
    idx = np.arange(tk)
    ntri = jnp.asarray(-(idx[None, :] > idx[:, None]).astype(np.float32), _BF16)
    grid_spec = pltpu.PrefetchScalarGridSpec(
        num_scalar_prefetch=2,
        grid=(batch, SB_HEADS // hp, nq),
        in_specs=[
            pl.BlockSpec((tq, hp * HEAD_COLS), lambda b, h, i, *_: (b * nq + i, SB_Q_BLK // hp + h)),
            pl.BlockSpec((seq, hp * HEAD_COLS), lambda b, h, i, *_: (b, SB_K_BLK // hp + h)),
            pl.BlockSpec((seq, hp * HEAD_COLS), lambda b, h, i, *_: (b, SB_V_BLK // hp + h)),
            pl.BlockSpec((1, 1, tq), lambda b, h, i, *_: (b * nq + i, 0, 0)),
            pl.BlockSpec((seq, LANES), lambda b, h, i, *_: (b, 0)),
            pl.BlockSpec((tk, tk), lambda b, h, i, *_: (0, 0)),
            pl.BlockSpec((1, SB_DIM), lambda b, h, i, *_: (0, 0)),
        ],
        out_specs=pl.BlockSpec((tq, hp * HEAD_COLS), lambda b, h, i, *_: (b * nq + i, h)),
        scratch_shapes=[pltpu.VMEM((hp, 1, tq), _F32), pltpu.VMEM((hp, SB_DIM, tq), _F32),
                        pltpu.VMEM((hp, tk, tq), _F32), pltpu.VMEM((hp, tk, tq), _F32)],
    )
    return pl.pallas_call(
        functools.partial(_sb_attn_kernel, tq=tq, tk=tk, hp=hp),
        grid_spec=grid_spec,
        out_shape=jax.ShapeDtypeStruct((batch * seq, SB_HEADS * SB_DIM), _BF16),
        compiler_params=_params(("parallel", "parallel", "arbitrary")),
        name="stick_breaking_attention",
    )(n_full, n_vis, qkv, qkv, qkv, pos_row.reshape(batch * nq, 1, tq), pos_rep, ntri, g.reshape(1, SB_DIM))


def _xattn_kernel(q_ref, k_ref, v_ref, o_ref):
    for hd in range(X_HEADS):
        cols = slice(hd * X_DIM, (hd + 1) * X_DIM)
        s = lax.dot_general(q_ref[:, cols], k_ref[:, cols], _NT, preferred_element_type=_F32)
        p = jnp.exp2(s - jnp.max(s, axis=-1, keepdims=True))
        o = jnp.dot(p.astype(_BF16), v_ref[:, cols], preferred_element_type=_F32)
        o_ref[:, cols] = (o / jnp.sum(p, axis=-1, keepdims=True)).astype(o_ref.dtype)


def _cross_attention(q, kv, batch, seq, tq):
    nq = seq // tq
    return pl.pallas_call(
        _xattn_kernel,
        grid=(batch, nq),
        in_specs=[
            pl.BlockSpec((tq, D_MODEL), lambda b, i: (b * nq + i, 0)),
            pl.BlockSpec((N_MEM, D_MODEL), lambda b, i: (b, 0)),
            pl.BlockSpec((N_MEM, D_MODEL), lambda b, i: (b, 1)),
        ],
        out_specs=pl.BlockSpec((tq, D_MODEL), lambda b, i: (b * nq + i, 0)),
        out_shape=jax.ShapeDtypeStruct((batch * seq, D_MODEL), _BF16),
        compiler_params=_params(("parallel", "arbitrary")),
        name="cross_attention",
    )(q, kv, kv)


def _visible_block_counts(positions, tq, tk, strict):
    batch, seq = positions.shape
    qmin = positions.reshape(batch, seq // tq, tq).min(-1)[:, :, None]
    qmax = positions.reshape(batch, seq // tq, tq).max(-1)[:, :, None]
    kmin = positions.reshape(batch, seq // tk, tk).min(-1)[:, None, :]
    kmax = positions.reshape(batch, seq // tk, tk).max(-1)[:, None, :]
    full = (kmax < qmin) if strict else (kmax <= qmin)
    some = (kmin < qmax) if strict else (kmin <= qmax)
    n_full = jnp.sum(full, axis=-1).astype(jnp.int32).reshape(-1)
    n_vis = jnp.sum(some, axis=-1).astype(jnp.int32).reshape(-1)
    return n_full, jnp.maximum(n_vis, n_full)


def _in_proj_col_scale():
    scale = np.ones((1, IN_COLS), np.float32)
    scale[:, DIFF_Q_BLK * HEAD_COLS:(DIFF_Q_BLK + DIFF_HEADS) * HEAD_COLS] = DIFF_QK_DIM ** -0.5 * LOG2E
    scale[:, SB_Q_BLK * HEAD_COLS:(SB_Q_BLK + SB_HEADS) * HEAD_COLS] = SB_DIM ** -0.5 * LOG2E
    return jnp.asarray(scale)


def kernel(x, mem, positions, norm_mix, w_in, lam_q1, lam_k1, lam_q2, lam_k2, subln_diff, subln_sb, w_out,
           norm_x, norm_mem, wq_x, wkv_x, wo_x, norm_mlp, w_up, w_down, norm_final):
    batch, seq, d = x.shape
    depth = w_in.shape[0]
    tokens = batch * seq
    assert d == D_MODEL and w_in.shape[2] == IN_COLS
    assert seq % DIFF_TQ == 0 and seq % DIFF_TK == 0 and seq % SB_TQ == 0 and seq % SB_TK == 0

    pos_f = positions.astype(_F32)
    pos_row = pos_f.reshape(batch, seq)
    pos_rep = jnp.broadcast_to(pos_f.reshape(tokens, 1), (tokens, LANES))
    diff_counts = _visible_block_counts(positions, DIFF_TQ, DIFF_TK, strict=False)
    sb_counts = _visible_block_counts(positions, SB_TQ, SB_TK, strict=True)
    in_scale = _in_proj_col_scale()
    xq_scale = jnp.full((1, d), X_DIM ** -0.5 * LOG2E, _F32)

    h = x.reshape(tokens, d)
    mem2 = mem.reshape(batch * N_MEM, d)
    u = _rmsnorm_bf16(h, norm_mix[0], tm=512)
    out = None
    for l in range(depth):
        lambda_init = 0.8 - 0.6 * math.exp(-0.3 * l)
        qkv = _proj(u, w_in[l].astype(_BF16), in_scale, tm=1024, tn=1024, name="in_proj")
        o_diff = _diff_attention(qkv, pos_row, pos_rep, *diff_counts,
                                 (lam_q1[l], lam_k1[l], lam_q2[l], lam_k2[l]), subln_diff[l], lambda_init,
                                 batch, seq)
        o_sb = _sb_attention(qkv, pos_row, pos_rep, *sb_counts, subln_sb[l], batch, seq)
        h, ux = _proj_res_norm([o_diff, o_sb], w_out[l].astype(_BF16), h, norm_x[l], tm=512, name="out_proj")

        mem_n = _rmsnorm_bf16(mem2, norm_mem[l], tm=256)
        kv = _proj(mem_n, wkv_x[l].astype(_BF16), None, tm=1024, tn=1024, name="xattn_kv_proj")
        qx = _proj(ux, wq_x[l].astype(_BF16), xq_scale, tm=1024, tn=1024, name="xattn_q_proj")
        ox = _cross_attention(qx, kv, batch, seq, tq=512)
        h, um = _proj_res_norm([ox], wo_x[l].astype(_BF16), h, norm_mlp[l], tm=512, name="xattn_o_proj")

        last = l == depth - 1
        g_next = norm_final if last else norm_mix[l + 1]
        h, y = _mlp(um, w_up[l].astype(_BF16), w_down[l].astype(_BF16), h, g_next,
                    _F32 if last else _BF16, tm=512, tf=1024)
        if last:
            out = y
        else:
            u = y
    return out.reshape(batch, seq, d)
```

```python
import functools
import math

import jax
import jax.numpy as jnp
import numpy as np
from jax import lax
from jax.experimental import pallas as pl
from jax.experimental.pallas import tpu as pltpu

D_MODEL = 2048
N_MEM = 256
DIFF_V_DIM = 128
DIFF_QK_DIM = 64
DIFF_HEADS = 8
SB_DIM = 128
SB_HEADS = 8
X_HEADS = 4
X_DIM = D_MODEL // X_HEADS
D_FF = 4 * D_MODEL
EPS = 1e-6
NEG_INF = -1e30
LOG2E = math.log2(math.e)

LANES = 128
HEAD_COLS = 128
DIFF_Q_BLK = 0
DIFF_K_BLK = DIFF_HEADS
DIFF_V_BLK = 2 * DIFF_HEADS
SB_Q_BLK = 3 * DIFF_HEADS
SB_K_BLK = SB_Q_BLK + SB_HEADS
SB_V_BLK = SB_K_BLK + SB_HEADS
IN_COLS = (3 * DIFF_HEADS + 3 * SB_HEADS) * HEAD_COLS

DIFF_TQ, DIFF_TK = 512, 512
SB_TQ, SB_TK = 512, 256
DIFF_HEADS_PER_STEP = 4
SB_HEADS_PER_STEP = 8
VMEM_LIMIT = 56 * 1024 * 1024

_BF16 = jnp.bfloat16
_F32 = jnp.float32
_NT = (((1,), (1,)), ((), ()))
_TN = (((0,), (0,)), ((), ()))


def _params(semantics):
    return pltpu.CompilerParams(dimension_semantics=semantics, vmem_limit_bytes=VMEM_LIMIT)


def _rms(x, g):
    return x * lax.rsqrt(jnp.mean(x * x, axis=-1, keepdims=True) + EPS) * g


def _lane_tiles(x):
    return [x[:, c * LANES:(c + 1) * LANES] for c in range(x.shape[1] // LANES)]


def _software_pipeline(stages, n_chains):
    for t in range(n_chains + len(stages) - 1):
        for s, stage in enumerate(stages):
            if 0 <= t - s < n_chains:
                stage(t - s)


def _rmsnorm_kernel(x_ref, g_ref, o_ref):
    o_ref[...] = _rms(x_ref[...], g_ref[...]).astype(o_ref.dtype)


def _rmsnorm_bf16(x, g, tm):
    m, d = x.shape
    return pl.pallas_call(
        _rmsnorm_kernel,
        grid=(m // tm,),
        in_specs=[pl.BlockSpec((tm, d), lambda i: (i, 0)), pl.BlockSpec((1, d), lambda i: (0, 0))],
        out_specs=pl.BlockSpec((tm, d), lambda i: (i, 0)),
        out_shape=jax.ShapeDtypeStruct((m, d), _BF16),
        compiler_params=_params(("parallel",)),
        name="rmsnorm_bf16",
    )(x, g.reshape(1, d))


def _proj_kernel(x_ref, w_ref, *rest):
    o_ref = rest[-1]
    acc = jnp.dot(x_ref[...], w_ref[...].astype(_BF16), preferred_element_type=_F32)
    if len(rest) == 2:
        acc = acc * rest[0][...]
    o_ref[...] = acc.astype(o_ref.dtype)


def _proj(x, w_stack, layer, col_scale, tm, tn, name):
    m, k = x.shape
    n = w_stack.shape[2]
    in_specs = [pl.BlockSpec((tm, k), lambda j, i: (i, 0)), pl.BlockSpec((None, k, tn), lambda j, i: (layer, 0, j))]
    args = [x, w_stack]
    if col_scale is not None:
        in_specs.append(pl.BlockSpec((1, tn), lambda j, i: (0, j)))
        args.append(col_scale)
    return pl.pallas_call(
        _proj_kernel,
        grid=(n // tn, m // tm),
        in_specs=in_specs,
        out_specs=pl.BlockSpec((tm, tn), lambda j, i: (i, j)),
        out_shape=jax.ShapeDtypeStruct((m, n), _BF16),
        compiler_params=_params(("parallel", "arbitrary")),
        name=name,
    )(*args)


def _proj_res_norm_kernel(*refs, n_in):
    xs = refs[:n_in]
    ws = refs[n_in:2 * n_in]
    h_ref, g_ref, h_out_ref, u_out_ref = refs[2 * n_in:]
    acc = h_ref[...]
    for x_ref, w_ref in zip(xs, ws):
        acc = acc + jnp.dot(x_ref[...], w_ref[...], preferred_element_type=_F32)
    h_out_ref[...] = acc
    u_out_ref[...] = _rms(acc, g_ref[...]).astype(u_out_ref.dtype)


def _proj_res_norm(xs, w_stack, layer, h, g, tm, name):
    m, d = h.shape
    n_in = len(xs)
    kx = xs[0].shape[1]
    in_specs = [pl.BlockSpec((tm, kx), lambda i: (i, 0)) for _ in xs]
    in_specs += [pl.BlockSpec((None, kx, d), lambda i, r=r: (layer, r, 0)) for r in range(n_in)]
    in_specs += [pl.BlockSpec((tm, d), lambda i: (i, 0)), pl.BlockSpec((1, d), lambda i: (0, 0))]
    return pl.pallas_call(
        functools.partial(_proj_res_norm_kernel, n_in=n_in),
        grid=(m // tm,),
        in_specs=in_specs,
        out_specs=[pl.BlockSpec((tm, d), lambda i: (i, 0)), pl.BlockSpec((tm, d), lambda i: (i, 0))],
        out_shape=[jax.ShapeDtypeStruct((m, d), _F32), jax.ShapeDtypeStruct((m, d), _BF16)],
        compiler_params=_params(("parallel",)),
        name=name,
    )(*xs, *([w_stack] * n_in), h, g.reshape(1, d))


def _mlp_kernel(u_ref, wu_ref, wd_ref, h_ref, g_ref, h_out_ref, y_out_ref):
    f = pl.program_id(1)

    @pl.when(f == 0)
    def _():
        h_out_ref[...] = h_ref[...]

    up = jnp.dot(u_ref[...], wu_ref[...], preferred_element_type=_F32)
    a = jnp.square(jnp.maximum(up, 0.0)).astype(_BF16)
    h_out_ref[...] += jnp.dot(a, wd_ref[...], preferred_element_type=_F32)

    @pl.when(f == pl.num_programs(1) - 1)
    def _():
        y_out_ref[...] = _rms(h_out_ref[...], g_ref[...]).astype(y_out_ref.dtype)


def _mlp(u, w_up_stack, w_down_stack, layer, h, g, y_dtype, tm, tf):
    m, d = h.shape
    dff = w_up_stack.shape[2]
    return pl.pallas_call(
        _mlp_kernel,
        grid=(m // tm, dff // tf),
        in_specs=[
            pl.BlockSpec((tm, d), lambda i, f: (i, 0)),
            pl.BlockSpec((None, d, tf), lambda i, f: (layer, 0, f)),
            pl.BlockSpec((None, tf, d), lambda i, f: (layer, f, 0)),
            pl.BlockSpec((tm, d), lambda i, f: (i, 0)),
            pl.BlockSpec((1, d), lambda i, f: (0, 0)),
        ],
        out_specs=[pl.BlockSpec((tm, d), lambda i, f: (i, 0)), pl.BlockSpec((tm, d), lambda i, f: (i, 0))],
        out_shape=[jax.ShapeDtypeStruct((m, d), _F32), jax.ShapeDtypeStruct((m, d), y_dtype)],
        compiler_params=_params(("parallel", "arbitrary")),
        name="mlp_relu2",
    )(u, w_up_stack, w_down_stack, h, g.reshape(1, d))


def _diff_attn_kernel(nfull_ref, nvis_ref, slopes_ref,
                      q_ref, k_ref, v_ref, pq_ref, pk_ref, lq1_ref, lk1_ref, lq2_ref, lk2_ref, g_ref,
                      o_ref, m_s, l_s, acc_s, s_s, *, lambda_init, tq, tk, hp):
    b, hg, i = pl.program_id(0), pl.program_id(1), pl.program_id(2)
    nq = pl.num_programs(2)
    n_full = nfull_ref[b * nq + i]
    n_vis = nvis_ref[b * nq + i]
    n_lane_tiles = tq // LANES
    heads = [slice(hh * HEAD_COLS, (hh + 1) * HEAD_COLS) for hh in range(hp)]

    lane = lax.broadcasted_iota(jnp.int32, (tq, HEAD_COLS), 1)
    zero = jnp.zeros((tq, HEAD_COLS), _BF16)
    q_maps = []
    for cols in heads:
        q = q_ref[:, cols]
        q_maps += [jnp.where(lane < DIFF_QK_DIM, q, zero), jnp.where(lane >= DIFF_QK_DIM, q, zero)]
    slope2 = [slopes_ref[hg * hp + hh] * LOG2E for hh in range(hp)]
    pq = pq_ref[0]
    pq0 = pq[:, 0:1]

    m_s[...] = jnp.full(m_s.shape, NEG_INF, _F32)
    l_s[...] = jnp.zeros(l_s.shape, _F32)
    acc_s[...] = jnp.zeros(acc_s.shape, _F32)

    def step(j, masked):
        start = pl.multiple_of(j * tk, tk)
        pk = pk_ref[pl.ds(start, tk), :]
        rel = pk - pq0
        bias = [jnp.concatenate([slope2[hh] * rel] * n_lane_tiles, axis=1) for hh in range(hp)]
        if masked:
            penalty = jnp.concatenate([jnp.where(pq_c >= pk, 0.0, NEG_INF) for pq_c in _lane_tiles(pq)], axis=1)
            bias = [b_h + penalty for b_h in bias]

        def scores(ci):
            k = k_ref[pl.ds(start, tk), heads[ci // 2]]
            s_s[ci] = lax.dot_general(k, q_maps[ci], _NT, preferred_element_type=_F32) + bias[ci // 2]

        def softmax_pv(ci):
            v = v_ref[pl.ds(start, tk), heads[ci // 2]]
            s = s_s[ci]
            m_prev = m_s[ci]
            m_new = jnp.maximum(m_prev, jnp.max(s, axis=0, keepdims=True))
            alpha = jnp.exp2(m_prev - m_new)
            p = jnp.exp2(s - m_new)
            l_s[ci] = alpha * l_s[ci] + jnp.sum(p, axis=0, keepdims=True)
            pv = lax.dot_general(v, p.astype(_BF16), _TN, preferred_element_type=_F32)
            acc_s[ci] = alpha * acc_s[ci] + pv
            m_s[ci] = m_new

        _software_pipeline((scores, softmax_pv), 2 * hp)

    def full_body(j, c):
        step(j, False)
        return c

    def masked_body(j, c):
        step(j, True)
        return c

    lax.fori_loop(0, n_full, full_body, 0)
    lax.fori_loop(n_full, n_vis, masked_body, 0)

    lam = (jnp.exp(jnp.sum(lq1_ref[...] * lk1_ref[...], axis=-1, keepdims=True))
           - jnp.exp(jnp.sum(lq2_ref[...] * lk2_ref[...], axis=-1, keepdims=True)) + lambda_init)
    for hh, cols in enumerate(heads):
        o_t = acc_s[2 * hh] / l_s[2 * hh] - lam * (acc_s[2 * hh + 1] / l_s[2 * hh + 1])
        o_ref[:, cols] = (_rms(o_t.T, g_ref[...]) * (1.0 - lambda_init)).astype(o_ref.dtype)


def _diff_attention(qkv, pos_row, pos_rep, n_full, n_vis, lam_params, g, lambda_init, batch, seq):
    tq, tk, hp = DIFF_TQ, DIFF_TK, DIFF_HEADS_PER_STEP
    nq = seq // tq
    slopes = jnp.asarray(np.array([2.0 ** (-8.0 * (i + 1) / DIFF_HEADS) for i in range(DIFF_HEADS)], np.float32))
    small = lambda b, h, i, *_: (0, 0)
    grid_spec = pltpu.PrefetchScalarGridSpec(
        num_scalar_prefetch=3,
        grid=(batch, DIFF_HEADS // hp, nq),
        in_specs=[
            pl.BlockSpec((tq, hp * HEAD_COLS), lambda b, h, i, *_: (b * nq + i, DIFF_Q_BLK // hp + h)),
            pl.BlockSpec((seq, hp * HEAD_COLS), lambda b, h, i, *_: (b, DIFF_K_BLK // hp + h)),
            pl.BlockSpec((seq, hp * HEAD_COLS), lambda b, h, i, *_: (b, DIFF_V_BLK // hp + h)),
            pl.BlockSpec((1, 1, tq), lambda b, h, i, *_: (b * nq + i, 0, 0)),
            pl.BlockSpec((seq, LANES), lambda b, h, i, *_: (b, 0)),
            pl.BlockSpec((1, DIFF_QK_DIM), small), pl.BlockSpec((1, DIFF_QK_DIM), small),
            pl.BlockSpec((1, DIFF_QK_DIM), small), pl.BlockSpec((1, DIFF_QK_DIM), small),
            pl.BlockSpec((1, DIFF_V_DIM), small),
        ],
        out_specs=pl.BlockSpec((tq, hp * HEAD_COLS), lambda b, h, i, *_: (b * nq + i, h)),
        scratch_shapes=[pltpu.VMEM((2 * hp, 1, tq), _F32), pltpu.VMEM((2 * hp, 1, tq), _F32),
                        pltpu.VMEM((2 * hp, DIFF_V_DIM, tq), _F32), pltpu.VMEM((2 * hp, tk, tq), _F32)],
    )
    return pl.pallas_call(
        functools.partial(_diff_attn_kernel, lambda_init=lambda_init, tq=tq, tk=tk, hp=hp),
        grid_spec=grid_spec,
        out_shape=jax.ShapeDtypeStruct((batch * seq, DIFF_HEADS * DIFF_V_DIM), _BF16),
        compiler_params=_params(("parallel", "parallel", "arbitrary")),
        name="diff_attention",
    )(n_full, n_vis, slopes, qkv, qkv, qkv, pos_row.reshape(batch * nq, 1, tq), pos_rep,
      *[p.reshape(1, DIFF_QK_DIM) for p in lam_params], g.reshape(1, DIFF_V_DIM))


def _sb_attn_kernel(nfull_ref, nvis_ref, q_ref, k_ref, v_ref, pq_ref, pk_ref, ntri_ref, g_ref,
                    o_ref, c_s, w_s, acc_s, z_s, e_s, *, tq, tk, hp):
    b, i = pl.program_id(0), pl.program_id(2)
    nq = pl.num_programs(2)
    n_full = nfull_ref[b * nq + i]
    n_vis = nvis_ref[b * nq + i]
    heads = [slice(hh * HEAD_COLS, (hh + 1) * HEAD_COLS) for hh in range(hp)]
    qs = [q_ref[:, cols] for cols in heads]
    pq = pq_ref[0]
    ntri = ntri_ref[...]

    c_s[...] = jnp.zeros(c_s.shape, _F32)
    acc_s[...] = jnp.zeros(acc_s.shape, _F32)

    def step(j, masked):
        start = pl.multiple_of(j * tk, tk)
        if masked:
            pk = pk_ref[pl.ds(start, tk), :]
            strict = [pq_c > pk for pq_c in _lane_tiles(pq)]
            keep = jnp.concatenate([jnp.where(m, 1.0, 0.0) for m in strict], axis=1)
            penalty = jnp.concatenate([jnp.where(m, 0.0, NEG_INF) for m in strict], axis=1)

        def logits(hh):
            k = k_ref[pl.ds(start, tk), heads[hh]]
            z_s[hh] = lax.dot_general(k, qs[hh], _NT, preferred_element_type=_F32)

        def log_weights(hh):
            z = z_s[hh]
            softplus = jnp.maximum(z, 0.0) + jnp.log2(1.0 + jnp.exp2(-jnp.abs(z)))
            log_sig = z - softplus
            if masked:
                softplus = softplus * keep
            terms = softplus.astype(_BF16)
            later = jnp.dot(ntri, terms, preferred_element_type=_F32)
            e = log_sig + later
            if masked:
                e = e + penalty
            e_s[hh] = e
            carry = c_s[hh]
            w_s[hh] = jnp.exp2(carry)
            c_s[hh] = carry + (later[0:1, :] - terms[0:1, :].astype(_F32))

        def weighted_values(hh):
            v = v_ref[pl.ds(start, tk), heads[hh]]
            a = jnp.exp2(e_s[hh]).astype(_BF16)
            acc_s[hh] += w_s[hh] * lax.dot_general(v, a, _TN, preferred_element_type=_F32)

        _software_pipeline((logits, log_weights, weighted_values), hp)

    def rev(lo_j, hi_j, masked):
        def body(t, c):
            step(hi_j - 1 - t, masked)
            return c
        lax.fori_loop(0, hi_j - lo_j, body, 0)

    rev(n_full, n_vis, True)
    rev(0, n_full, False)
    for hh, cols in enumerate(heads):
        o_ref[:, cols] = _rms(acc_s[hh].T, g_ref[...]).astype(o_ref.dtype)


def _sb_attention(qkv, pos_row, pos_rep, n_full, n_vis, g, batch, seq):
    tq, tk, hp = SB_TQ, SB_TK, SB_HEADS_PER_STEP
    nq = seq // tq
    idx = np.arange(tk)
    ntri = jnp.asarray(-(idx[None, :] > idx[:, None]).astype(np.float32), _BF16)
    grid_spec = pltpu.PrefetchScalarGridSpec(
        num_scalar_prefetch=2,
        grid=(batch, SB_HEADS // hp, nq),
        in_specs=[
            pl.BlockSpec((tq, hp * HEAD_COLS), lambda b, h, i, *_: (b * nq + i, SB_Q_BLK // hp + h)),
            pl.BlockSpec((seq, hp * HEAD_COLS), lambda b, h, i, *_: (b, SB_K_BLK // hp + h)),
            pl.BlockSpec((seq, hp * HEAD_COLS), lambda b, h, i, *_: (b, SB_V_BLK // hp + h)),
            pl.BlockSpec((1, 1, tq), lambda b, h, i, *_: (b * nq + i, 0, 0)),
            pl.BlockSpec((seq, LANES), lambda b, h, i, *_: (b, 0)),
            pl.BlockSpec((tk, tk), lambda b, h, i, *_: (0, 0)),
            pl.BlockSpec((1, SB_DIM), lambda b, h, i, *_: (0, 0)),
        ],
        out_specs=pl.BlockSpec((tq, hp * HEAD_COLS), lambda b, h, i, *_: (b * nq + i, h)),
        scratch_shapes=[pltpu.VMEM((hp, 1, tq), _F32), pltpu.VMEM((hp, 1, tq), _F32),
                        pltpu.VMEM((hp, SB_DIM, tq), _F32),
                        pltpu.VMEM((hp, tk, tq), _F32), pltpu.VMEM((hp, tk, tq), _F32)],
    )
    return pl.pallas_call(
        functools.partial(_sb_attn_kernel, tq=tq, tk=tk, hp=hp),
        grid_spec=grid_spec,
        out_shape=jax.ShapeDtypeStruct((batch * seq, SB_HEADS * SB_DIM), _BF16),
        compiler_params=_params(("parallel", "parallel", "arbitrary")),
        name="stick_breaking_attention",
    )(n_full, n_vis, qkv, qkv, qkv, pos_row.reshape(batch * nq, 1, tq), pos_rep, ntri, g.reshape(1, SB_DIM))


def _xattn_kernel(q_ref, k_ref, v_ref, o_ref):
    for hd in range(X_HEADS):
        cols = slice(hd * X_DIM, (hd + 1) * X_DIM)
        s = lax.dot_general(q_ref[:, cols], k_ref[:, cols], _NT, preferred_element_type=_F32)
        p = jnp.exp2(s - jnp.max(s, axis=-1, keepdims=True))
        o = jnp.dot(p.astype(_BF16), v_ref[:, cols], preferred_element_type=_F32)
        o_ref[:, cols] = (o / jnp.sum(p, axis=-1, keepdims=True)).astype(o_ref.dtype)


def _cross_attention(q, kv, batch, seq, tq):
    nq = seq // tq
    return pl.pallas_call(
        _xattn_kernel,
        grid=(batch, nq),
        in_specs=[
            pl.BlockSpec((tq, D_MODEL), lambda b, i: (b * nq + i, 0)),
            pl.BlockSpec((N_MEM, D_MODEL), lambda b, i: (b, 0)),
            pl.BlockSpec((N_MEM, D_MODEL), lambda b, i: (b, 1)),
        ],
        out_specs=pl.BlockSpec((tq, D_MODEL), lambda b, i: (b * nq + i, 0)),
        out_shape=jax.ShapeDtypeStruct((batch * seq, D_MODEL), _BF16),
        compiler_params=_params(("parallel", "arbitrary")),
        name="cross_attention",
    )(q, kv, kv)


def _visible_block_counts(positions, tq, tk, strict):
    batch, seq = positions.shape
    qmin = positions.reshape(batch, seq // tq, tq).min(-1)[:, :, None]
    qmax = positions.reshape(batch, seq // tq, tq).max(-1)[:, :, None]
    kmin = positions.reshape(batch, seq // tk, tk).min(-1)[:, None, :]
    kmax = positions.reshape(batch, seq // tk, tk).max(-1)[:, None, :]
    full = (kmax < qmin) if strict else (kmax <= qmin)
    some = (kmin < qmax) if strict else (kmin <= qmax)
    n_full = jnp.sum(full, axis=-1).astype(jnp.int32).reshape(-1)
    n_vis = jnp.sum(some, axis=-1).astype(jnp.int32).reshape(-1)
    return n_full, jnp.maximum(n_vis, n_full)


def _in_proj_col_scale():
    scale = np.ones((1, IN_COLS), np.float32)
    scale[:, DIFF_Q_BLK * HEAD_COLS:(DIFF_Q_BLK + DIFF_HEADS) * HEAD_COLS] = DIFF_QK_DIM ** -0.5 * LOG2E
    scale[:, SB_Q_BLK * HEAD_COLS:(SB_Q_BLK + SB_HEADS) * HEAD_COLS] = SB_DIM ** -0.5 * LOG2E
    return jnp.asarray(scale)


def kernel(x, mem, positions, norm_mix, w_in, lam_q1, lam_k1, lam_q2, lam_k2, subln_diff, subln_sb, w_out,
           norm_x, norm_mem, wq_x, wkv_x, wo_x, norm_mlp, w_up, w_down, norm_final):
    batch, seq, d = x.shape
    depth = w_in.shape[0]
    tokens = batch * seq
    assert d == D_MODEL and w_in.shape[2] == IN_COLS
    assert seq % DIFF_TQ == 0 and seq % DIFF_TK == 0 and seq % SB_TQ == 0 and seq % SB_TK == 0

    pos_f = positions.astype(_F32)
    pos_row = pos_f.reshape(batch, seq)
    pos_rep = jnp.broadcast_to(pos_f.reshape(tokens, 1), (tokens, LANES))
    diff_counts = _visible_block_counts(positions, DIFF_TQ, DIFF_TK, strict=False)
    sb_counts = _visible_block_counts(positions, SB_TQ, SB_TK, strict=True)
    in_scale = _in_proj_col_scale()
    xq_scale = jnp.full((1, d), X_DIM ** -0.5 * LOG2E, _F32)

    w_out_bf, wo_x_bf, w_up_bf, w_down_bf = (w.astype(_BF16) for w in (w_out, wo_x, w_up, w_down))

    h = x.reshape(tokens, d)
    mem2 = mem.reshape(batch * N_MEM, d)
    u = _rmsnorm_bf16(h, norm_mix[0], tm=512)
    out = None
    for l in range(depth):
        lambda_init = 0.8 - 0.6 * math.exp(-0.3 * l)
        qkv = _proj(u, w_in, l, in_scale, tm=1024, tn=1024, name="in_proj")
        o_diff = _diff_attention(qkv, pos_row, pos_rep, *diff_counts,
                                 (lam_q1[l], lam_k1[l], lam_q2[l], lam_k2[l]), subln_diff[l], lambda_init,
                                 batch, seq)
        o_sb = _sb_attention(qkv, pos_row, pos_rep, *sb_counts, subln_sb[l], batch, seq)
        h, ux = _proj_res_norm([o_diff, o_sb], w_out_bf, l, h, norm_x[l], tm=512, name="out_proj")

        mem_n = _rmsnorm_bf16(mem2, norm_mem[l], tm=256)
        kv = _proj(mem_n, wkv_x, l, None, tm=1024, tn=1024, name="xattn_kv_proj")
        qx = _proj(ux, wq_x, l, xq_scale, tm=1024, tn=1024, name="xattn_q_proj")
        ox = _cross_attention(qx, kv, batch, seq, tq=512)
        h, um = _proj_res_norm([ox], wo_x_bf, l, h, norm_mlp[l], tm=512, name="xattn_o_proj")

        last = l == depth - 1
        g_next = norm_final if last else norm_mix[l + 1]
        h, y = _mlp(um, w_up_bf, w_down_bf, l, h, g_next, _F32 if last else _BF16, tm=512, tf=1024)
        if last:
            out = y
        else:
            u = y
    return out.reshape(batch, seq, d)
```

```python
import functools
import math

import jax
import jax.numpy as jnp
import numpy as np
from jax import lax
from jax.experimental import pallas as pl
from jax.experimental.pallas import tpu as pltpu

D_MODEL = 2048
N_MEM = 256
DIFF_V_DIM = 128
DIFF_QK_DIM = 64
DIFF_HEADS = 8
SB_DIM = 128
SB_HEADS = 8
X_HEADS = 4
X_DIM = D_MODEL // X_HEADS
D_FF = 4 * D_MODEL
EPS = 1e-6
NEG_INF = -1e30
DEAD_CARRY = -160.0
LOG2E = math.log2(math.e)

LANES = 128
HEAD_COLS = 128
DIFF_Q_BLK = 0
DIFF_K_BLK = DIFF_HEADS
DIFF_V_BLK = 2 * DIFF_HEADS
SB_Q_BLK = 3 * DIFF_HEADS
SB_K_BLK = SB_Q_BLK + SB_HEADS
SB_V_BLK = SB_K_BLK + SB_HEADS
IN_COLS = (3 * DIFF_HEADS + 3 * SB_HEADS) * HEAD_COLS

DIFF_TQ, DIFF_TK = 512, 512
SB_TQ, SB_TK = 512, 256
DIFF_HEADS_PER_STEP = 4
SB_HEADS_PER_STEP = 8
VMEM_LIMIT = 56 * 1024 * 1024

_BF16 = jnp.bfloat16
_F32 = jnp.float32
_NT = (((1,), (1,)), ((), ()))
_TN = (((0,), (0,)), ((), ()))


def _params(semantics):
    return pltpu.CompilerParams(dimension_semantics=semantics, vmem_limit_bytes=VMEM_LIMIT)


def _rms(x, g):
    return x * lax.rsqrt(jnp.mean(x * x, axis=-1, keepdims=True) + EPS) * g


def _lane_tiles(x):
    return [x[:, c * LANES:(c + 1) * LANES] for c in range(x.shape[1] // LANES)]


def _software_pipeline(stages, n_chains):
    for t in range(n_chains + len(stages) - 1):
        for s, stage in enumerate(stages):
            if 0 <= t - s < n_chains:
                stage(t - s)


def _rmsnorm_kernel(x_ref, g_ref, o_ref):
    o_ref[...] = _rms(x_ref[...], g_ref[...]).astype(o_ref.dtype)


def _rmsnorm_bf16(x, g, tm):
    m, d = x.shape
    return pl.pallas_call(
        _rmsnorm_kernel,
        grid=(m // tm,),
        in_specs=[pl.BlockSpec((tm, d), lambda i: (i, 0)), pl.BlockSpec((1, d), lambda i: (0, 0))],
        out_specs=pl.BlockSpec((tm, d), lambda i: (i, 0)),
        out_shape=jax.ShapeDtypeStruct((m, d), _BF16),
        compiler_params=_params(("parallel",)),
        name="rmsnorm_bf16",
    )(x, g.reshape(1, d))


def _proj_kernel(x_ref, w_ref, *rest):
    o_ref = rest[-1]
    acc = jnp.dot(x_ref[...], w_ref[...].astype(_BF16), preferred_element_type=_F32)
    if len(rest) == 2:
        acc = acc * rest[0][...]
    o_ref[...] = acc.astype(o_ref.dtype)


def _proj(x, w_stack, layer, col_scale, tm, tn, name):
    m, k = x.shape
    n = w_stack.shape[2]
    in_specs = [pl.BlockSpec((tm, k), lambda j, i: (i, 0)), pl.BlockSpec((None, k, tn), lambda j, i: (layer, 0, j))]
    args = [x, w_stack]
    if col_scale is not None:
        in_specs.append(pl.BlockSpec((1, tn), lambda j, i: (0, j)))
        args.append(col_scale)
    return pl.pallas_call(
        _proj_kernel,
        grid=(n // tn, m // tm),
        in_specs=in_specs,
        out_specs=pl.BlockSpec((tm, tn), lambda j, i: (i, j)),
        out_shape=jax.ShapeDtypeStruct((m, n), _BF16),
        compiler_params=_params(("parallel", "arbitrary")),
        name=name,
    )(*args)


def _proj_res_norm_kernel(*refs, n_in):
    xs = refs[:n_in]
    ws = refs[n_in:2 * n_in]
    h_ref, g_ref, h_out_ref, u_out_ref = refs[2 * n_in:]
    acc = h_ref[...]
    for x_ref, w_ref in zip(xs, ws):
        acc = acc + jnp.dot(x_ref[...], w_ref[...], preferred_element_type=_F32)
    h_out_ref[...] = acc
    u_out_ref[...] = _rms(acc, g_ref[...]).astype(u_out_ref.dtype)


def _proj_res_norm(xs, w_stack, layer, h, g, tm, name):
    m, d = h.shape
    n_in = len(xs)
    kx = xs[0].shape[1]
    in_specs = [pl.BlockSpec((tm, kx), lambda i: (i, 0)) for _ in xs]
    in_specs += [pl.BlockSpec((None, kx, d), lambda i, r=r: (layer, r, 0)) for r in range(n_in)]
    in_specs += [pl.BlockSpec((tm, d), lambda i: (i, 0)), pl.BlockSpec((1, d), lambda i: (0, 0))]
    return pl.pallas_call(
        functools.partial(_proj_res_norm_kernel, n_in=n_in),
        grid=(m // tm,),
        in_specs=in_specs,
        out_specs=[pl.BlockSpec((tm, d), lambda i: (i, 0)), pl.BlockSpec((tm, d), lambda i: (i, 0))],
        out_shape=[jax.ShapeDtypeStruct((m, d), _F32), jax.ShapeDtypeStruct((m, d), _BF16)],
        compiler_params=_params(("parallel",)),
        name=name,
    )(*xs, *([w_stack] * n_in), h, g.reshape(1, d))


def _mlp_kernel(u_ref, wu_ref, wd_ref, h_ref, g_ref, h_out_ref, y_out_ref):
    f = pl.program_id(1)

    @pl.when(f == 0)
    def _():
        h_out_ref[...] = h_ref[...]

    up = jnp.dot(u_ref[...], wu_ref[...], preferred_element_type=_F32)
    a = jnp.square(jnp.maximum(up, 0.0)).astype(_BF16)
    h_out_ref[...] += jnp.dot(a, wd_ref[...], preferred_element_type=_F32)

    @pl.when(f == pl.num_programs(1) - 1)
    def _():
        y_out_ref[...] = _rms(h_out_ref[...], g_ref[...]).astype(y_out_ref.dtype)


def _mlp(u, w_up_stack, w_down_stack, layer, h, g, y_dtype, tm, tf):
    m, d = h.shape
    dff = w_up_stack.shape[2]
    return pl.pallas_call(
        _mlp_kernel,
        grid=(m // tm, dff // tf),
        in_specs=[
            pl.BlockSpec((tm, d), lambda i, f: (i, 0)),
            pl.BlockSpec((None, d, tf), lambda i, f: (layer, 0, f)),
            pl.BlockSpec((None, tf, d), lambda i, f: (layer, f, 0)),
            pl.BlockSpec((tm, d), lambda i, f: (i, 0)),
            pl.BlockSpec((1, d), lambda i, f: (0, 0)),
        ],
        out_specs=[pl.BlockSpec((tm, d), lambda i, f: (i, 0)), pl.BlockSpec((tm, d), lambda i, f: (i, 0))],
        out_shape=[jax.ShapeDtypeStruct((m, d), _F32), jax.ShapeDtypeStruct((m, d), y_dtype)],
        compiler_params=_params(("parallel", "arbitrary")),
        name="mlp_relu2",
    )(u, w_up_stack, w_down_stack, h, g.reshape(1, d))


def _diff_attn_kernel(nfull_ref, nvis_ref, slopes_ref,
                      q_ref, k_ref, v_ref, pq_ref, pk_ref, lq1_ref, lk1_ref, lq2_ref, lk2_ref, g_ref,
                      o_ref, m_s, l_s, acc_s, s_s, *, lambda_init, tq, tk, hp):
    b, hg, i = pl.program_id(0), pl.program_id(1), pl.program_id(2)
    nq = pl.num_programs(2)
    n_full = nfull_ref[b * nq + i]
    n_vis = nvis_ref[b * nq + i]
    n_lane_tiles = tq // LANES
    heads = [slice(hh * HEAD_COLS, (hh + 1) * HEAD_COLS) for hh in range(hp)]

    lane = lax.broadcasted_iota(jnp.int32, (tq, HEAD_COLS), 1)
    zero = jnp.zeros((tq, HEAD_COLS), _BF16)
    q_maps = []
    for cols in heads:
        q = q_ref[:, cols]
        q_maps += [jnp.where(lane < DIFF_QK_DIM, q, zero), jnp.where(lane >= DIFF_QK_DIM, q, zero)]
    slope2 = [slopes_ref[hg * hp + hh] * LOG2E for hh in range(hp)]
    pq = pq_ref[0]
    pq0 = pq[:, 0:1]

    m_s[...] = jnp.full(m_s.shape, NEG_INF, _F32)
    l_s[...] = jnp.zeros(l_s.shape, _F32)
    acc_s[...] = jnp.zeros(acc_s.shape, _F32)

    def step(j, masked):
        start = pl.multiple_of(j * tk, tk)
        pk = pk_ref[pl.ds(start, tk), :]
        rel = pk - pq0
        bias = [jnp.concatenate([slope2[hh] * rel] * n_lane_tiles, axis=1) for hh in range(hp)]
        if masked:
            penalty = jnp.concatenate([jnp.where(pq_c >= pk, 0.0, NEG_INF) for pq_c in _lane_tiles(pq)], axis=1)
            bias = [b_h + penalty for b_h in bias]

        def scores(ci):
            k = k_ref[pl.ds(start, tk), heads[ci // 2]]
            s_s[ci] = lax.dot_general(k, q_maps[ci], _NT, preferred_element_type=_F32) + bias[ci // 2]

        def softmax_pv(ci):
            v = v_ref[pl.ds(start, tk), heads[ci // 2]]
            s = s_s[ci]
            m_prev = m_s[ci]
            m_new = jnp.maximum(m_prev, jnp.max(s, axis=0, keepdims=True))
            alpha = jnp.exp2(m_prev - m_new)
            p = jnp.exp2(s - m_new)
            l_s[ci] = alpha * l_s[ci] + jnp.sum(p, axis=0, keepdims=True)
            pv = lax.dot_general(v, p.astype(_BF16), _TN, preferred_element_type=_F32)
            acc_s[ci] = alpha * acc_s[ci] + pv
            m_s[ci] = m_new

        _software_pipeline((scores, softmax_pv), 2 * hp)

    def full_body(j, c):
        step(j, False)
        return c

    def masked_body(j, c):
        step(j, True)
        return c

    lax.fori_loop(0, n_full, full_body, 0)
    lax.fori_loop(n_full, n_vis, masked_body, 0)

    lam = (jnp.exp(jnp.sum(lq1_ref[...] * lk1_ref[...], axis=-1, keepdims=True))
           - jnp.exp(jnp.sum(lq2_ref[...] * lk2_ref[...], axis=-1, keepdims=True)) + lambda_init)
    for hh, cols in enumerate(heads):
        o_t = acc_s[2 * hh] / l_s[2 * hh] - lam * (acc_s[2 * hh + 1] / l_s[2 * hh + 1])
        o_ref[:, cols] = (_rms(o_t.T, g_ref[...]) * (1.0 - lambda_init)).astype(o_ref.dtype)


def _diff_attention(qkv, pos_row, pos_rep, n_full, n_vis, lam_params, g, lambda_init, batch, seq):
    tq, tk, hp = DIFF_TQ, DIFF_TK, DIFF_HEADS_PER_STEP
    nq = seq // tq
    slopes = jnp.asarray(np.array([2.0 ** (-8.0 * (i + 1) / DIFF_HEADS) for i in range(DIFF_HEADS)], np.float32))
    small = lambda b, h, i, *_: (0, 0)
    grid_spec = pltpu.PrefetchScalarGridSpec(
        num_scalar_prefetch=3,
        grid=(batch, DIFF_HEADS // hp, nq),
        in_specs=[
            pl.BlockSpec((tq, hp * HEAD_COLS), lambda b, h, i, *_: (b * nq + i, DIFF_Q_BLK // hp + h)),
            pl.BlockSpec((seq, hp * HEAD_COLS), lambda b, h, i, *_: (b, DIFF_K_BLK // hp + h)),
            pl.BlockSpec((seq, hp * HEAD_COLS), lambda b, h, i, *_: (b, DIFF_V_BLK // hp + h)),
            pl.BlockSpec((1, 1, tq), lambda b, h, i, *_: (b * nq + i, 0, 0)),
            pl.BlockSpec((seq, LANES), lambda b, h, i, *_: (b, 0)),
            pl.BlockSpec((1, DIFF_QK_DIM), small), pl.BlockSpec((1, DIFF_QK_DIM), small),
            pl.BlockSpec((1, DIFF_QK_DIM), small), pl.BlockSpec((1, DIFF_QK_DIM), small),
            pl.BlockSpec((1, DIFF_V_DIM), small),
        ],
        out_specs=pl.BlockSpec((tq, hp * HEAD_COLS), lambda b, h, i, *_: (b * nq + i, h)),
        scratch_shapes=[pltpu.VMEM((2 * hp, 1, tq), _F32), pltpu.VMEM((2 * hp, 1, tq), _F32),
                        pltpu.VMEM((2 * hp, DIFF_V_DIM, tq), _F32), pltpu.VMEM((2 * hp, tk, tq), _F32)],
    )
    return pl.pallas_call(
        functools.partial(_diff_attn_kernel, lambda_init=lambda_init, tq=tq, tk=tk, hp=hp),
        grid_spec=grid_spec,
        out_shape=jax.ShapeDtypeStruct((batch * seq, DIFF_HEADS * DIFF_V_DIM), _BF16),
        compiler_params=_params(("parallel", "parallel", "arbitrary")),
        name="diff_attention",
    )(n_full, n_vis, slopes, qkv, qkv, qkv, pos_row.reshape(batch * nq, 1, tq), pos_rep,
      *[p.reshape(1, DIFF_QK_DIM) for p in lam_params], g.reshape(1, DIFF_V_DIM))


def _sb_attn_kernel(nfull_ref, nvis_ref, q_ref, k_ref, v_ref, pq_ref, pk_ref, ntri_ref, g_ref,
                    o_ref, c_s, w_s, acc_s, z_s, e_s, *, tq, tk, hp):
    b, i = pl.program_id(0), pl.program_id(2)
    nq = pl.num_programs(2)
    n_full = nfull_ref[b * nq + i]
    n_vis = nvis_ref[b * nq + i]
    heads = [slice(hh * HEAD_COLS, (hh + 1) * HEAD_COLS) for hh in range(hp)]
    qs = [q_ref[:, cols] for cols in heads]
    pq = pq_ref[0]
    ntri = ntri_ref[...]

    c_s[...] = jnp.zeros(c_s.shape, _F32)
    acc_s[...] = jnp.zeros(acc_s.shape, _F32)

    def step(j, masked):
        start = pl.multiple_of(j * tk, tk)
        if masked:
            pk = pk_ref[pl.ds(start, tk), :]
            strict = [pq_c > pk for pq_c in _lane_tiles(pq)]
            keep = jnp.concatenate([jnp.where(m, 1.0, 0.0) for m in strict], axis=1)
            penalty = jnp.concatenate([jnp.where(m, 0.0, NEG_INF) for m in strict], axis=1)

        def logits(hh):
            k = k_ref[pl.ds(start, tk), heads[hh]]
            z_s[hh] = lax.dot_general(k, qs[hh], _NT, preferred_element_type=_F32)

        def log_weights(hh):
            z = z_s[hh]
            softplus = jnp.maximum(z, 0.0) + jnp.log2(1.0 + jnp.exp2(-jnp.abs(z)))
            log_sig = z - softplus
            if masked:
                softplus = softplus * keep
            terms = softplus.astype(_BF16)
            later = jnp.dot(ntri, terms, preferred_element_type=_F32)
            e = log_sig + later
            if masked:
                e = e + penalty
            e_s[hh] = e
            carry = c_s[hh]
            w_s[hh] = jnp.exp2(carry)
            c_s[hh] = carry + (later[0:1, :] - terms[0:1, :].astype(_F32))

        def weighted_values(hh):
            v = v_ref[pl.ds(start, tk), heads[hh]]
            a = jnp.exp2(e_s[hh]).astype(_BF16)
            acc_s[hh] += w_s[hh] * lax.dot_general(v, a, _TN, preferred_element_type=_F32)

        _software_pipeline((logits, log_weights, weighted_values), hp)

    def masked_body(t, c):
        step(n_vis - 1 - t, True)
        return c

    lax.fori_loop(0, n_vis - n_full, masked_body, 0)

    def any_weight_left():
        return jnp.max(c_s[...]) > DEAD_CARRY

    def full_cond(state):
        t, go = state
        return jnp.logical_and(t < n_full, go)

    def full_body(state):
        t, _ = state
        step(n_full - 1 - t, False)
        return t + 1, any_weight_left()

    lax.while_loop(full_cond, full_body, (jnp.int32(0), any_weight_left()))
    for hh, cols in enumerate(heads):
        o_ref[:, cols] = _rms(acc_s[hh].T, g_ref[...]).astype(o_ref.dtype)


def _sb_attention(qkv, pos_row, pos_rep, n_full, n_vis, g, batch, seq):
    tq, tk, hp = SB_TQ, SB_TK, SB_HEADS_PER_STEP
    nq = seq // tq
    idx = np.arange(tk)
    ntri = jnp.asarray(-(idx[None, :] > idx[:, None]).astype(np.float32), _BF16)
    grid_spec = pltpu.PrefetchScalarGridSpec(
        num_scalar_prefetch=2,
        grid=(batch, SB_HEADS // hp, nq),
        in_specs=[
            pl.BlockSpec((tq, hp * HEAD_COLS), lambda b, h, i, *_: (b * nq + i, SB_Q_BLK // hp + h)),
            pl.BlockSpec((seq, hp * HEAD_COLS), lambda b, h, i, *_: (b, SB_K_BLK // hp + h)),
            pl.BlockSpec((seq, hp * HEAD_COLS), lambda b, h, i, *_: (b, SB_V_BLK // hp + h)),
            pl.BlockSpec((1, 1, tq), lambda b, h, i, *_: (b * nq + i, 0, 0)),
            pl.BlockSpec((seq, LANES), lambda b, h, i, *_: (b, 0)),
            pl.BlockSpec((tk, tk), lambda b, h, i, *_: (0, 0)),
            pl.BlockSpec((1, SB_DIM), lambda b, h, i, *_: (0, 0)),
        ],
        out_specs=pl.BlockSpec((tq, hp * HEAD_COLS), lambda b, h, i, *_: (b * nq + i, h)),
        scratch_shapes=[pltpu.VMEM((hp, 1, tq), _F32), pltpu.VMEM((hp, 1, tq), _F32),
                        pltpu.VMEM((hp, SB_DIM, tq), _F32),
                        pltpu.VMEM((hp, tk, tq), _F32), pltpu.VMEM((hp, tk, tq), _F32)],
    )
    return pl.pallas_call(
        functools.partial(_sb_attn_kernel, tq=tq, tk=tk, hp=hp),
        grid_spec=grid_spec,
        out_shape=jax.ShapeDtypeStruct((batch * seq, SB_HEADS * SB_DIM), _BF16),
        compiler_params=_params(("parallel", "parallel", "arbitrary")),
        name="stick_breaking_attention",
    )(n_full, n_vis, qkv, qkv, qkv, pos_row.reshape(batch * nq, 1, tq), pos_rep, ntri, g.reshape(1, SB_DIM))


def _xattn_kernel(q_ref, k_ref, v_ref, o_ref):
    for hd in range(X_HEADS):
        cols = slice(hd * X_DIM, (hd + 1) * X_DIM)
        s = lax.dot_general(q_ref[:, cols], k_ref[:, cols], _NT, preferred_element_type=_F32)
        p = jnp.exp2(s - jnp.max(s, axis=-1, keepdims=True))
        o = jnp.dot(p.astype(_BF16), v_ref[:, cols], preferred_element_type=_F32)
        o_ref[:, cols] = (o / jnp.sum(p, axis=-1, keepdims=True)).astype(o_ref.dtype)


def _cross_attention(q, kv, batch, seq, tq):
    nq = seq // tq
    return pl.pallas_call(
        _xattn_kernel,
        grid=(batch, nq),
        in_specs=[
            pl.BlockSpec((tq, D_MODEL), lambda b, i: (b * nq + i, 0)),
            pl.BlockSpec((N_MEM, D_MODEL), lambda b, i: (b, 0)),
            pl.BlockSpec((N_MEM, D_MODEL), lambda b, i: (b, 1)),
        ],
        out_specs=pl.BlockSpec((tq, D_MODEL), lambda b, i: (b * nq + i, 0)),
        out_shape=jax.ShapeDtypeStruct((batch * seq, D_MODEL), _BF16),
        compiler_params=_params(("parallel", "arbitrary")),
        name="cross_attention",
    )(q, kv, kv)


def _visible_block_counts(positions, tq, tk, strict):
    batch, seq = positions.shape
    qmin = positions.reshape(batch, seq // tq, tq).min(-1)[:, :, None]
    qmax = positions.reshape(batch, seq // tq, tq).max(-1)[:, :, None]
    kmin = positions.reshape(batch, seq // tk, tk).min(-1)[:, None, :]
    kmax = positions.reshape(batch, seq // tk, tk).max(-1)[:, None, :]
    full = (kmax < qmin) if strict else (kmax <= qmin)
    some = (kmin < qmax) if strict else (kmin <= qmax)
    n_full = jnp.sum(full, axis=-1).astype(jnp.int32).reshape(-1)
    n_vis = jnp.sum(some, axis=-1).astype(jnp.int32).reshape(-1)
    return n_full, jnp.maximum(n_vis, n_full)


def _in_proj_col_scale():
    scale = np.ones((1, IN_COLS), np.float32)
    scale[:, DIFF_Q_BLK * HEAD_COLS:(DIFF_Q_BLK + DIFF_HEADS) * HEAD_COLS] = DIFF_QK_DIM ** -0.5 * LOG2E
    scale[:, SB_Q_BLK * HEAD_COLS:(SB_Q_BLK + SB_HEADS) * HEAD_COLS] = SB_DIM ** -0.5 * LOG2E
    return jnp.asarray(scale)


def kernel(x, mem, positions, norm_mix, w_in, lam_q1, lam_k1, lam_q2, lam_k2, subln_diff, subln_sb, w_out,
           norm_x, norm_mem, wq_x, wkv_x, wo_x, norm_mlp, w_up, w_down, norm_final):
    batch, seq, d = x.shape
    depth = w_in.shape[0]
    tokens = batch * seq
    assert d == D_MODEL and w_in.shape[2] == IN_COLS
    assert seq % DIFF_TQ == 0 and seq % DIFF_TK == 0 and seq % SB_TQ == 0 and seq % SB_TK == 0

    pos_f = positions.astype(_F32)
    pos_row = pos_f.reshape(batch, seq)
    pos_rep = jnp.broadcast_to(pos_f.reshape(tokens, 1), (tokens, LANES))
    diff_counts = _visible_block_counts(positions, DIFF_TQ, DIFF_TK, strict=False)
    sb_counts = _visible_block_counts(positions, SB_TQ, SB_TK, strict=True)
    in_scale = _in_proj_col_scale()
    xq_scale = jnp.full((1, d), X_DIM ** -0.5 * LOG2E, _F32)

    w_out_bf, wo_x_bf, w_up_bf, w_down_bf = (w.astype(_BF16) for w in (w_out, wo_x, w_up, w_down))

    h = x.reshape(tokens, d)
    mem2 = mem.reshape(batch * N_MEM, d)
    u = _rmsnorm_bf16(h, norm_mix[0], tm=512)
    out = None
    for l in range(depth):
        lambda_init = 0.8 - 0.6 * math.exp(-0.3 * l)
        qkv = _proj(u, w_in, l, in_scale, tm=1024, tn=1024, name="in_proj")
        o_diff = _diff_attention(qkv, pos_row, pos_rep, *diff_counts,
                                 (lam_q1[l], lam_k1[l], lam_q2[l], lam_k2[l]), subln_diff[l], lambda_init,
                                 batch, seq)
        o_sb = _sb_attention(qkv, pos_row, pos_rep, *sb_counts, subln_sb[l], batch, seq)
        h, ux = _proj_res_norm([o_diff, o_sb], w_out_bf, l, h, norm_x[l], tm=512, name="out_proj")

        mem_n = _rmsnorm_bf16(mem2, norm_mem[l], tm=256)
        kv = _proj(mem_n, wkv_x, l, None, tm=1024, tn=1024, name="xattn_kv_proj")
        qx = _proj(ux, wq_x, l, xq_scale, tm=1024, tn=1024, name="xattn_q_proj")
        ox = _cross_attention(qx, kv, batch, seq, tq=512)
        h, um = _proj_res_norm([ox], wo_x_bf, l, h, norm_mlp[l], tm=512, name="xattn_o_proj")

        last = l == depth - 1
        g_next = norm_final if last else norm_mix[l + 1]
        h, y = _mlp(um, w_up_bf, w_down_bf, l, h, g_next, _F32 if last else _BF16, tm=512, tf=1024)
        if last:
            out = y
        else:
            u = y
    return out.reshape(batch, seq, d)
```

```python
import functools
import math

import jax
import jax.numpy as jnp
import numpy as np
from jax import lax
from jax.experimental import pallas as pl
from jax.experimental.pallas import tpu as pltpu

D_MODEL = 2048
N_MEM = 256
DIFF_V_DIM = 128
DIFF_QK_DIM = 64
DIFF_HEADS = 8
SB_DIM = 128
SB_HEADS = 8
X_HEADS = 4
X_DIM = D_MODEL // X_HEADS
D_FF = 4 * D_MODEL
EPS = 1e-6
NEG_INF = -1e30
DEAD_CARRY = -160.0
LOG2E = math.log2(math.e)

LANES = 128
HEAD_COLS = 128
DIFF_Q_BLK = 0
DIFF_K_BLK = DIFF_HEADS
DIFF_V_BLK = 2 * DIFF_HEADS
SB_Q_BLK = 3 * DIFF_HEADS
SB_K_BLK = SB_Q_BLK + SB_HEADS
SB_V_BLK = SB_K_BLK + SB_HEADS
IN_COLS = (3 * DIFF_HEADS + 3 * SB_HEADS) * HEAD_COLS

DIFF_TQ, DIFF_TK = 512, 512
SB_TQ, SB_TK = 256, 256
DIFF_HEADS_PER_STEP = 4
SB_HEADS_PER_STEP = 8
VMEM_LIMIT = 56 * 1024 * 1024

_BF16 = jnp.bfloat16
_F32 = jnp.float32
_NT = (((1,), (1,)), ((), ()))
_TN = (((0,), (0,)), ((), ()))


def _params(semantics):
    return pltpu.CompilerParams(dimension_semantics=semantics, vmem_limit_bytes=VMEM_LIMIT)


def _rms(x, g):
    return x * lax.rsqrt(jnp.mean(x * x, axis=-1, keepdims=True) + EPS) * g


def _lane_tiles(x):
    return [x[:, c * LANES:(c + 1) * LANES] for c in range(x.shape[1] // LANES)]


def _software_pipeline(stages, n_chains):
    for t in range(n_chains + len(stages) - 1):
        for s, stage in enumerate(stages):
            if 0 <= t - s < n_chains:
                stage(t - s)


def _rmsnorm_kernel(x_ref, g_ref, o_ref):
    o_ref[...] = _rms(x_ref[...], g_ref[...]).astype(o_ref.dtype)


def _rmsnorm_bf16(x, g, tm):
    m, d = x.shape
    return pl.pallas_call(
        _rmsnorm_kernel,
        grid=(m // tm,),
        in_specs=[pl.BlockSpec((tm, d), lambda i: (i, 0)), pl.BlockSpec((1, d), lambda i: (0, 0))],
        out_specs=pl.BlockSpec((tm, d), lambda i: (i, 0)),
        out_shape=jax.ShapeDtypeStruct((m, d), _BF16),
        compiler_params=_params(("parallel",)),
        name="rmsnorm_bf16",
    )(x, g.reshape(1, d))


def _proj_kernel(x_ref, w_ref, *rest):
    o_ref = rest[-1]
    acc = jnp.dot(x_ref[...], w_ref[...].astype(_BF16), preferred_element_type=_F32)
    if len(rest) == 2:
        acc = acc * rest[0][...]
    o_ref[...] = acc.astype(o_ref.dtype)


def _proj(x, w_stack, layer, col_scale, tm, tn, name):
    m, k = x.shape
    n = w_stack.shape[2]
    in_specs = [pl.BlockSpec((tm, k), lambda j, i: (i, 0)), pl.BlockSpec((None, k, tn), lambda j, i: (layer, 0, j))]
    args = [x, w_stack]
    if col_scale is not None:
        in_specs.append(pl.BlockSpec((1, tn), lambda j, i: (0, j)))
        args.append(col_scale)
    return pl.pallas_call(
        _proj_kernel,
        grid=(n // tn, m // tm),
        in_specs=in_specs,
        out_specs=pl.BlockSpec((tm, tn), lambda j, i: (i, j)),
        out_shape=jax.ShapeDtypeStruct((m, n), _BF16),
        compiler_params=_params(("parallel", "arbitrary")),
        name=name,
    )(*args)


def _proj_res_norm_kernel(*refs, n_in):
    xs = refs[:n_in]
    ws = refs[n_in:2 * n_in]
    h_ref, g_ref, h_out_ref, u_out_ref = refs[2 * n_in:]
    acc = h_ref[...]
    for x_ref, w_ref in zip(xs, ws):
        acc = acc + jnp.dot(x_ref[...], w_ref[...], preferred_element_type=_F32)
    h_out_ref[...] = acc
    u_out_ref[...] = _rms(acc, g_ref[...]).astype(u_out_ref.dtype)


def _proj_res_norm(xs, w_stack, layer, h, g, tm, name):
    m, d = h.shape
    n_in = len(xs)
    kx = xs[0].shape[1]
    in_specs = [pl.BlockSpec((tm, kx), lambda i: (i, 0)) for _ in xs]
    in_specs += [pl.BlockSpec((None, kx, d), lambda i, r=r: (layer, r, 0)) for r in range(n_in)]
    in_specs += [pl.BlockSpec((tm, d), lambda i: (i, 0)), pl.BlockSpec((1, d), lambda i: (0, 0))]
    return pl.pallas_call(
        functools.partial(_proj_res_norm_kernel, n_in=n_in),
        grid=(m // tm,),
        in_specs=in_specs,
        out_specs=[pl.BlockSpec((tm, d), lambda i: (i, 0)), pl.BlockSpec((tm, d), lambda i: (i, 0))],
        out_shape=[jax.ShapeDtypeStruct((m, d), _F32), jax.ShapeDtypeStruct((m, d), _BF16)],
        compiler_params=_params(("parallel",)),
        name=name,
    )(*xs, *([w_stack] * n_in), h, g.reshape(1, d))


def _mlp_kernel(u_ref, wu_ref, wd_ref, h_ref, g_ref, h_out_ref, y_out_ref):
    f = pl.program_id(1)

    @pl.when(f == 0)
    def _():
        h_out_ref[...] = h_ref[...]

    up = jnp.dot(u_ref[...], wu_ref[...], preferred_element_type=_F32)
    a = jnp.square(jnp.maximum(up, 0.0)).astype(_BF16)
    h_out_ref[...] += jnp.dot(a, wd_ref[...], preferred_element_type=_F32)

    @pl.when(f == pl.num_programs(1) - 1)
    def _():
        y_out_ref[...] = _rms(h_out_ref[...], g_ref[...]).astype(y_out_ref.dtype)


def _mlp(u, w_up_stack, w_down_stack, layer, h, g, y_dtype, tm, tf):
    m, d = h.shape
    dff = w_up_stack.shape[2]
    return pl.pallas_call(
        _mlp_kernel,
        grid=(m // tm, dff // tf),
        in_specs=[
            pl.BlockSpec((tm, d), lambda i, f: (i, 0)),
            pl.BlockSpec((None, d, tf), lambda i, f: (layer, 0, f)),
            pl.BlockSpec((None, tf, d), lambda i, f: (layer, f, 0)),
            pl.BlockSpec((tm, d), lambda i, f: (i, 0), pipeline_mode=pl.Buffered(1)),
            pl.BlockSpec((1, d), lambda i, f: (0, 0)),
        ],
        out_specs=[pl.BlockSpec((tm, d), lambda i, f: (i, 0)),
                   pl.BlockSpec((tm, d), lambda i, f: (i, 0), pipeline_mode=pl.Buffered(1))],
        out_shape=[jax.ShapeDtypeStruct((m, d), _F32), jax.ShapeDtypeStruct((m, d), y_dtype)],
        compiler_params=_params(("parallel", "arbitrary")),
        name="mlp_relu2",
    )(u, w_up_stack, w_down_stack, h, g.reshape(1, d))


def _diff_attn_kernel(nfull_ref, nvis_ref, slopes_ref,
                      q_ref, k_ref, v_ref, pq_ref, pk_ref, lq1_ref, lk1_ref, lq2_ref, lk2_ref, g_ref,
                      o_ref, m_s, l_s, acc_s, s_s, *, lambda_init, tq, tk, hp):
    b, hg, i = pl.program_id(0), pl.program_id(1), pl.program_id(2)
    nq = pl.num_programs(2)
    n_full = nfull_ref[b * nq + i]
    n_vis = nvis_ref[b * nq + i]
    n_lane_tiles = tq // LANES
    heads = [slice(hh * HEAD_COLS, (hh + 1) * HEAD_COLS) for hh in range(hp)]

    lane = lax.broadcasted_iota(jnp.int32, (tq, HEAD_COLS), 1)
    zero = jnp.zeros((tq, HEAD_COLS), _BF16)
    q_maps = []
    for cols in heads:
        q = q_ref[:, cols]
        q_maps += [jnp.where(lane < DIFF_QK_DIM, q, zero), jnp.where(lane >= DIFF_QK_DIM, q, zero)]
    slope2 = [slopes_ref[hg * hp + hh] * LOG2E for hh in range(hp)]
    pq = pq_ref[0]
    pq0 = pq[:, 0:1]

    m_s[...] = jnp.full(m_s.shape, NEG_INF, _F32)
    l_s[...] = jnp.zeros(l_s.shape, _F32)
    acc_s[...] = jnp.zeros(acc_s.shape, _F32)

    def step(j, masked):
        start = pl.multiple_of(j * tk, tk)
        pk = pk_ref[pl.ds(start, tk), :]
        rel = pk - pq0
        bias = [jnp.concatenate([slope2[hh] * rel] * n_lane_tiles, axis=1) for hh in range(hp)]
        if masked:
            penalty = jnp.concatenate([jnp.where(pq_c >= pk, 0.0, NEG_INF) for pq_c in _lane_tiles(pq)], axis=1)
            bias = [b_h + penalty for b_h in bias]

        def scores(ci):
            k = k_ref[pl.ds(start, tk), heads[ci // 2]]
            s_s[ci] = lax.dot_general(k, q_maps[ci], _NT, preferred_element_type=_F32) + bias[ci // 2]

        def softmax_pv(ci):
            v = v_ref[pl.ds(start, tk), heads[ci // 2]]
            s = s_s[ci]
            m_prev = m_s[ci]
            m_new = jnp.maximum(m_prev, jnp.max(s, axis=0, keepdims=True))
            alpha = jnp.exp2(m_prev - m_new)
            p = jnp.exp2(s - m_new)
            l_s[ci] = alpha * l_s[ci] + jnp.sum(p, axis=0, keepdims=True)
            pv = lax.dot_general(v, p.astype(_BF16), _TN, preferred_element_type=_F32)
            acc_s[ci] = alpha * acc_s[ci] + pv
            m_s[ci] = m_new

        _software_pipeline((scores, softmax_pv), 2 * hp)

    def full_body(j, c):
        step(j, False)
        return c

    def masked_body(j, c):
        step(j, True)
        return c

    lax.fori_loop(0, n_full, full_body, 0)
    lax.fori_loop(n_full, n_vis, masked_body, 0)

    lam = (jnp.exp(jnp.sum(lq1_ref[...] * lk1_ref[...], axis=-1, keepdims=True))
           - jnp.exp(jnp.sum(lq2_ref[...] * lk2_ref[...], axis=-1, keepdims=True)) + lambda_init)
    for hh, cols in enumerate(heads):
        o_t = acc_s[2 * hh] / l_s[2 * hh] - lam * (acc_s[2 * hh + 1] / l_s[2 * hh + 1])
        o_ref[:, cols] = (_rms(o_t.T, g_ref[...]) * (1.0 - lambda_init)).astype(o_ref.dtype)


def _diff_attention(qkv, pos_row, pos_rep, n_full, n_vis, lam_params, g, lambda_init, batch, seq):
    tq, tk, hp = DIFF_TQ, DIFF_TK, DIFF_HEADS_PER_STEP
    nq = seq // tq
    slopes = jnp.asarray(np.array([2.0 ** (-8.0 * (i + 1) / DIFF_HEADS) for i in range(DIFF_HEADS)], np.float32))
    small = lambda b, h, i, *_: (0, 0)
    grid_spec = pltpu.PrefetchScalarGridSpec(
        num_scalar_prefetch=3,
        grid=(batch, DIFF_HEADS // hp, nq),
        in_specs=[
            pl.BlockSpec((tq, hp * HEAD_COLS), lambda b, h, i, *_: (b * nq + i, DIFF_Q_BLK // hp + h)),
            pl.BlockSpec((seq, hp * HEAD_COLS), lambda b, h, i, *_: (b, DIFF_K_BLK // hp + h)),
            pl.BlockSpec((seq, hp * HEAD_COLS), lambda b, h, i, *_: (b, DIFF_V_BLK // hp + h)),
            pl.BlockSpec((1, 1, tq), lambda b, h, i, *_: (b * nq + i, 0, 0)),
            pl.BlockSpec((seq, LANES), lambda b, h, i, *_: (b, 0)),
            pl.BlockSpec((1, DIFF_QK_DIM), small), pl.BlockSpec((1, DIFF_QK_DIM), small),
            pl.BlockSpec((1, DIFF_QK_DIM), small), pl.BlockSpec((1, DIFF_QK_DIM), small),
            pl.BlockSpec((1, DIFF_V_DIM), small),
        ],
        out_specs=pl.BlockSpec((tq, hp * HEAD_COLS), lambda b, h, i, *_: (b * nq + i, h)),
        scratch_shapes=[pltpu.VMEM((2 * hp, 1, tq), _F32), pltpu.VMEM((2 * hp, 1, tq), _F32),
                        pltpu.VMEM((2 * hp, DIFF_V_DIM, tq), _F32), pltpu.VMEM((2 * hp, tk, tq), _F32)],
    )
    return pl.pallas_call(
        functools.partial(_diff_attn_kernel, lambda_init=lambda_init, tq=tq, tk=tk, hp=hp),
        grid_spec=grid_spec,
        out_shape=jax.ShapeDtypeStruct((batch * seq, DIFF_HEADS * DIFF_V_DIM), _BF16),
        compiler_params=_params(("parallel", "parallel", "arbitrary")),
        name="diff_attention",
    )(n_full, n_vis, slopes, qkv, qkv, qkv, pos_row.reshape(batch * nq, 1, tq), pos_rep,
      *[p.reshape(1, DIFF_QK_DIM) for p in lam_params], g.reshape(1, DIFF_V_DIM))


def _sb_attn_kernel(nfull_ref, nvis_ref, q_ref, k_ref, v_ref, pq_ref, pk_ref, ntri_ref, g_ref,
                    o_ref, c_s, w_s, acc_s, z_s, e_s, *, tq, tk, hp):
    b, i = pl.program_id(0), pl.program_id(2)
    nq = pl.num_programs(2)
    n_full = nfull_ref[b * nq + i]
    n_vis = nvis_ref[b * nq + i]
    heads = [slice(hh * HEAD_COLS, (hh + 1) * HEAD_COLS) for hh in range(hp)]
    qs = [q_ref[:, cols] for cols in heads]
    pq = pq_ref[0]
    ntri = ntri_ref[...]

    c_s[...] = jnp.zeros(c_s.shape, _F32)
    acc_s[...] = jnp.zeros(acc_s.shape, _F32)

    def step(j, masked):
        start = pl.multiple_of(j * tk, tk)
        if masked:
            pk = pk_ref[pl.ds(start, tk), :]
            strict = [pq_c > pk for pq_c in _lane_tiles(pq)]
            keep = jnp.concatenate([jnp.where(m, 1.0, 0.0) for m in strict], axis=1)
            penalty = jnp.concatenate([jnp.where(m, 0.0, NEG_INF) for m in strict], axis=1)

        def logits(hh):
            k = k_ref[pl.ds(start, tk), heads[hh]]
            z_s[hh] = lax.dot_general(k, qs[hh], _NT, preferred_element_type=_F32)

        def log_weights(hh):
            z = z_s[hh]
            softplus = jnp.maximum(z, 0.0) + jnp.log2(1.0 + jnp.exp2(-jnp.abs(z)))
            log_sig = z - softplus
            if masked:
                softplus = softplus * keep
            terms = softplus.astype(_BF16)
            later = jnp.dot(ntri, terms, preferred_element_type=_F32)
            e = log_sig + later
            if masked:
                e = e + penalty
            e_s[hh] = e
            carry = c_s[hh]
            w_s[hh] = jnp.exp2(carry)
            c_s[hh] = carry + (later[0:1, :] - terms[0:1, :].astype(_F32))

        def weighted_values(hh):
            v = v_ref[pl.ds(start, tk), heads[hh]]
            a = jnp.exp2(e_s[hh]).astype(_BF16)
            acc_s[hh] += w_s[hh] * lax.dot_general(v, a, _TN, preferred_element_type=_F32)

        _software_pipeline((logits, log_weights, weighted_values), hp)

    def masked_body(t, c):
        step(n_vis - 1 - t, True)
        return c

    lax.fori_loop(0, n_vis - n_full, masked_body, 0)

    def any_weight_left():
        return jnp.max(c_s[...]) > DEAD_CARRY

    def full_cond(state):
        t, go = state
        return jnp.logical_and(t < n_full, go)

    def full_body(state):
        t, _ = state
        step(n_full - 1 - t, False)
        return t + 1, any_weight_left()

    lax.while_loop(full_cond, full_body, (jnp.int32(0), any_weight_left()))
    for hh, cols in enumerate(heads):
        o_ref[:, cols] = _rms(acc_s[hh].T, g_ref[...]).astype(o_ref.dtype)


def _sb_attention(qkv, pos_row, pos_rep, n_full, n_vis, g, batch, seq):
    tq, tk, hp = SB_TQ, SB_TK, SB_HEADS_PER_STEP
    nq = seq // tq
    idx = np.arange(tk)
    ntri = jnp.asarray(-(idx[None, :] > idx[:, None]).astype(np.float32), _BF16)
    grid_spec = pltpu.PrefetchScalarGridSpec(
        num_scalar_prefetch=2,
        grid=(batch, SB_HEADS // hp, nq),
        in_specs=[
            pl.BlockSpec((tq, hp * HEAD_COLS), lambda b, h, i, *_: (b * nq + i, SB_Q_BLK // hp + h)),
            pl.BlockSpec((seq, hp * HEAD_COLS), lambda b, h, i, *_: (b, SB_K_BLK // hp + h)),
            pl.BlockSpec((seq, hp * HEAD_COLS), lambda b, h, i, *_: (b, SB_V_BLK // hp + h)),
            pl.BlockSpec((1, 1, tq), lambda b, h, i, *_: (b * nq + i, 0, 0)),
            pl.BlockSpec((seq, LANES), lambda b, h, i, *_: (b, 0)),
            pl.BlockSpec((tk, tk), lambda b, h, i, *_: (0, 0)),
            pl.BlockSpec((1, SB_DIM), lambda b, h, i, *_: (0, 0)),
        ],
        out_specs=pl.BlockSpec((tq, hp * HEAD_COLS), lambda b, h, i, *_: (b * nq + i, h)),
        scratch_shapes=[pltpu.VMEM((hp, 1, tq), _F32), pltpu.VMEM((hp, 1, tq), _F32),
                        pltpu.VMEM((hp, SB_DIM, tq), _F32),
                        pltpu.VMEM((hp, tk, tq), _F32), pltpu.VMEM((hp, tk, tq), _F32)],
    )
    return pl.pallas_call(
        functools.partial(_sb_attn_kernel, tq=tq, tk=tk, hp=hp),
        grid_spec=grid_spec,
        out_shape=jax.ShapeDtypeStruct((batch * seq, SB_HEADS * SB_DIM), _BF16),
        compiler_params=_params(("parallel", "parallel", "arbitrary")),
        name="stick_breaking_attention",
    )(n_full, n_vis, qkv, qkv, qkv, pos_row.reshape(batch * nq, 1, tq), pos_rep, ntri, g.reshape(1, SB_DIM))


def _xattn_kernel(q_ref, k_ref, v_ref, o_ref):
    for hd in range(X_HEADS):
        cols = slice(hd * X_DIM, (hd + 1) * X_DIM)
        s = lax.dot_general(q_ref[:, cols], k_ref[:, cols], _NT, preferred_element_type=_F32)
        p = jnp.exp2(s - jnp.max(s, axis=-1, keepdims=True))
        o = jnp.dot(p.astype(_BF16), v_ref[:, cols], preferred_element_type=_F32)
        o_ref[:, cols] = (o / jnp.sum(p, axis=-1, keepdims=True)).astype(o_ref.dtype)


def _cross_attention(q, kv, batch, seq, tq):
    nq = seq // tq
    return pl.pallas_call(
        _xattn_kernel,
        grid=(batch, nq),
        in_specs=[
            pl.BlockSpec((tq, D_MODEL), lambda b, i: (b * nq + i, 0)),
            pl.BlockSpec((N_MEM, D_MODEL), lambda b, i: (b, 0)),
            pl.BlockSpec((N_MEM, D_MODEL), lambda b, i: (b, 1)),
        ],
        out_specs=pl.BlockSpec((tq, D_MODEL), lambda b, i: (b * nq + i, 0)),
        out_shape=jax.ShapeDtypeStruct((batch * seq, D_MODEL), _BF16),
        compiler_params=_params(("parallel", "arbitrary")),
        name="cross_attention",
    )(q, kv, kv)


def _visible_block_counts(positions, tq, tk, strict):
    batch, seq = positions.shape
    qmin = positions.reshape(batch, seq // tq, tq).min(-1)[:, :, None]
    qmax = positions.reshape(batch, seq // tq, tq).max(-1)[:, :, None]
    kmin = positions.reshape(batch, seq // tk, tk).min(-1)[:, None, :]
    kmax = positions.reshape(batch, seq // tk, tk).max(-1)[:, None, :]
    full = (kmax < qmin) if strict else (kmax <= qmin)
    some = (kmin < qmax) if strict else (kmin <= qmax)
    n_full = jnp.sum(full, axis=-1).astype(jnp.int32).reshape(-1)
    n_vis = jnp.sum(some, axis=-1).astype(jnp.int32).reshape(-1)
    return n_full, jnp.maximum(n_vis, n_full)


def _in_proj_col_scale():
    scale = np.ones((1, IN_COLS), np.float32)
    scale[:, DIFF_Q_BLK * HEAD_COLS:(DIFF_Q_BLK + DIFF_HEADS) * HEAD_COLS] = DIFF_QK_DIM ** -0.5 * LOG2E
    scale[:, SB_Q_BLK * HEAD_COLS:(SB_Q_BLK + SB_HEADS) * HEAD_COLS] = SB_DIM ** -0.5 * LOG2E
    return jnp.asarray(scale)


def kernel(x, mem, positions, norm_mix, w_in, lam_q1, lam_k1, lam_q2, lam_k2, subln_diff, subln_sb, w_out,
           norm_x, norm_mem, wq_x, wkv_x, wo_x, norm_mlp, w_up, w_down, norm_final):
    batch, seq, d = x.shape
    depth = w_in.shape[0]
    tokens = batch * seq
    assert d == D_MODEL and w_in.shape[2] == IN_COLS
    assert seq % DIFF_TQ == 0 and seq % DIFF_TK == 0 and seq % SB_TQ == 0 and seq % SB_TK == 0

    pos_f = positions.astype(_F32)
    pos_row = pos_f.reshape(batch, seq)
    pos_rep = jnp.broadcast_to(pos_f.reshape(tokens, 1), (tokens, LANES))
    diff_counts = _visible_block_counts(positions, DIFF_TQ, DIFF_TK, strict=False)
    sb_counts = _visible_block_counts(positions, SB_TQ, SB_TK, strict=True)
    in_scale = _in_proj_col_scale()
    xq_scale = jnp.full((1, d), X_DIM ** -0.5 * LOG2E, _F32)

    w_out_bf, wo_x_bf, w_up_bf, w_down_bf = (w.astype(_BF16) for w in (w_out, wo_x, w_up, w_down))

    h = x.reshape(tokens, d)
    mem2 = mem.reshape(batch * N_MEM, d)
    u = _rmsnorm_bf16(h, norm_mix[0], tm=512)
    out = None
    for l in range(depth):
        lambda_init = 0.8 - 0.6 * math.exp(-0.3 * l)
        qkv = _proj(u, w_in, l, in_scale, tm=1024, tn=1024, name="in_proj")
        o_diff = _diff_attention(qkv, pos_row, pos_rep, *diff_counts,
                                 (lam_q1[l], lam_k1[l], lam_q2[l], lam_k2[l]), subln_diff[l], lambda_init,
                                 batch, seq)
        o_sb = _sb_attention(qkv, pos_row, pos_rep, *sb_counts, subln_sb[l], batch, seq)
        h, ux = _proj_res_norm([o_diff, o_sb], w_out_bf, l, h, norm_x[l], tm=512, name="out_proj")

        mem_n = _rmsnorm_bf16(mem2, norm_mem[l], tm=256)
        kv = _proj(mem_n, wkv_x, l, None, tm=1024, tn=1024, name="xattn_kv_proj")
        qx = _proj(ux, wq_x, l, xq_scale, tm=1024, tn=1024, name="xattn_q_proj")
        ox = _cross_attention(qx, kv, batch, seq, tq=512)
        h, um = _proj_res_norm([ox], wo_x_bf, l, h, norm_mlp[l], tm=512, name="xattn_o_proj")

        last = l == depth - 1
        g_next = norm_final if last else norm_mix[l + 1]
        h, y = _mlp(um, w_up_bf, w_down_bf, l, h, g_next, _F32 if last else _BF16, tm=1024, tf=512)
        if last:
            out = y
        else:
            u = y
    return out.reshape(batch, seq, d)
```

```python
import functools
import math

import jax
import jax.numpy as jnp
import numpy as np
from jax import lax
from jax.experimental import pallas as pl
from jax.experimental.pallas import tpu as pltpu

D_MODEL = 2048
N_MEM = 256
DIFF_V_DIM = 128
DIFF_QK_DIM = 64
DIFF_HEADS = 8
SB_DIM = 128
SB_HEADS = 8
X_HEADS = 4
X_DIM = D_MODEL // X_HEADS
D_FF = 4 * D_MODEL
EPS = 1e-6
NEG_INF = -1e30
DEAD_CARRY = -160.0
LOG2E = math.log2(math.e)

LANES = 128
HEAD_COLS = 128
DIFF_Q_BLK = 0
DIFF_K_BLK = DIFF_HEADS
DIFF_V_BLK = 2 * DIFF_HEADS
SB_Q_BLK = 3 * DIFF_HEADS
SB_K_BLK = SB_Q_BLK + SB_HEADS
SB_V_BLK = SB_K_BLK + SB_HEADS
IN_COLS = (3 * DIFF_HEADS + 3 * SB_HEADS) * HEAD_COLS

DIFF_TQ, DIFF_TK = 512, 512
SB_TQ, SB_TK = 256, 256
DIFF_HEADS_PER_STEP = 4
SB_HEADS_PER_STEP = 8
VMEM_LIMIT = 56 * 1024 * 1024

_BF16 = jnp.bfloat16
_F32 = jnp.float32
_NT = (((1,), (1,)), ((), ()))
_TN = (((0,), (0,)), ((), ()))


def _params(semantics):
    return pltpu.CompilerParams(dimension_semantics=semantics, vmem_limit_bytes=VMEM_LIMIT)


def _rms(x, g):
    return x * lax.rsqrt(jnp.mean(x * x, axis=-1, keepdims=True) + EPS) * g


def _lane_tiles(x):
    return [x[:, c * LANES:(c + 1) * LANES] for c in range(x.shape[1] // LANES)]


def _lane_tile_repeat(tile, width):
    return jnp.concatenate([tile] * (width // LANES), axis=1)


def _lane_replicated_column(vec):
    return jnp.broadcast_to(vec.astype(_F32).reshape(-1, 1), (vec.shape[0], LANES))


def _rms_over_rows(x_t):
    return x_t * lax.rsqrt(jnp.mean(x_t * x_t, axis=0, keepdims=True) + EPS)


def _software_pipeline(stages, n_chains):
    for t in range(n_chains + len(stages) - 1):
        for s, stage in enumerate(stages):
            if 0 <= t - s < n_chains:
                stage(t - s)


def _rmsnorm_kernel(x_ref, g_ref, o_ref):
    o_ref[...] = _rms(x_ref[...], g_ref[...]).astype(o_ref.dtype)


def _rmsnorm_bf16(x, g, tm):
    m, d = x.shape
    return pl.pallas_call(
        _rmsnorm_kernel,
        grid=(m // tm,),
        in_specs=[pl.BlockSpec((tm, d), lambda i: (i, 0)), pl.BlockSpec((1, d), lambda i: (0, 0))],
        out_specs=pl.BlockSpec((tm, d), lambda i: (i, 0)),
        out_shape=jax.ShapeDtypeStruct((m, d), _BF16),
        compiler_params=_params(("parallel",)),
        name="rmsnorm_bf16",
    )(x, g.reshape(1, d))


def _proj_kernel(x_ref, w_ref, *rest):
    o_ref = rest[-1]
    acc = jnp.dot(x_ref[...], w_ref[...].astype(_BF16), preferred_element_type=_F32)
    if len(rest) == 2:
        acc = acc * rest[0][...]
    o_ref[...] = acc.astype(o_ref.dtype)


def _proj(x, w_stack, layer, col_scale, tm, tn, name):
    m, k = x.shape
    n = w_stack.shape[2]
    in_specs = [pl.BlockSpec((tm, k), lambda j, i: (i, 0)), pl.BlockSpec((None, k, tn), lambda j, i: (layer, 0, j))]
    args = [x, w_stack]
    if col_scale is not None:
        in_specs.append(pl.BlockSpec((1, tn), lambda j, i: (0, j)))
        args.append(col_scale)
    return pl.pallas_call(
        _proj_kernel,
        grid=(n // tn, m // tm),
        in_specs=in_specs,
        out_specs=pl.BlockSpec((tm, tn), lambda j, i: (i, j)),
        out_shape=jax.ShapeDtypeStruct((m, n), _BF16),
        compiler_params=_params(("parallel", "arbitrary")),
        name=name,
    )(*args)


def _proj_res_norm_kernel(*refs, n_in, feature_major):
    xs = refs[:n_in]
    ws = refs[n_in:2 * n_in]
    h_ref, g_ref, h_out_ref, u_out_ref = refs[2 * n_in:]
    acc = h_ref[...]
    for x_ref, w_ref in zip(xs, ws):
        dims = _TN if feature_major else (((1,), (0,)), ((), ()))
        acc = acc + lax.dot_general(x_ref[...], w_ref[...], dims, preferred_element_type=_F32)
    h_out_ref[...] = acc
    u_out_ref[...] = _rms(acc, g_ref[...]).astype(u_out_ref.dtype)


def _proj_res_norm(xs, w_stack, layer, h, g, tm, name, feature_major=False):
    m, d = h.shape
    n_in = len(xs)
    if feature_major:
        kx, seq = xs[0].shape[1:]
        tiles = seq // tm
        in_specs = [pl.BlockSpec((None, kx, tm), lambda i: (i // tiles, 0, i % tiles)) for _ in xs]
    else:
        kx = xs[0].shape[1]
        in_specs = [pl.BlockSpec((tm, kx), lambda i: (i, 0)) for _ in xs]
    in_specs += [pl.BlockSpec((None, kx, d), lambda i, r=r: (layer, r, 0)) for r in range(n_in)]
    in_specs += [pl.BlockSpec((tm, d), lambda i: (i, 0)), pl.BlockSpec((1, d), lambda i: (0, 0))]
    return pl.pallas_call(
        functools.partial(_proj_res_norm_kernel, n_in=n_in, feature_major=feature_major),
        grid=(m // tm,),
        in_specs=in_specs,
        out_specs=[pl.BlockSpec((tm, d), lambda i: (i, 0)), pl.BlockSpec((tm, d), lambda i: (i, 0))],
        out_shape=[jax.ShapeDtypeStruct((m, d), _F32), jax.ShapeDtypeStruct((m, d), _BF16)],
        compiler_params=_params(("parallel",)),
        name=name,
    )(*xs, *([w_stack] * n_in), h, g.reshape(1, d))


def _mlp_kernel(u_ref, wu_ref, wd_ref, h_ref, g_ref, h_out_ref, y_out_ref):
    f = pl.program_id(1)

    @pl.when(f == 0)
    def _():
        h_out_ref[...] = h_ref[...]

    up = jnp.dot(u_ref[...], wu_ref[...], preferred_element_type=_F32)
    a = jnp.square(jnp.maximum(up, 0.0)).astype(_BF16)
    h_out_ref[...] += jnp.dot(a, wd_ref[...], preferred_element_type=_F32)

    @pl.when(f == pl.num_programs(1) - 1)
    def _():
        y_out_ref[...] = _rms(h_out_ref[...], g_ref[...]).astype(y_out_ref.dtype)


def _mlp(u, w_up_stack, w_down_stack, layer, h, g, y_dtype, tm, tf):
    m, d = h.shape
    dff = w_up_stack.shape[2]
    return pl.pallas_call(
        _mlp_kernel,
        grid=(m // tm, dff // tf),
        in_specs=[
            pl.BlockSpec((tm, d), lambda i, f: (i, 0)),
            pl.BlockSpec((None, d, tf), lambda i, f: (layer, 0, f)),
            pl.BlockSpec((None, tf, d), lambda i, f: (layer, f, 0)),
            pl.BlockSpec((tm, d), lambda i, f: (i, 0)),
            pl.BlockSpec((1, d), lambda i, f: (0, 0)),
        ],
        out_specs=[pl.BlockSpec((tm, d), lambda i, f: (i, 0)), pl.BlockSpec((tm, d), lambda i, f: (i, 0))],
        out_shape=[jax.ShapeDtypeStruct((m, d), _F32), jax.ShapeDtypeStruct((m, d), y_dtype)],
        compiler_params=_params(("parallel", "arbitrary")),
        name="mlp_relu2",
    )(u, w_up_stack, w_down_stack, h, g.reshape(1, d))


def _diff_attn_kernel(nfull_ref, nvis_ref, slopes_ref,
                      q_ref, k_ref, v_ref, pq_ref, pk_ref, lq1_ref, lk1_ref, lq2_ref, lk2_ref, g_ref,
                      o_ref, m_s, l_s, acc_s, s_s, *, lambda_init, tq, tk, hp):
    b, hg, i = pl.program_id(0), pl.program_id(1), pl.program_id(2)
    nq = pl.num_programs(2)
    n_full = nfull_ref[b * nq + i]
    n_vis = nvis_ref[b * nq + i]
    n_lane_tiles = tq // LANES
    heads = [slice(hh * HEAD_COLS, (hh + 1) * HEAD_COLS) for hh in range(hp)]

    lane = lax.broadcasted_iota(jnp.int32, (tq, HEAD_COLS), 1)
    zero = jnp.zeros((tq, HEAD_COLS), _BF16)
    q_maps = []
    for cols in heads:
        q = q_ref[:, cols]
        q_maps += [jnp.where(lane < DIFF_QK_DIM, q, zero), jnp.where(lane >= DIFF_QK_DIM, q, zero)]
    slope2 = [slopes_ref[hg * hp + hh] * LOG2E for hh in range(hp)]
    pq = pq_ref[0]
    pq0 = pq[:, 0:1]

    m_s[...] = jnp.full(m_s.shape, NEG_INF, _F32)
    l_s[...] = jnp.zeros(l_s.shape, _F32)
    acc_s[...] = jnp.zeros(acc_s.shape, _F32)

    def step(j, masked):
        start = pl.multiple_of(j * tk, tk)
        pk = pk_ref[pl.ds(start, tk), :]
        rel = pk - pq0
        bias = [jnp.concatenate([slope2[hh] * rel] * n_lane_tiles, axis=1) for hh in range(hp)]
        if masked:
            penalty = jnp.concatenate([jnp.where(pq_c >= pk, 0.0, NEG_INF) for pq_c in _lane_tiles(pq)], axis=1)
            bias = [b_h + penalty for b_h in bias]

        def scores(ci):
            k = k_ref[pl.ds(start, tk), heads[ci // 2]]
            s_s[ci] = lax.dot_general(k, q_maps[ci], _NT, preferred_element_type=_F32) + bias[ci // 2]

        def softmax_pv(ci):
            v = v_ref[pl.ds(start, tk), heads[ci // 2]]
            s = s_s[ci]
            m_prev = m_s[ci]
            m_new = jnp.maximum(m_prev, jnp.max(s, axis=0, keepdims=True))
            alpha = jnp.exp2(m_prev - m_new)
            p = jnp.exp2(s - m_new)
            l_s[ci] = alpha * l_s[ci] + jnp.sum(p, axis=0, keepdims=True)
            pv = lax.dot_general(v, p.astype(_BF16), _TN, preferred_element_type=_F32)
            acc_s[ci] = alpha * acc_s[ci] + pv
            m_s[ci] = m_new

        _software_pipeline((scores, softmax_pv), 2 * hp)

    def full_body(j, c):
        step(j, False)
        return c

    def masked_body(j, c):
        step(j, True)
        return c

    lax.fori_loop(0, n_full, full_body, 0)
    lax.fori_loop(n_full, n_vis, masked_body, 0)

    lam = (jnp.exp(jnp.sum(lq1_ref[...] * lk1_ref[...], axis=-1, keepdims=True))
           - jnp.exp(jnp.sum(lq2_ref[...] * lk2_ref[...], axis=-1, keepdims=True)) + lambda_init)
    gain = _lane_tile_repeat(g_ref[...] * (1.0 - lambda_init), tq)
    for hh, cols in enumerate(heads):
        o_t = acc_s[2 * hh] / l_s[2 * hh] - lam * (acc_s[2 * hh + 1] / l_s[2 * hh + 1])
        o_ref[cols, :] = (_rms_over_rows(o_t) * gain).astype(o_ref.dtype)


def _diff_attention(qkv, pos_row, pos_rep, n_full, n_vis, lam_params, g, lambda_init, batch, seq):
    tq, tk, hp = DIFF_TQ, DIFF_TK, DIFF_HEADS_PER_STEP
    nq = seq // tq
    slopes = jnp.asarray(np.array([2.0 ** (-8.0 * (i + 1) / DIFF_HEADS) for i in range(DIFF_HEADS)], np.float32))
    small = lambda b, h, i, *_: (0, 0)
    grid_spec = pltpu.PrefetchScalarGridSpec(
        num_scalar_prefetch=3,
        grid=(batch, DIFF_HEADS // hp, nq),
        in_specs=[
            pl.BlockSpec((tq, hp * HEAD_COLS), lambda b, h, i, *_: (b * nq + i, DIFF_Q_BLK // hp + h)),
            pl.BlockSpec((seq, hp * HEAD_COLS), lambda b, h, i, *_: (b, DIFF_K_BLK // hp + h)),
            pl.BlockSpec((seq, hp * HEAD_COLS), lambda b, h, i, *_: (b, DIFF_V_BLK // hp + h)),
            pl.BlockSpec((1, 1, tq), lambda b, h, i, *_: (b * nq + i, 0, 0)),
            pl.BlockSpec((seq, LANES), lambda b, h, i, *_: (b, 0)),
            pl.BlockSpec((1, DIFF_QK_DIM), small), pl.BlockSpec((1, DIFF_QK_DIM), small),
            pl.BlockSpec((1, DIFF_QK_DIM), small), pl.BlockSpec((1, DIFF_QK_DIM), small),
            pl.BlockSpec((DIFF_V_DIM, LANES), small),
        ],
        out_specs=pl.BlockSpec((None, hp * HEAD_COLS, tq), lambda b, h, i, *_: (b, h, i)),
        scratch_shapes=[pltpu.VMEM((2 * hp, 1, tq), _F32), pltpu.VMEM((2 * hp, 1, tq), _F32),
                        pltpu.VMEM((2 * hp, DIFF_V_DIM, tq), _F32), pltpu.VMEM((2 * hp, tk, tq), _F32)],
    )
    return pl.pallas_call(
        functools.partial(_diff_attn_kernel, lambda_init=lambda_init, tq=tq, tk=tk, hp=hp),
        grid_spec=grid_spec,
        out_shape=jax.ShapeDtypeStruct((batch, DIFF_HEADS * DIFF_V_DIM, seq), _BF16),
        compiler_params=_params(("parallel", "parallel", "arbitrary")),
        name="diff_attention",
    )(n_full, n_vis, slopes, qkv, qkv, qkv, pos_row.reshape(batch * nq, 1, tq), pos_rep,
      *[p.reshape(1, DIFF_QK_DIM) for p in lam_params], _lane_replicated_column(g))


def _sb_attn_kernel(nfull_ref, nvis_ref, q_ref, k_ref, v_ref, pq_ref, pk_ref, ntri_ref, g_ref,
                    o_ref, c_s, w_s, acc_s, z_s, e_s, *, tq, tk, hp):
    b, i = pl.program_id(0), pl.program_id(2)
    nq = pl.num_programs(2)
    n_full = nfull_ref[b * nq + i]
    n_vis = nvis_ref[b * nq + i]
    heads = [slice(hh * HEAD_COLS, (hh + 1) * HEAD_COLS) for hh in range(hp)]
    qs = [q_ref[:, cols] for cols in heads]
    pq = pq_ref[0]
    ntri = ntri_ref[...]

    c_s[...] = jnp.zeros(c_s.shape, _F32)
    acc_s[...] = jnp.zeros(acc_s.shape, _F32)

    def step(j, masked):
        start = pl.multiple_of(j * tk, tk)
        if masked:
            pk = pk_ref[pl.ds(start, tk), :]
            strict = [pq_c > pk for pq_c in _lane_tiles(pq)]
            keep = jnp.concatenate([jnp.where(m, 1.0, 0.0) for m in strict], axis=1)
            penalty = jnp.concatenate([jnp.where(m, 0.0, NEG_INF) for m in strict], axis=1)

        def logits(hh):
            k = k_ref[pl.ds(start, tk), heads[hh]]
            z_s[hh] = lax.dot_general(k, qs[hh], _NT, preferred_element_type=_F32)

        def log_weights(hh):
            z = z_s[hh]
            softplus = jnp.maximum(z, 0.0) + jnp.log2(1.0 + jnp.exp2(-jnp.abs(z)))
            log_sig = z - softplus
            if masked:
                softplus = softplus * keep
            terms = softplus.astype(_BF16)
            later = jnp.dot(ntri, terms, preferred_element_type=_F32)
            e = log_sig + later
            if masked:
                e = e + penalty
            e_s[hh] = e
            carry = c_s[hh]
            w_s[hh] = jnp.exp2(carry)
            c_s[hh] = carry + (later[0:1, :] - terms[0:1, :].astype(_F32))

        def weighted_values(hh):
            v = v_ref[pl.ds(start, tk), heads[hh]]
            a = jnp.exp2(e_s[hh]).astype(_BF16)
            acc_s[hh] += w_s[hh] * lax.dot_general(v, a, _TN, preferred_element_type=_F32)

        _software_pipeline((logits, log_weights, weighted_values), hp)

    def masked_body(t, c):
        step(n_vis - 1 - t, True)
        return c

    lax.fori_loop(0, n_vis - n_full, masked_body, 0)

    def any_weight_left():
        return jnp.max(c_s[...]) > DEAD_CARRY

    def full_cond(state):
        t, go = state
        return jnp.logical_and(t < n_full, go)

    def full_body(state):
        t, _ = state
        step(n_full - 1 - t, False)
        return t + 1, any_weight_left()

    lax.while_loop(full_cond, full_body, (jnp.int32(0), any_weight_left()))
    gain = _lane_tile_repeat(g_ref[...], tq)
    for hh, cols in enumerate(heads):
        o_ref[cols, :] = (_rms_over_rows(acc_s[hh]) * gain).astype(o_ref.dtype)


def _sb_attention(qkv, pos_row, pos_rep, n_full, n_vis, g, batch, seq):
    tq, tk, hp = SB_TQ, SB_TK, SB_HEADS_PER_STEP
    nq = seq // tq
    idx = np.arange(tk)
    ntri = jnp.asarray(-(idx[None, :] > idx[:, None]).astype(np.float32), _BF16)
    grid_spec = pltpu.PrefetchScalarGridSpec(
        num_scalar_prefetch=2,
        grid=(batch, SB_HEADS // hp, nq),
        in_specs=[
            pl.BlockSpec((tq, hp * HEAD_COLS), lambda b, h, i, *_: (b * nq + i, SB_Q_BLK // hp + h)),
            pl.BlockSpec((seq, hp * HEAD_COLS), lambda b, h, i, *_: (b, SB_K_BLK // hp + h)),
            pl.BlockSpec((seq, hp * HEAD_COLS), lambda b, h, i, *_: (b, SB_V_BLK // hp + h)),
            pl.BlockSpec((1, 1, tq), lambda b, h, i, *_: (b * nq + i, 0, 0)),
            pl.BlockSpec((seq, LANES), lambda b, h, i, *_: (b, 0)),
            pl.BlockSpec((tk, tk), lambda b, h, i, *_: (0, 0)),
            pl.BlockSpec((SB_DIM, LANES), lambda b, h, i, *_: (0, 0)),
        ],
        out_specs=pl.BlockSpec((None, hp * HEAD_COLS, tq), lambda b, h, i, *_: (b, h, i)),
        scratch_shapes=[pltpu.VMEM((hp, 1, tq), _F32), pltpu.VMEM((hp, 1, tq), _F32),
                        pltpu.VMEM((hp, SB_DIM, tq), _F32),
                        pltpu.VMEM((hp, tk, tq), _F32), pltpu.VMEM((hp, tk, tq), _F32)],
    )
    return pl.pallas_call(
        functools.partial(_sb_attn_kernel, tq=tq, tk=tk, hp=hp),
        grid_spec=grid_spec,
        out_shape=jax.ShapeDtypeStruct((batch, SB_HEADS * SB_DIM, seq), _BF16),
        compiler_params=_params(("parallel", "parallel", "arbitrary")),
        name="stick_breaking_attention",
    )(n_full, n_vis, qkv, qkv, qkv, pos_row.reshape(batch * nq, 1, tq), pos_rep, ntri, _lane_replicated_column(g))


def _xattn_kernel(q_ref, k_ref, v_ref, o_ref):
    for hd in range(X_HEADS):
        cols = slice(hd * X_DIM, (hd + 1) * X_DIM)
        s = lax.dot_general(q_ref[:, cols], k_ref[:, cols], _NT, preferred_element_type=_F32)
        p = jnp.exp2(s - jnp.max(s, axis=-1, keepdims=True))
        o = jnp.dot(p.astype(_BF16), v_ref[:, cols], preferred_element_type=_F32)
        o_ref[:, cols] = (o / jnp.sum(p, axis=-1, keepdims=True)).astype(o_ref.dtype)


def _cross_attention(q, kv, batch, seq, tq):
    nq = seq // tq
    return pl.pallas_call(
        _xattn_kernel,
        grid=(batch, nq),
        in_specs=[
            pl.BlockSpec((tq, D_MODEL), lambda b, i: (b * nq + i, 0)),
            pl.BlockSpec((N_MEM, D_MODEL), lambda b, i: (b, 0)),
            pl.BlockSpec((N_MEM, D_MODEL), lambda b, i: (b, 1)),
        ],
        out_specs=pl.BlockSpec((tq, D_MODEL), lambda b, i: (b * nq + i, 0)),
        out_shape=jax.ShapeDtypeStruct((batch * seq, D_MODEL), _BF16),
        compiler_params=_params(("parallel", "arbitrary")),
        name="cross_attention",
    )(q, kv, kv)


def _visible_block_counts(positions, tq, tk, strict):
    batch, seq = positions.shape
    qmin = positions.reshape(batch, seq // tq, tq).min(-1)[:, :, None]
    qmax = positions.reshape(batch, seq // tq, tq).max(-1)[:, :, None]
    kmin = positions.reshape(batch, seq // tk, tk).min(-1)[:, None, :]
    kmax = positions.reshape(batch, seq // tk, tk).max(-1)[:, None, :]
    full = (kmax < qmin) if strict else (kmax <= qmin)
    some = (kmin < qmax) if strict else (kmin <= qmax)
    n_full = jnp.sum(full, axis=-1).astype(jnp.int32).reshape(-1)
    n_vis = jnp.sum(some, axis=-1).astype(jnp.int32).reshape(-1)
    return n_full, jnp.maximum(n_vis, n_full)


def _in_proj_col_scale():
    scale = np.ones((1, IN_COLS), np.float32)
    scale[:, DIFF_Q_BLK * HEAD_COLS:(DIFF_Q_BLK + DIFF_HEADS) * HEAD_COLS] = DIFF_QK_DIM ** -0.5 * LOG2E
    scale[:, SB_Q_BLK * HEAD_COLS:(SB_Q_BLK + SB_HEADS) * HEAD_COLS] = SB_DIM ** -0.5 * LOG2E
    return jnp.asarray(scale)


def kernel(x, mem, positions, norm_mix, w_in, lam_q1, lam_k1, lam_q2, lam_k2, subln_diff, subln_sb, w_out,
           norm_x, norm_mem, wq_x, wkv_x, wo_x, norm_mlp, w_up, w_down, norm_final):
    batch, seq, d = x.shape
    depth = w_in.shape[0]
    tokens = batch * seq
    assert d == D_MODEL and w_in.shape[2] == IN_COLS
    assert seq % DIFF_TQ == 0 and seq % DIFF_TK == 0 and seq % SB_TQ == 0 and seq % SB_TK == 0

    pos_f = positions.astype(_F32)
    pos_row = pos_f.reshape(batch, seq)
    pos_rep = jnp.broadcast_to(pos_f.reshape(tokens, 1), (tokens, LANES))
    diff_counts = _visible_block_counts(positions, DIFF_TQ, DIFF_TK, strict=False)
    sb_counts = _visible_block_counts(positions, SB_TQ, SB_TK, strict=True)
    in_scale = _in_proj_col_scale()
    xq_scale = jnp.full((1, d), X_DIM ** -0.5 * LOG2E, _F32)

    w_out_bf, wo_x_bf, w_up_bf, w_down_bf = (w.astype(_BF16) for w in (w_out, wo_x, w_up, w_down))

    h = x.reshape(tokens, d)
    mem2 = mem.reshape(batch * N_MEM, d)
    u = _rmsnorm_bf16(h, norm_mix[0], tm=512)
    out = None
    for l in range(depth):
        lambda_init = 0.8 - 0.6 * math.exp(-0.3 * l)
        qkv = _proj(u, w_in, l, in_scale, tm=1024, tn=1024, name="in_proj")
        o_diff = _diff_attention(qkv, pos_row, pos_rep, *diff_counts,
                                 (lam_q1[l], lam_k1[l], lam_q2[l], lam_k2[l]), subln_diff[l], lambda_init,
                                 batch, seq)
        o_sb = _sb_attention(qkv, pos_row, pos_rep, *sb_counts, subln_sb[l], batch, seq)
        h, ux = _proj_res_norm([o_diff, o_sb], w_out_bf, l, h, norm_x[l], tm=512, name="out_proj",
                               feature_major=True)

        mem_n = _rmsnorm_bf16(mem2, norm_mem[l], tm=256)
        kv = _proj(mem_n, wkv_x, l, None, tm=1024, tn=1024, name="xattn_kv_proj")
        qx = _proj(ux, wq_x, l, xq_scale, tm=1024, tn=1024, name="xattn_q_proj")
        ox = _cross_attention(qx, kv, batch, seq, tq=512)
        h, um = _proj_res_norm([ox], wo_x_bf, l, h, norm_mlp[l], tm=512, name="xattn_o_proj")

        last = l == depth - 1
        g_next = norm_final if last else norm_mix[l + 1]
        h, y = _mlp(um, w_up_bf, w_down_bf, l, h, g_next, _F32 if last else _BF16, tm=512, tf=1024)
        if last:
            out = y
        else:
            u = y
    return out.reshape(batch, seq, d)
```

```python
import functools
import math

import jax
import jax.numpy as jnp
import numpy as np
from jax import lax
from jax.experimental import pallas as pl
from jax.experimental.pallas import tpu as pltpu

D_MODEL = 2048
N_MEM = 256
DIFF_V_DIM = 128
DIFF_QK_DIM = 64
DIFF_HEADS = 8
SB_DIM = 128
SB_HEADS = 8
X_HEADS = 4
X_DIM = D_MODEL // X_HEADS
D_FF = 4 * D_MODEL
EPS = 1e-6
NEG_INF = -1e30
DEAD_CARRY = -160.0
LOG2E = math.log2(math.e)

LANES = 128
ONES_ROWS = 16
HEAD_COLS = 128
DIFF_Q_BLK = 0
DIFF_K_BLK = DIFF_HEADS
DIFF_V_BLK = 2 * DIFF_HEADS
SB_Q_BLK = 3 * DIFF_HEADS
SB_K_BLK = SB_Q_BLK + SB_HEADS
SB_V_BLK = SB_K_BLK + SB_HEADS
IN_COLS = (3 * DIFF_HEADS + 3 * SB_HEADS) * HEAD_COLS

DIFF_TQ, DIFF_TK = 512, 512
SB_TQ, SB_TK = 256, 256
DIFF_HEADS_PER_STEP = 4
SB_HEADS_PER_STEP = 8
VMEM_LIMIT = 56 * 1024 * 1024

_BF16 = jnp.bfloat16
_F32 = jnp.float32
_NT = (((1,), (1,)), ((), ()))
_TN = (((0,), (0,)), ((), ()))


def _params(semantics):
    return pltpu.CompilerParams(dimension_semantics=semantics, vmem_limit_bytes=VMEM_LIMIT)


def _rms(x, g):
    return x * lax.rsqrt(jnp.mean(x * x, axis=-1, keepdims=True) + EPS) * g


def _lane_tiles(x):
    return [x[:, c * LANES:(c + 1) * LANES] for c in range(x.shape[1] // LANES)]


def _lane_tile_repeat(tile, width):
    return jnp.concatenate([tile] * (width // LANES), axis=1)


def _lane_replicated_column(vec):
    return jnp.broadcast_to(vec.astype(_F32).reshape(-1, 1), (vec.shape[0], LANES))


def _rms_over_rows(x_t):
    return x_t * lax.rsqrt(jnp.mean(x_t * x_t, axis=0, keepdims=True) + EPS)


def _software_pipeline(stages, n_chains):
    for t in range(n_chains + len(stages) - 1):
        for s, stage in enumerate(stages):
            if 0 <= t - s < n_chains:
                stage(t - s)


def _rmsnorm_kernel(x_ref, g_ref, o_ref):
    o_ref[...] = _rms(x_ref[...], g_ref[...]).astype(o_ref.dtype)


def _rmsnorm_bf16(x, g, tm):
    m, d = x.shape
    return pl.pallas_call(
        _rmsnorm_kernel,
        grid=(m // tm,),
        in_specs=[pl.BlockSpec((tm, d), lambda i: (i, 0)), pl.BlockSpec((1, d), lambda i: (0, 0))],
        out_specs=pl.BlockSpec((tm, d), lambda i: (i, 0)),
        out_shape=jax.ShapeDtypeStruct((m, d), _BF16),
        compiler_params=_params(("parallel",)),
        name="rmsnorm_bf16",
    )(x, g.reshape(1, d))


def _proj_kernel(x_ref, w_ref, *rest):
    o_ref = rest[-1]
    acc = jnp.dot(x_ref[...], w_ref[...].astype(_BF16), preferred_element_type=_F32)
    if len(rest) == 2:
        acc = acc * rest[0][...]
    o_ref[...] = acc.astype(o_ref.dtype)


def _proj(x, w_stack, layer, col_scale, tm, tn, name):
    m, k = x.shape
    n = w_stack.shape[2]
    in_specs = [pl.BlockSpec((tm, k), lambda j, i: (i, 0)), pl.BlockSpec((None, k, tn), lambda j, i: (layer, 0, j))]
    args = [x, w_stack]
    if col_scale is not None:
        in_specs.append(pl.BlockSpec((1, tn), lambda j, i: (0, j)))
        args.append(col_scale)
    return pl.pallas_call(
        _proj_kernel,
        grid=(n // tn, m // tm),
        in_specs=in_specs,
        out_specs=pl.BlockSpec((tm, tn), lambda j, i: (i, j)),
        out_shape=jax.ShapeDtypeStruct((m, n), _BF16),
        compiler_params=_params(("parallel", "arbitrary")),
        name=name,
    )(*args)


def _proj_res_norm_kernel(*refs, n_in, feature_major):
    xs = refs[:n_in]
    ws = refs[n_in:2 * n_in]
    h_ref, g_ref, h_out_ref, u_out_ref = refs[2 * n_in:]
    acc = h_ref[...]
    for x_ref, w_ref in zip(xs, ws):
        dims = _TN if feature_major else (((1,), (0,)), ((), ()))
        acc = acc + lax.dot_general(x_ref[...], w_ref[...], dims, preferred_element_type=_F32)
    h_out_ref[...] = acc
    u_out_ref[...] = _rms(acc, g_ref[...]).astype(u_out_ref.dtype)


def _proj_res_norm(xs, w_stack, layer, h, g, tm, name, feature_major=False):
    m, d = h.shape
    n_in = len(xs)
    if feature_major:
        kx, seq = xs[0].shape[1:]
        tiles = seq // tm
        in_specs = [pl.BlockSpec((None, kx, tm), lambda i: (i // tiles, 0, i % tiles)) for _ in xs]
    else:
        kx = xs[0].shape[1]
        in_specs = [pl.BlockSpec((tm, kx), lambda i: (i, 0)) for _ in xs]
    in_specs += [pl.BlockSpec((None, kx, d), lambda i, r=r: (layer, r, 0)) for r in range(n_in)]
    in_specs += [pl.BlockSpec((tm, d), lambda i: (i, 0)), pl.BlockSpec((1, d), lambda i: (0, 0))]
    return pl.pallas_call(
        functools.partial(_proj_res_norm_kernel, n_in=n_in, feature_major=feature_major),
        grid=(m // tm,),
        in_specs=in_specs,
        out_specs=[pl.BlockSpec((tm, d), lambda i: (i, 0)), pl.BlockSpec((tm, d), lambda i: (i, 0))],
        out_shape=[jax.ShapeDtypeStruct((m, d), _F32), jax.ShapeDtypeStruct((m, d), _BF16)],
        compiler_params=_params(("parallel",)),
        name=name,
    )(*xs, *([w_stack] * n_in), h, g.reshape(1, d))


def _mlp_kernel(u_ref, wu_ref, wd_ref, h_ref, g_ref, h_out_ref, y_out_ref):
    f = pl.program_id(1)

    @pl.when(f == 0)
    def _():
        h_out_ref[...] = h_ref[...]

    up = jnp.dot(u_ref[...], wu_ref[...], preferred_element_type=_F32)
    a = jnp.square(jnp.maximum(up, 0.0)).astype(_BF16)
    h_out_ref[...] += jnp.dot(a, wd_ref[...], preferred_element_type=_F32)

    @pl.when(f == pl.num_programs(1) - 1)
    def _():
        y_out_ref[...] = _rms(h_out_ref[...], g_ref[...]).astype(y_out_ref.dtype)


def _mlp(u, w_up_stack, w_down_stack, layer, h, g, y_dtype, tm, tf):
    m, d = h.shape
    dff = w_up_stack.shape[2]
    return pl.pallas_call(
        _mlp_kernel,
        grid=(m // tm, dff // tf),
        in_specs=[
            pl.BlockSpec((tm, d), lambda i, f: (i, 0)),
            pl.BlockSpec((None, d, tf), lambda i, f: (layer, 0, f)),
            pl.BlockSpec((None, tf, d), lambda i, f: (layer, f, 0)),
            pl.BlockSpec((tm, d), lambda i, f: (i, 0)),
            pl.BlockSpec((1, d), lambda i, f: (0, 0)),
        ],
        out_specs=[pl.BlockSpec((tm, d), lambda i, f: (i, 0)), pl.BlockSpec((tm, d), lambda i, f: (i, 0))],
        out_shape=[jax.ShapeDtypeStruct((m, d), _F32), jax.ShapeDtypeStruct((m, d), y_dtype)],
        compiler_params=_params(("parallel", "arbitrary")),
        name="mlp_relu2",
    )(u, w_up_stack, w_down_stack, h, g.reshape(1, d))


def _diff_attn_kernel(nfull_ref, nvis_ref, slopes_ref,
                      q_ref, k_ref, v_ref, pq_ref, pk_ref, lq1_ref, lk1_ref, lq2_ref, lk2_ref, g_ref,
                      o_ref, m_s, acc_s, s_s, *, lambda_init, tq, tk, hp):
    b, hg, i = pl.program_id(0), pl.program_id(1), pl.program_id(2)
    nq = pl.num_programs(2)
    n_full = nfull_ref[b * nq + i]
    n_vis = nvis_ref[b * nq + i]
    n_lane_tiles = tq // LANES
    heads = [slice(hh * HEAD_COLS, (hh + 1) * HEAD_COLS) for hh in range(hp)]

    lane = lax.broadcasted_iota(jnp.int32, (tq, HEAD_COLS), 1)
    zero = jnp.zeros((tq, HEAD_COLS), _BF16)
    q_maps = []
    for cols in heads:
        q = q_ref[:, cols]
        q_maps += [jnp.where(lane < DIFF_QK_DIM, q, zero), jnp.where(lane >= DIFF_QK_DIM, q, zero)]
    slope2 = [slopes_ref[hg * hp + hh] * LOG2E for hh in range(hp)]
    pq = pq_ref[0]
    pq0 = pq[:, 0:1]

    m_s[...] = jnp.full(m_s.shape, NEG_INF, _F32)
    acc_s[...] = jnp.zeros(acc_s.shape, _F32)
    ones_rows = jnp.ones((ONES_ROWS, tk), _BF16)

    n_chains = 2 * hp

    def softmax_pv(ci, j):
        v = v_ref[pl.ds(pl.multiple_of(j * tk, tk), tk), heads[ci // 2]]
        v_t = jnp.concatenate([v.T, ones_rows], axis=0)
        s = s_s[ci]
        m_prev = m_s[ci]
        m_new = jnp.maximum(m_prev, jnp.max(s, axis=0, keepdims=True))
        alpha = jnp.exp2(m_prev - m_new)
        p = jnp.exp2(s - m_new).astype(_BF16)
        acc_s[ci] = alpha * acc_s[ci] + jnp.dot(v_t, p, preferred_element_type=_F32)
        m_s[ci] = m_new

    def step(j, masked, first):
        start = pl.multiple_of(j * tk, tk)
        pk = pk_ref[pl.ds(start, tk), :]
        rel = pk - pq0
        bias = [jnp.concatenate([slope2[hh] * rel] * n_lane_tiles, axis=1) for hh in range(hp)]
        if masked:
            penalty = jnp.concatenate([jnp.where(pq_c >= pk, 0.0, NEG_INF) for pq_c in _lane_tiles(pq)], axis=1)
            bias = [b_h + penalty for b_h in bias]

        def scores(ci):
            k = k_ref[pl.ds(start, tk), heads[ci // 2]]
            s_s[ci] = lax.dot_general(k, q_maps[ci], _NT, preferred_element_type=_F32) + bias[ci // 2]

        scores(0)
        if not first:
            softmax_pv(n_chains - 1, j - 1)
        for ci in range(n_chains - 1):
            scores(ci + 1)
            softmax_pv(ci, j)

    @pl.when(n_full > 0)
    def _():
        step(0, masked=False, first=True)

    @pl.when(n_full == 0)
    def _():
        step(0, masked=True, first=True)

    def full_body(j, c):
        step(j, masked=False, first=False)
        return c

    def masked_body(j, c):
        step(j, masked=True, first=False)
        return c

    lax.fori_loop(1, n_full, full_body, 0)
    lax.fori_loop(jnp.maximum(n_full, 1), n_vis, masked_body, 0)
    softmax_pv(n_chains - 1, n_vis - 1)

    lam = (jnp.exp(jnp.sum(lq1_ref[...] * lk1_ref[...], axis=-1, keepdims=True))
           - jnp.exp(jnp.sum(lq2_ref[...] * lk2_ref[...], axis=-1, keepdims=True)) + lambda_init)
    gain = _lane_tile_repeat(g_ref[...] * (1.0 - lambda_init), tq)
    dv = DIFF_V_DIM
    for hh, cols in enumerate(heads):
        acc1, acc2 = acc_s[2 * hh], acc_s[2 * hh + 1]
        o_t = acc1[:dv] / acc1[dv:dv + 1] - lam * (acc2[:dv] / acc2[dv:dv + 1])
        o_ref[cols, :] = (_rms_over_rows(o_t) * gain).astype(o_ref.dtype)


def _diff_attention(qkv, pos_row, pos_rep, n_full, n_vis, lam_params, g, lambda_init, batch, seq):
    tq, tk, hp = DIFF_TQ, DIFF_TK, DIFF_HEADS_PER_STEP
    nq = seq // tq
    slopes = jnp.asarray(np.array([2.0 ** (-8.0 * (i + 1) / DIFF_HEADS) for i in range(DIFF_HEADS)], np.float32))
    small = lambda b, h, i, *_: (0, 0)
    grid_spec = pltpu.PrefetchScalarGridSpec(
        num_scalar_prefetch=3,
        grid=(batch, DIFF_HEADS // hp, nq),
        in_specs=[
            pl.BlockSpec((tq, hp * HEAD_COLS), lambda b, h, i, *_: (b * nq + i, DIFF_Q_BLK // hp + h)),
            pl.BlockSpec((seq, hp * HEAD_COLS), lambda b, h, i, *_: (b, DIFF_K_BLK // hp + h)),
            pl.BlockSpec((seq, hp * HEAD_COLS), lambda b, h, i, *_: (b, DIFF_V_BLK // hp + h)),
            pl.BlockSpec((1, 1, tq), lambda b, h, i, *_: (b * nq + i, 0, 0)),
            pl.BlockSpec((seq, LANES), lambda b, h, i, *_: (b, 0)),
            pl.BlockSpec((1, DIFF_QK_DIM), small), pl.BlockSpec((1, DIFF_QK_DIM), small),
            pl.BlockSpec((1, DIFF_QK_DIM), small), pl.BlockSpec((1, DIFF_QK_DIM), small),
            pl.BlockSpec((DIFF_V_DIM, LANES), small),
        ],
        out_specs=pl.BlockSpec((None, hp * HEAD_COLS, tq), lambda b, h, i, *_: (b, h, i)),
        scratch_shapes=[pltpu.VMEM((2 * hp, 1, tq), _F32),
                        pltpu.VMEM((2 * hp, DIFF_V_DIM + ONES_ROWS, tq), _F32), pltpu.VMEM((2 * hp, tk, tq), _F32)],
    )
    return pl.pallas_call(
        functools.partial(_diff_attn_kernel, lambda_init=lambda_init, tq=tq, tk=tk, hp=hp),
        grid_spec=grid_spec,
        out_shape=jax.ShapeDtypeStruct((batch, DIFF_HEADS * DIFF_V_DIM, seq), _BF16),
        compiler_params=_params(("parallel", "parallel", "arbitrary")),
        name="diff_attention",
    )(n_full, n_vis, slopes, qkv, qkv, qkv, pos_row.reshape(batch * nq, 1, tq), pos_rep,
      *[p.reshape(1, DIFF_QK_DIM) for p in lam_params], _lane_replicated_column(g))


def _sb_attn_kernel(nfull_ref, nvis_ref, q_ref, k_ref, v_ref, pq_ref, pk_ref, ntri_ref, g_ref,
                    o_ref, c_s, w_s, acc_s, z_s, e_s, *, tq, tk, hp):
    b, i = pl.program_id(0), pl.program_id(2)
    nq = pl.num_programs(2)
    n_full = nfull_ref[b * nq + i]
    n_vis = nvis_ref[b * nq + i]
    heads = [slice(hh * HEAD_COLS, (hh + 1) * HEAD_COLS) for hh in range(hp)]
    qs = [q_ref[:, cols] for cols in heads]
    pq = pq_ref[0]
    ntri = ntri_ref[...]

    c_s[...] = jnp.zeros(c_s.shape, _F32)
    acc_s[...] = jnp.zeros(acc_s.shape, _F32)

    def step(blocks):
        starts, keeps, penalties = [], [], []
        for j, masked in blocks:
            start = pl.multiple_of(j * tk, tk)
            starts.append(start)
            if masked:
                pk = pk_ref[pl.ds(start, tk), :]
                strict = [pq_c > pk for pq_c in _lane_tiles(pq)]
                keeps.append(jnp.concatenate([jnp.where(m, 1.0, 0.0) for m in strict], axis=1))
                penalties.append(jnp.concatenate([jnp.where(m, 0.0, NEG_INF) for m in strict], axis=1))
            else:
                keeps.append(None)
                penalties.append(None)

        def logits(c):
            blk, hh = divmod(c, hp)
            k = k_ref[pl.ds(starts[blk], tk), heads[hh]]
            z_s[c] = lax.dot_general(k, qs[hh], _NT, preferred_element_type=_F32)

        def log_weights(c):
            blk, hh = divmod(c, hp)
            z = z_s[c]
            softplus = jnp.maximum(z, 0.0) + jnp.log2(1.0 + jnp.exp2(-jnp.abs(z)))
            log_sig = z - softplus
            if keeps[blk] is not None:
                softplus = softplus * keeps[blk]
            terms = softplus.astype(_BF16)
            later = jnp.dot(ntri, terms, preferred_element_type=_F32)
            e = log_sig + later
            if penalties[blk] is not None:
                e = e + penalties[blk]
            e_s[c] = e
            carry = c_s[hh]
            w_s[c] = jnp.exp2(carry)
            c_s[hh] = carry + (later[0:1, :] - terms[0:1, :].astype(_F32))

        def weighted_values(c):
            blk, hh = divmod(c, hp)
            v = v_ref[pl.ds(starts[blk], tk), heads[hh]]
            a = jnp.exp2(e_s[c]).astype(_BF16)
            acc_s[hh] += w_s[c] * lax.dot_general(v, a, _TN, preferred_element_type=_F32)

        _software_pipeline((logits, log_weights, weighted_values), hp * len(blocks))

    n_masked = n_vis - n_full

    def masked_body(t, c):
        step([(n_vis - 1 - t, True)])
        return c

    lax.fori_loop(0, n_masked - 1, masked_body, 0)
    fuse = jnp.logical_and(n_masked > 0, n_full > 0)

    @pl.when(fuse)
    def _():
        step([(n_full, True), (n_full - 1, False)])

    @pl.when(jnp.logical_and(n_masked > 0, n_full == 0))
    def _():
        step([(n_full, True)])

    def any_weight_left():
        return jnp.max(c_s[...]) > DEAD_CARRY

    def full_cond(state):
        t, go = state
        return jnp.logical_and(t < n_full, go)

    def full_body(state):
        t, _ = state
        step([(n_full - 1 - t, False)])
        return t + 1, any_weight_left()

    lax.while_loop(full_cond, full_body, (fuse.astype(jnp.int32), any_weight_left()))
    gain = _lane_tile_repeat(g_ref[...], tq)
    for hh, cols in enumerate(heads):
        o_ref[cols, :] = (_rms_over_rows(acc_s[hh]) * gain).astype(o_ref.dtype)


def _sb_attention(qkv, pos_row, pos_rep, n_full, n_vis, g, batch, seq):
    tq, tk, hp = SB_TQ, SB_TK, SB_HEADS_PER_STEP
    nq = seq // tq
    idx = np.arange(tk)
    ntri = jnp.asarray(-(idx[None, :] > idx[:, None]).astype(np.float32), _BF16)
    grid_spec = pltpu.PrefetchScalarGridSpec(
        num_scalar_prefetch=2,
        grid=(batch, SB_HEADS // hp, nq),
        in_specs=[
            pl.BlockSpec((tq, hp * HEAD_COLS), lambda b, h, i, *_: (b * nq + i, SB_Q_BLK // hp + h)),
            pl.BlockSpec((seq, hp * HEAD_COLS), lambda b, h, i, *_: (b, SB_K_BLK // hp + h)),
            pl.BlockSpec((seq, hp * HEAD_COLS), lambda b, h, i, *_: (b, SB_V_BLK // hp + h)),
            pl.BlockSpec((1, 1, tq), lambda b, h, i, *_: (b * nq + i, 0, 0)),
            pl.BlockSpec((seq, LANES), lambda b, h, i, *_: (b, 0)),
            pl.BlockSpec((tk, tk), lambda b, h, i, *_: (0, 0)),
            pl.BlockSpec((SB_DIM, LANES), lambda b, h, i, *_: (0, 0)),
        ],
        out_specs=pl.BlockSpec((None, hp * HEAD_COLS, tq), lambda b, h, i, *_: (b, h, i)),
        scratch_shapes=[pltpu.VMEM((hp, 1, tq), _F32), pltpu.VMEM((2 * hp, 1, tq), _F32),
                        pltpu.VMEM((hp, SB_DIM, tq), _F32),
                        pltpu.VMEM((2 * hp, tk, tq), _F32), pltpu.VMEM((2 * hp, tk, tq), _F32)],
    )
    return pl.pallas_call(
        functools.partial(_sb_attn_kernel, tq=tq, tk=tk, hp=hp),
        grid_spec=grid_spec,
        out_shape=jax.ShapeDtypeStruct((batch, SB_HEADS * SB_DIM, seq), _BF16),
        compiler_params=_params(("parallel", "parallel", "arbitrary")),
        name="stick_breaking_attention",
    )(n_full, n_vis, qkv, qkv, qkv, pos_row.reshape(batch * nq, 1, tq), pos_rep, ntri, _lane_replicated_column(g))


def _xattn_kernel(q_ref, k_ref, v_ref, o_ref):
    for hd in range(X_HEADS):
        cols = slice(hd * X_DIM, (hd + 1) * X_DIM)
        s = lax.dot_general(q_ref[:, cols], k_ref[:, cols], _NT, preferred_element_type=_F32)
        p = jnp.exp2(s - jnp.max(s, axis=-1, keepdims=True))
        o = jnp.dot(p.astype(_BF16), v_ref[:, cols], preferred_element_type=_F32)
        o_ref[:, cols] = (o / jnp.sum(p, axis=-1, keepdims=True)).astype(o_ref.dtype)


def _cross_attention(q, kv, batch, seq, tq):
    nq = seq // tq
    return pl.pallas_call(
        _xattn_kernel,
        grid=(batch, nq),
        in_specs=[
            pl.BlockSpec((tq, D_MODEL), lambda b, i: (b * nq + i, 0)),
            pl.BlockSpec((N_MEM, D_MODEL), lambda b, i: (b, 0)),
            pl.BlockSpec((N_MEM, D_MODEL), lambda b, i: (b, 1)),
        ],
        out_specs=pl.BlockSpec((tq, D_MODEL), lambda b, i: (b * nq + i, 0)),
        out_shape=jax.ShapeDtypeStruct((batch * seq, D_MODEL), _BF16),
        compiler_params=_params(("parallel", "arbitrary")),
        name="cross_attention",
    )(q, kv, kv)


def _visible_block_counts(positions, tq, tk, strict):
    batch, seq = positions.shape
    qmin = positions.reshape(batch, seq // tq, tq).min(-1)[:, :, None]
    qmax = positions.reshape(batch, seq // tq, tq).max(-1)[:, :, None]
    kmin = positions.reshape(batch, seq // tk, tk).min(-1)[:, None, :]
    kmax = positions.reshape(batch, seq // tk, tk).max(-1)[:, None, :]
    full = (kmax < qmin) if strict else (kmax <= qmin)
    some = (kmin < qmax) if strict else (kmin <= qmax)
    n_full = jnp.sum(full, axis=-1).astype(jnp.int32).reshape(-1)
    n_vis = jnp.sum(some, axis=-1).astype(jnp.int32).reshape(-1)
    return n_full, jnp.maximum(n_vis, n_full)


def _in_proj_col_scale():
    scale = np.ones((1, IN_COLS), np.float32)
    scale[:, DIFF_Q_BLK * HEAD_COLS:(DIFF_Q_BLK + DIFF_HEADS) * HEAD_COLS] = DIFF_QK_DIM ** -0.5 * LOG2E
    scale[:, SB_Q_BLK * HEAD_COLS:(SB_Q_BLK + SB_HEADS) * HEAD_COLS] = SB_DIM ** -0.5 * LOG2E
    return jnp.asarray(scale)


def kernel(x, mem, positions, norm_mix, w_in, lam_q1, lam_k1, lam_q2, lam_k2, subln_diff, subln_sb, w_out,
           norm_x, norm_mem, wq_x, wkv_x, wo_x, norm_mlp, w_up, w_down, norm_final):
    batch, seq, d = x.shape
    depth = w_in.shape[0]
    tokens = batch * seq
    assert d == D_MODEL and w_in.shape[2] == IN_COLS
    assert seq % DIFF_TQ == 0 and seq % DIFF_TK == 0 and seq % SB_TQ == 0 and seq % SB_TK == 0

    pos_f = positions.astype(_F32)
    pos_row = pos_f.reshape(batch, seq)
    pos_rep = jnp.broadcast_to(pos_f.reshape(tokens, 1), (tokens, LANES))
    diff_counts = _visible_block_counts(positions, DIFF_TQ, DIFF_TK, strict=False)
    sb_counts = _visible_block_counts(positions, SB_TQ, SB_TK, strict=True)
    in_scale = _in_proj_col_scale()
    xq_scale = jnp.full((1, d), X_DIM ** -0.5 * LOG2E, _F32)

    w_out_bf, wo_x_bf, w_up_bf, w_down_bf = (w.astype(_BF16) for w in (w_out, wo_x, w_up, w_down))

    h = x.reshape(tokens, d)
    mem2 = mem.reshape(batch * N_MEM, d)
    u = _rmsnorm_bf16(h, norm_mix[0], tm=512)
    out = None
    for l in range(depth):
        lambda_init = 0.8 - 0.6 * math.exp(-0.3 * l)
        qkv = _proj(u, w_in, l, in_scale, tm=1024, tn=1024, name="in_proj")
        o_diff = _diff_attention(qkv, pos_row, pos_rep, *diff_counts,
                                 (lam_q1[l], lam_k1[l], lam_q2[l], lam_k2[l]), subln_diff[l], lambda_init,
                                 batch, seq)
        o_sb = _sb_attention(qkv, pos_row, pos_rep, *sb_counts, subln_sb[l], batch, seq)
        h, ux = _proj_res_norm([o_diff, o_sb], w_out_bf, l, h, norm_x[l], tm=512, name="out_proj",
                               feature_major=True)

        mem_n = _rmsnorm_bf16(mem2, norm_mem[l], tm=256)
        kv = _proj(mem_n, wkv_x, l, None, tm=1024, tn=1024, name="xattn_kv_proj")
        qx = _proj(ux, wq_x, l, xq_scale, tm=1024, tn=1024, name="xattn_q_proj")
        ox = _cross_attention(qx, kv, batch, seq, tq=512)
        h, um = _proj_res_norm([ox], wo_x_bf, l, h, norm_mlp[l], tm=512, name="xattn_o_proj")

        last = l == depth - 1
        g_next = norm_final if last else norm_mix[l + 1]
        h, y = _mlp(um, w_up_bf, w_down_bf, l, h, g_next, _F32 if last else _BF16, tm=512, tf=1024)
        if last:
            out = y
        else:
            u = y
    return out.reshape(batch, seq, d)
```

```python
import functools
import math

import jax
import jax.numpy as jnp
import numpy as np
from jax import lax
from jax.experimental import pallas as pl
from jax.experimental.pallas import tpu as pltpu

D_MODEL = 2048
N_MEM = 256
DIFF_V_DIM = 128
DIFF_QK_DIM = 64
DIFF_HEADS = 8
SB_DIM = 128
SB_HEADS = 8
X_HEADS = 4
X_DIM = D_MODEL // X_HEADS
D_FF = 4 * D_MODEL
EPS = 1e-6
NEG_INF = -1e30
DEAD_CARRY = -160.0
LOG2E = math.log2(math.e)

LANES = 128
ONES_ROWS = 16
HEAD_COLS = 128
DIFF_Q_BLK = 0
DIFF_K_BLK = DIFF_HEADS
DIFF_V_BLK = 2 * DIFF_HEADS
SB_Q_BLK = 3 * DIFF_HEADS
SB_K_BLK = SB_Q_BLK + SB_HEADS
SB_V_BLK = SB_K_BLK + SB_HEADS
IN_COLS = (3 * DIFF_HEADS + 3 * SB_HEADS) * HEAD_COLS

DIFF_TQ, DIFF_TK = 512, 512
SB_TQ, SB_TK = 256, 256
DIFF_HEADS_PER_STEP = 4
SB_HEADS_PER_STEP = 8
VMEM_LIMIT = 56 * 1024 * 1024

_BF16 = jnp.bfloat16
_F32 = jnp.float32
_NT = (((1,), (1,)), ((), ()))
_TN = (((0,), (0,)), ((), ()))


def _params(semantics):
    return pltpu.CompilerParams(dimension_semantics=semantics, vmem_limit_bytes=VMEM_LIMIT)


def _rms(x, g):
    return x * lax.rsqrt(jnp.mean(x * x, axis=-1, keepdims=True) + EPS) * g


def _lane_tiles(x):
    return [x[:, c * LANES:(c + 1) * LANES] for c in range(x.shape[1] // LANES)]


def _lane_tile_repeat(tile, width):
    return jnp.concatenate([tile] * (width // LANES), axis=1)


def _lane_replicated_column(vec):
    return jnp.broadcast_to(vec.astype(_F32).reshape(-1, 1), (vec.shape[0], LANES))


def _rms_over_rows(x_t):
    return x_t * lax.rsqrt(jnp.mean(x_t * x_t, axis=0, keepdims=True) + EPS)


def _software_pipeline(stages, n_chains):
    for t in range(n_chains + len(stages) - 1):
        for s, stage in enumerate(stages):
            if 0 <= t - s < n_chains:
                stage(t - s)


def _rmsnorm_kernel(x_ref, g_ref, o_ref):
    o_ref[...] = _rms(x_ref[...], g_ref[...]).astype(o_ref.dtype)


def _rmsnorm_bf16(x, g, tm):
    m, d = x.shape
    return pl.pallas_call(
        _rmsnorm_kernel,
        grid=(m // tm,),
        in_specs=[pl.BlockSpec((tm, d), lambda i: (i, 0)), pl.BlockSpec((1, d), lambda i: (0, 0))],
        out_specs=pl.BlockSpec((tm, d), lambda i: (i, 0)),
        out_shape=jax.ShapeDtypeStruct((m, d), _BF16),
        compiler_params=_params(("parallel",)),
        name="rmsnorm_bf16",
    )(x, g.reshape(1, d))


def _proj_kernel(x_ref, w_ref, *rest):
    o_ref = rest[-1]
    acc = jnp.dot(x_ref[...], w_ref[...].astype(_BF16), preferred_element_type=_F32)
    if len(rest) == 2:
        acc = acc * rest[0][...]
    o_ref[...] = acc.astype(o_ref.dtype)


def _proj(x, w_stack, layer, col_scale, tm, tn, name):
    m, k = x.shape
    n = w_stack.shape[2]
    in_specs = [pl.BlockSpec((tm, k), lambda j, i: (i, 0)), pl.BlockSpec((None, k, tn), lambda j, i: (layer, 0, j))]
    args = [x, w_stack]
    if col_scale is not None:
        in_specs.append(pl.BlockSpec((1, tn), lambda j, i: (0, j)))
        args.append(col_scale)
    return pl.pallas_call(
        _proj_kernel,
        grid=(n // tn, m // tm),
        in_specs=in_specs,
        out_specs=pl.BlockSpec((tm, tn), lambda j, i: (i, j)),
        out_shape=jax.ShapeDtypeStruct((m, n), _BF16),
        compiler_params=_params(("parallel", "arbitrary")),
        name=name,
    )(*args)


def _proj_res_norm_kernel(*refs, n_in, feature_major):
    xs = refs[:n_in]
    ws = refs[n_in:2 * n_in]
    h_ref, g_ref, h_out_ref, u_out_ref = refs[2 * n_in:]
    acc = h_ref[...]
    for x_ref, w_ref in zip(xs, ws):
        dims = _TN if feature_major else (((1,), (0,)), ((), ()))
        acc = acc + lax.dot_general(x_ref[...], w_ref[...], dims, preferred_element_type=_F32)
    h_out_ref[...] = acc
    u_out_ref[...] = _rms(acc, g_ref[...]).astype(u_out_ref.dtype)


def _proj_res_norm(xs, w_stack, layer, h, g, tm, name, feature_major=False):
    m, d = h.shape
    n_in = len(xs)
    if feature_major:
        kx, seq = xs[0].shape[1:]
        tiles = seq // tm
        in_specs = [pl.BlockSpec((None, kx, tm), lambda i: (i // tiles, 0, i % tiles)) for _ in xs]
    else:
        kx = xs[0].shape[1]
        in_specs = [pl.BlockSpec((tm, kx), lambda i: (i, 0)) for _ in xs]
    in_specs += [pl.BlockSpec((None, kx, d), lambda i, r=r: (layer, r, 0)) for r in range(n_in)]
    in_specs += [pl.BlockSpec((tm, d), lambda i: (i, 0)), pl.BlockSpec((1, d), lambda i: (0, 0))]
    return pl.pallas_call(
        functools.partial(_proj_res_norm_kernel, n_in=n_in, feature_major=feature_major),
        grid=(m // tm,),
        in_specs=in_specs,
        out_specs=[pl.BlockSpec((tm, d), lambda i: (i, 0)), pl.BlockSpec((tm, d), lambda i: (i, 0))],
        out_shape=[jax.ShapeDtypeStruct((m, d), _F32), jax.ShapeDtypeStruct((m, d), _BF16)],
        compiler_params=_params(("parallel",)),
        name=name,
    )(*xs, *([w_stack] * n_in), h, g.reshape(1, d))


def _mlp_kernel(u_ref, wu_ref, wd_ref, h_ref, g_ref, h_out_ref, y_out_ref):
    f = pl.program_id(1)

    @pl.when(f == 0)
    def _():
        h_out_ref[...] = h_ref[...]

    up = jnp.dot(u_ref[...], wu_ref[...], preferred_element_type=_F32)
    a = jnp.square(jnp.maximum(up, 0.0)).astype(_BF16)
    h_out_ref[...] += jnp.dot(a, wd_ref[...], preferred_element_type=_F32)

    @pl.when(f == pl.num_programs(1) - 1)
    def _():
        y_out_ref[...] = _rms(h_out_ref[...], g_ref[...]).astype(y_out_ref.dtype)


def _mlp(u, w_up_stack, w_down_stack, layer, h, g, y_dtype, tm, tf):
    m, d = h.shape
    dff = w_up_stack.shape[2]
    return pl.pallas_call(
        _mlp_kernel,
        grid=(m // tm, dff // tf),
        in_specs=[
            pl.BlockSpec((tm, d), lambda i, f: (i, 0)),
            pl.BlockSpec((None, d, tf), lambda i, f: (layer, 0, f)),
            pl.BlockSpec((None, tf, d), lambda i, f: (layer, f, 0)),
            pl.BlockSpec((tm, d), lambda i, f: (i, 0)),
            pl.BlockSpec((1, d), lambda i, f: (0, 0)),
        ],
        out_specs=[pl.BlockSpec((tm, d), lambda i, f: (i, 0)), pl.BlockSpec((tm, d), lambda i, f: (i, 0))],
        out_shape=[jax.ShapeDtypeStruct((m, d), _F32), jax.ShapeDtypeStruct((m, d), y_dtype)],
        compiler_params=_params(("parallel", "arbitrary")),
        name="mlp_relu2",
    )(u, w_up_stack, w_down_stack, h, g.reshape(1, d))


def _diff_attn_kernel(nfull_ref, nvis_ref, slopes_ref,
                      q_ref, k_ref, v_ref, pq_ref, pk_ref, lq1_ref, lk1_ref, lq2_ref, lk2_ref, g_ref,
                      o_ref, m_s, acc_s, s_s, *, lambda_init, tq, tk, hp):
    b, hg, i = pl.program_id(0), pl.program_id(1), pl.program_id(2)
    nq = pl.num_programs(2)
    n_full = nfull_ref[b * nq + i]
    n_vis = nvis_ref[b * nq + i]
    n_lane_tiles = tq // LANES
    heads = [slice(hh * HEAD_COLS, (hh + 1) * HEAD_COLS) for hh in range(hp)]

    lane = lax.broadcasted_iota(jnp.int32, (tq, HEAD_COLS), 1)
    zero = jnp.zeros((tq, HEAD_COLS), _BF16)
    q_maps = []
    for cols in heads:
        q = q_ref[:, cols]
        q_maps += [jnp.where(lane < DIFF_QK_DIM, q, zero), jnp.where(lane >= DIFF_QK_DIM, q, zero)]
    slope2 = [slopes_ref[hg * hp + hh] * LOG2E for hh in range(hp)]
    pq = pq_ref[0]
    pq0 = pq[:, 0:1]

    m_s[...] = jnp.full(m_s.shape, NEG_INF, _F32)
    acc_s[...] = jnp.zeros(acc_s.shape, _F32)
    ones_rows = jnp.ones((ONES_ROWS, tk), _BF16)

    n_chains = 2 * hp

    def softmax_pv(ci, j):
        v = v_ref[pl.ds(pl.multiple_of(j * tk, tk), tk), heads[ci // 2]]
        v_t = jnp.concatenate([v.T, ones_rows], axis=0)
        s = s_s[ci]
        m_prev = m_s[ci]
        m_new = jnp.maximum(m_prev, jnp.max(s, axis=0, keepdims=True))
        alpha = jnp.exp2(m_prev - m_new)
        p = jnp.exp2(s - m_new).astype(_BF16)
        acc_s[ci] = alpha * acc_s[ci] + jnp.dot(v_t, p, preferred_element_type=_F32)
        m_s[ci] = m_new

    def step(j, masked, first):
        start = pl.multiple_of(j * tk, tk)
        pk = pk_ref[pl.ds(start, tk), :]
        rel = pk - pq0
        bias = [jnp.concatenate([slope2[hh] * rel] * n_lane_tiles, axis=1) for hh in range(hp)]
        if masked:
            penalty = jnp.concatenate([jnp.where(pq_c >= pk, 0.0, NEG_INF) for pq_c in _lane_tiles(pq)], axis=1)
            bias = [b_h + penalty for b_h in bias]

        def scores(ci):
            k = k_ref[pl.ds(start, tk), heads[ci // 2]]
            s_s[ci] = lax.dot_general(k, q_maps[ci], _NT, preferred_element_type=_F32) + bias[ci // 2]

        scores(0)
        if not first:
            softmax_pv(n_chains - 1, j - 1)
        for ci in range(n_chains - 1):
            scores(ci + 1)
            softmax_pv(ci, j)

    @pl.when(n_full > 0)
    def _():
        step(0, masked=False, first=True)

    @pl.when(n_full == 0)
    def _():
        step(0, masked=True, first=True)

    def full_body(j, c):
        step(j, masked=False, first=False)
        return c

    def masked_body(j, c):
        step(j, masked=True, first=False)
        return c

    lax.fori_loop(1, n_full, full_body, 0)
    lax.fori_loop(jnp.maximum(n_full, 1), n_vis, masked_body, 0)
    softmax_pv(n_chains - 1, n_vis - 1)

    lam = (jnp.exp(jnp.sum(lq1_ref[...] * lk1_ref[...], axis=-1, keepdims=True))
           - jnp.exp(jnp.sum(lq2_ref[...] * lk2_ref[...], axis=-1, keepdims=True)) + lambda_init)
    gain = _lane_tile_repeat(g_ref[...] * (1.0 - lambda_init), tq)
    dv = DIFF_V_DIM
    for hh, cols in enumerate(heads):
        acc1, acc2 = acc_s[2 * hh], acc_s[2 * hh + 1]
        o_t = acc1[:dv] / acc1[dv:dv + 1] - lam * (acc2[:dv] / acc2[dv:dv + 1])
        o_ref[cols, :] = (_rms_over_rows(o_t) * gain).astype(o_ref.dtype)


def _diff_attention(qkv, pos_row, pos_rep, n_full, n_vis, lam_params, g, lambda_init, batch, seq):
    tq, tk, hp = DIFF_TQ, DIFF_TK, DIFF_HEADS_PER_STEP
    nq = seq // tq
    slopes = jnp.asarray(np.array([2.0 ** (-8.0 * (i + 1) / DIFF_HEADS) for i in range(DIFF_HEADS)], np.float32))
    small = lambda b, h, i, *_: (0, 0)
    grid_spec = pltpu.PrefetchScalarGridSpec(
        num_scalar_prefetch=3,
        grid=(batch, DIFF_HEADS // hp, nq),
        in_specs=[
            pl.BlockSpec((tq, hp * HEAD_COLS), lambda b, h, i, *_: (b * nq + i, DIFF_Q_BLK // hp + h)),
            pl.BlockSpec((seq, hp * HEAD_COLS), lambda b, h, i, *_: (b, DIFF_K_BLK // hp + h)),
            pl.BlockSpec((seq, hp * HEAD_COLS), lambda b, h, i, *_: (b, DIFF_V_BLK // hp + h)),
            pl.BlockSpec((1, 1, tq), lambda b, h, i, *_: (b * nq + i, 0, 0)),
            pl.BlockSpec((seq, LANES), lambda b, h, i, *_: (b, 0)),
            pl.BlockSpec((1, DIFF_QK_DIM), small), pl.BlockSpec((1, DIFF_QK_DIM), small),
            pl.BlockSpec((1, DIFF_QK_DIM), small), pl.BlockSpec((1, DIFF_QK_DIM), small),
            pl.BlockSpec((DIFF_V_DIM, LANES), small),
        ],
        out_specs=pl.BlockSpec((None, hp * HEAD_COLS, tq), lambda b, h, i, *_: (b, h, i)),
        scratch_shapes=[pltpu.VMEM((2 * hp, 1, tq), _F32),
                        pltpu.VMEM((2 * hp, DIFF_V_DIM + ONES_ROWS, tq), _F32), pltpu.VMEM((2 * hp, tk, tq), _F32)],
    )
    return pl.pallas_call(
        functools.partial(_diff_attn_kernel, lambda_init=lambda_init, tq=tq, tk=tk, hp=hp),
        grid_spec=grid_spec,
        out_shape=jax.ShapeDtypeStruct((batch, DIFF_HEADS * DIFF_V_DIM, seq), _BF16),
        compiler_params=_params(("parallel", "parallel", "arbitrary")),
        name="diff_attention",
    )(n_full, n_vis, slopes, qkv, qkv, qkv, pos_row.reshape(batch * nq, 1, tq), pos_rep,
      *[p.reshape(1, DIFF_QK_DIM) for p in lam_params], _lane_replicated_column(g))


def _sb_attn_kernel(nfull_ref, nvis_ref, q_ref, k_ref, v_ref, pq_ref, pk_ref, ntri_ref, g_ref,
                    o_ref, c_s, w_s, acc_s, z_s, e_s, *, tq, tk, hp):
    b, i = pl.program_id(0), pl.program_id(2)
    nq = pl.num_programs(2)
    n_full = nfull_ref[b * nq + i]
    n_vis = nvis_ref[b * nq + i]
    heads = [slice(hh * HEAD_COLS, (hh + 1) * HEAD_COLS) for hh in range(hp)]
    qs = [q_ref[:, cols] for cols in heads]
    pq = pq_ref[0]
    ntri = ntri_ref[...]

    c_s[...] = jnp.zeros(c_s.shape, _F32)
    acc_s[...] = jnp.zeros(acc_s.shape, _F32)

    def step(blocks):
        starts, keeps, penalties = [], [], []
        for j, masked in blocks:
            start = pl.multiple_of(j * tk, tk)
            starts.append(start)
            if masked:
                pk = pk_ref[pl.ds(start, tk), :]
                strict = [pq_c > pk for pq_c in _lane_tiles(pq)]
                keeps.append(jnp.concatenate([jnp.where(m, 1.0, 0.0) for m in strict], axis=1))
                penalties.append(jnp.concatenate([jnp.where(m, 0.0, NEG_INF) for m in strict], axis=1))
            else:
                keeps.append(None)
                penalties.append(None)

        def logits(c):
            blk, hh = divmod(c, hp)
            k = k_ref[pl.ds(starts[blk], tk), heads[hh]]
            z_s[c] = lax.dot_general(k, qs[hh], _NT, preferred_element_type=_F32)

        def log_weights(c):
            blk, hh = divmod(c, hp)
            z = z_s[c]
            softplus = jnp.maximum(z, 0.0) + jnp.log2(1.0 + jnp.exp2(-jnp.abs(z)))
            log_sig = z - softplus
            if keeps[blk] is not None:
                softplus = softplus * keeps[blk]
            terms = softplus.astype(_BF16)
            later = jnp.dot(ntri, terms, preferred_element_type=_F32)
            e = log_sig + later
            if penalties[blk] is not None:
                e = e + penalties[blk]
            e_s[c] = e
            carry = c_s[hh]
            w_s[c] = jnp.exp2(carry)
            c_s[hh] = carry + (later[0:1, :] - terms[0:1, :].astype(_F32))

        def weighted_values(c):
            blk, hh = divmod(c, hp)
            v = v_ref[pl.ds(starts[blk], tk), heads[hh]]
            a = jnp.exp2(e_s[c]).astype(_BF16)
            acc_s[hh] += w_s[c] * lax.dot_general(v, a, _TN, preferred_element_type=_F32)

        _software_pipeline((logits, log_weights, weighted_values), hp * len(blocks))

    n_masked = n_vis - n_full

    def masked_body(t, c):
        step([(n_vis - 1 - t, True)])
        return c

    lax.fori_loop(0, n_masked - 1, masked_body, 0)
    fuse = jnp.logical_and(n_masked > 0, n_full > 0)

    @pl.when(fuse)
    def _():
        step([(n_full, True), (n_full - 1, False)])

    @pl.when(jnp.logical_and(n_masked > 0, n_full == 0))
    def _():
        step([(n_full, True)])

    def any_weight_left():
        return jnp.max(c_s[...]) > DEAD_CARRY

    def full_cond(state):
        t, go = state
        return jnp.logical_and(t < n_full, go)

    def full_body(state):
        t, _ = state
        step([(n_full - 1 - t, False)])
        return t + 1, any_weight_left()

    lax.while_loop(full_cond, full_body, (fuse.astype(jnp.int32), any_weight_left()))
    gain = _lane_tile_repeat(g_ref[...], tq)
    for hh, cols in enumerate(heads):
        o_ref[cols, :] = (_rms_over_rows(acc_s[hh]) * gain).astype(o_ref.dtype)


def _sb_attention(qkv, pos_row, pos_rep, n_full, n_vis, g, batch, seq):
    tq, tk, hp = SB_TQ, SB_TK, SB_HEADS_PER_STEP
    nq = seq // tq
    idx = np.arange(tk)
    ntri = jnp.asarray(-(idx[None, :] > idx[:, None]).astype(np.float32), _BF16)
    grid_spec = pltpu.PrefetchScalarGridSpec(
        num_scalar_prefetch=2,
        grid=(batch, SB_HEADS // hp, nq),
        in_specs=[
            pl.BlockSpec((tq, hp * HEAD_COLS), lambda b, h, i, *_: (b * nq + i, SB_Q_BLK // hp + h)),
            pl.BlockSpec((seq, hp * HEAD_COLS), lambda b, h, i, *_: (b, SB_K_BLK // hp + h)),
            pl.BlockSpec((seq, hp * HEAD_COLS), lambda b, h, i, *_: (b, SB_V_BLK // hp + h)),
            pl.BlockSpec((1, 1, tq), lambda b, h, i, *_: (b * nq + i, 0, 0)),
            pl.BlockSpec((seq, LANES), lambda b, h, i, *_: (b, 0)),
            pl.BlockSpec((tk, tk), lambda b, h, i, *_: (0, 0)),
            pl.BlockSpec((SB_DIM, LANES), lambda b, h, i, *_: (0, 0)),
        ],
        out_specs=pl.BlockSpec((None, hp * HEAD_COLS, tq), lambda b, h, i, *_: (b, h, i)),
        scratch_shapes=[pltpu.VMEM((hp, 1, tq), _F32), pltpu.VMEM((2 * hp, 1, tq), _F32),
                        pltpu.VMEM((hp, SB_DIM, tq), _F32),
                        pltpu.VMEM((2 * hp, tk, tq), _F32), pltpu.VMEM((2 * hp, tk, tq), _F32)],
    )
    return pl.pallas_call(
        functools.partial(_sb_attn_kernel, tq=tq, tk=tk, hp=hp),
        grid_spec=grid_spec,
        out_shape=jax.ShapeDtypeStruct((batch, SB_HEADS * SB_DIM, seq), _BF16),
        compiler_params=_params(("parallel", "parallel", "arbitrary")),
        name="stick_breaking_attention",
    )(n_full, n_vis, qkv, qkv, qkv, pos_row.reshape(batch * nq, 1, tq), pos_rep, ntri, _lane_replicated_column(g))


def _xattn_block_kernel(u_ref, wq_ref, k_ref, v_ref, wo_ref, h_ref, g_ref, h_out_ref, u_out_ref):
    q = jnp.dot(u_ref[...], wq_ref[...], preferred_element_type=_F32) * (X_DIM ** -0.5 * LOG2E)
    q = q.astype(_BF16)
    heads_out = []
    for hd in range(X_HEADS):
        cols = slice(hd * X_DIM, (hd + 1) * X_DIM)
        s = lax.dot_general(q[:, cols], k_ref[:, cols], _NT, preferred_element_type=_F32)
        p = jnp.exp2(s - jnp.max(s, axis=-1, keepdims=True))
        o = jnp.dot(p.astype(_BF16), v_ref[:, cols], preferred_element_type=_F32)
        heads_out.append((o / jnp.sum(p, axis=-1, keepdims=True)).astype(_BF16))
    o_all = jnp.concatenate(heads_out, axis=1)
    acc = h_ref[...] + jnp.dot(o_all, wo_ref[...], preferred_element_type=_F32)
    h_out_ref[...] = acc
    u_out_ref[...] = _rms(acc, g_ref[...]).astype(u_out_ref.dtype)


def _xattn_block(u, wq_stack, kv, wo_stack, layer, h, g, batch, seq, tm):
    m, d = h.shape
    tiles = seq // tm
    resident = functools.partial(pl.BlockSpec, (None, d, d), lambda i: (layer, 0, 0), pipeline_mode=pl.Buffered(1))
    return pl.pallas_call(
        _xattn_block_kernel,
        grid=(m // tm,),
        in_specs=[
            pl.BlockSpec((tm, d), lambda i: (i, 0)),
            resident(),
            pl.BlockSpec((N_MEM, d), lambda i: (i // tiles, 0)),
            pl.BlockSpec((N_MEM, d), lambda i: (i // tiles, 1)),
            resident(),
            pl.BlockSpec((tm, d), lambda i: (i, 0)),
            pl.BlockSpec((1, d), lambda i: (0, 0)),
        ],
        out_specs=[pl.BlockSpec((tm, d), lambda i: (i, 0)), pl.BlockSpec((tm, d), lambda i: (i, 0))],
        out_shape=[jax.ShapeDtypeStruct((m, d), _F32), jax.ShapeDtypeStruct((m, d), _BF16)],
        compiler_params=_params(("parallel",)),
        name="xattn_block",
    )(u, wq_stack, kv, kv, wo_stack, h, g.reshape(1, d))


def _visible_block_counts(positions, tq, tk, strict):
    batch, seq = positions.shape
    qmin = positions.reshape(batch, seq // tq, tq).min(-1)[:, :, None]
    qmax = positions.reshape(batch, seq // tq, tq).max(-1)[:, :, None]
    kmin = positions.reshape(batch, seq // tk, tk).min(-1)[:, None, :]
    kmax = positions.reshape(batch, seq // tk, tk).max(-1)[:, None, :]
    full = (kmax < qmin) if strict else (kmax <= qmin)
    some = (kmin < qmax) if strict else (kmin <= qmax)
    n_full = jnp.sum(full, axis=-1).astype(jnp.int32).reshape(-1)
    n_vis = jnp.sum(some, axis=-1).astype(jnp.int32).reshape(-1)
    return n_full, jnp.maximum(n_vis, n_full)


def _in_proj_col_scale():
    scale = np.ones((1, IN_COLS), np.float32)
    scale[:, DIFF_Q_BLK * HEAD_COLS:(DIFF_Q_BLK + DIFF_HEADS) * HEAD_COLS] = DIFF_QK_DIM ** -0.5 * LOG2E
    scale[:, SB_Q_BLK * HEAD_COLS:(SB_Q_BLK + SB_HEADS) * HEAD_COLS] = SB_DIM ** -0.5 * LOG2E
    return jnp.asarray(scale)


def kernel(x, mem, positions, norm_mix, w_in, lam_q1, lam_k1, lam_q2, lam_k2, subln_diff, subln_sb, w_out,
           norm_x, norm_mem, wq_x, wkv_x, wo_x, norm_mlp, w_up, w_down, norm_final):
    batch, seq, d = x.shape
    depth = w_in.shape[0]
    tokens = batch * seq
    assert d == D_MODEL and w_in.shape[2] == IN_COLS
    assert seq % DIFF_TQ == 0 and seq % DIFF_TK == 0 and seq % SB_TQ == 0 and seq % SB_TK == 0

    pos_f = positions.astype(_F32)
    pos_row = pos_f.reshape(batch, seq)
    pos_rep = jnp.broadcast_to(pos_f.reshape(tokens, 1), (tokens, LANES))
    diff_counts = _visible_block_counts(positions, DIFF_TQ, DIFF_TK, strict=False)
    sb_counts = _visible_block_counts(positions, SB_TQ, SB_TK, strict=True)
    in_scale = _in_proj_col_scale()

    w_out_bf, wq_x_bf, wo_x_bf, w_up_bf, w_down_bf = (w.astype(_BF16) for w in (w_out, wq_x, wo_x, w_up, w_down))

    h = x.reshape(tokens, d)
    mem2 = mem.reshape(batch * N_MEM, d)
    u = _rmsnorm_bf16(h, norm_mix[0], tm=512)
    out = None
    for l in range(depth):
        lambda_init = 0.8 - 0.6 * math.exp(-0.3 * l)
        qkv = _proj(u, w_in, l, in_scale, tm=1024, tn=1024, name="in_proj")
        o_diff = _diff_attention(qkv, pos_row, pos_rep, *diff_counts,
                                 (lam_q1[l], lam_k1[l], lam_q2[l], lam_k2[l]), subln_diff[l], lambda_init,
                                 batch, seq)
        o_sb = _sb_attention(qkv, pos_row, pos_rep, *sb_counts, subln_sb[l], batch, seq)
        h, ux = _proj_res_norm([o_diff, o_sb], w_out_bf, l, h, norm_x[l], tm=512, name="out_proj",
                               feature_major=True)

        mem_n = _rmsnorm_bf16(mem2, norm_mem[l], tm=256)
        kv = _proj(mem_n, wkv_x, l, None, tm=1024, tn=1024, name="xattn_kv_proj")
        h, um = _xattn_block(ux, wq_x_bf, kv, wo_x_bf, l, h, norm_mlp[l], batch, seq, tm=512)

        last = l == depth - 1
        g_next = norm_final if last else norm_mix[l + 1]
        h, y = _mlp(um, w_up_bf, w_down_bf, l, h, g_next, _F32 if last else _BF16, tm=512, tf=1024)
        if last:
            out = y
        else:
            u = y
    return out.reshape(batch, seq, d)
```

```python
import functools
import math

import jax
import jax.numpy as jnp
import numpy as np
from jax import lax
from jax.experimental import pallas as pl
from jax.experimental.pallas import tpu as pltpu

D_MODEL = 2048
N_MEM = 256
DIFF_V_DIM = 128
DIFF_QK_DIM = 64
DIFF_HEADS = 8
SB_DIM = 128
SB_HEADS = 8
X_HEADS = 4
X_DIM = D_MODEL // X_HEADS
D_FF = 4 * D_MODEL
EPS = 1e-6
NEG_INF = -1e30
DEAD_CARRY = -160.0
LOG2E = math.log2(math.e)

LANES = 128
ONES_ROWS = 16
HEAD_COLS = 128
DIFF_Q_BLK = 0
DIFF_K_BLK = DIFF_HEADS
DIFF_V_BLK = 2 * DIFF_HEADS
SB_Q_BLK = 3 * DIFF_HEADS
SB_K_BLK = SB_Q_BLK + SB_HEADS
SB_V_BLK = SB_K_BLK + SB_HEADS
IN_COLS = (3 * DIFF_HEADS + 3 * SB_HEADS) * HEAD_COLS

DIFF_TQ, DIFF_TK = 512, 512
SB_TQ, SB_TK = 256, 256
DIFF_HEADS_PER_STEP = 4
SB_HEADS_PER_STEP = 8
VMEM_LIMIT = 56 * 1024 * 1024

_BF16 = jnp.bfloat16
_F32 = jnp.float32
_NT = (((1,), (1,)), ((), ()))
_TN = (((0,), (0,)), ((), ()))


def _params(semantics):
    return pltpu.CompilerParams(dimension_semantics=semantics, vmem_limit_bytes=VMEM_LIMIT)


def _rms(x, g):
    return x * lax.rsqrt(jnp.mean(x * x, axis=-1, keepdims=True) + EPS) * g


def _lane_tiles(x):
    return [x[:, c * LANES:(c + 1) * LANES] for c in range(x.shape[1] // LANES)]


def _lane_tile_repeat(tile, width):
    return jnp.concatenate([tile] * (width // LANES), axis=1)


def _lane_replicated_column(vec):
    return jnp.broadcast_to(vec.astype(_F32).reshape(-1, 1), (vec.shape[0], LANES))


def _rms_over_rows(x_t):
    return x_t * lax.rsqrt(jnp.mean(x_t * x_t, axis=0, keepdims=True) + EPS)


def _software_pipeline(stages, n_chains):
    for t in range(n_chains + len(stages) - 1):
        for s, stage in enumerate(stages):
            if 0 <= t - s < n_chains:
                stage(t - s)


def _rmsnorm_kernel(x_ref, g_ref, o_ref):
    o_ref[...] = _rms(x_ref[...], g_ref[...]).astype(o_ref.dtype)


def _rmsnorm_bf16(x, g, tm):
    m, d = x.shape
    return pl.pallas_call(
        _rmsnorm_kernel,
        grid=(m // tm,),
        in_specs=[pl.BlockSpec((tm, d), lambda i: (i, 0)), pl.BlockSpec((1, d), lambda i: (0, 0))],
        out_specs=pl.BlockSpec((tm, d), lambda i: (i, 0)),
        out_shape=jax.ShapeDtypeStruct((m, d), _BF16),
        compiler_params=_params(("parallel",)),
        name="rmsnorm_bf16",
    )(x, g.reshape(1, d))


def _proj_kernel(x_ref, w_ref, *rest):
    o_ref = rest[-1]
    acc = jnp.dot(x_ref[...], w_ref[...].astype(_BF16), preferred_element_type=_F32)
    if len(rest) == 2:
        acc = acc * rest[0][...]
    o_ref[...] = acc.astype(o_ref.dtype)


def _proj(x, w_stack, layer, col_scale, tm, tn, name):
    m, k = x.shape
    n = w_stack.shape[2]
    in_specs = [pl.BlockSpec((tm, k), lambda j, i: (i, 0)), pl.BlockSpec((None, k, tn), lambda j, i: (layer, 0, j))]
    args = [x, w_stack]
    if col_scale is not None:
        in_specs.append(pl.BlockSpec((1, tn), lambda j, i: (0, j)))
        args.append(col_scale)
    return pl.pallas_call(
        _proj_kernel,
        grid=(n // tn, m // tm),
        in_specs=in_specs,
        out_specs=pl.BlockSpec((tm, tn), lambda j, i: (i, j)),
        out_shape=jax.ShapeDtypeStruct((m, n), _BF16),
        compiler_params=_params(("parallel", "arbitrary")),
        name=name,
    )(*args)


def _proj_res_norm_kernel(*refs, n_in, feature_major):
    xs = refs[:n_in]
    ws = refs[n_in:2 * n_in]
    h_ref, g_ref, h_out_ref, u_out_ref = refs[2 * n_in:]
    acc = h_ref[...]
    for x_ref, w_ref in zip(xs, ws):
        dims = _TN if feature_major else (((1,), (0,)), ((), ()))
        acc = acc + lax.dot_general(x_ref[...], w_ref[...], dims, preferred_element_type=_F32)
    h_out_ref[...] = acc
    u_out_ref[...] = _rms(acc, g_ref[...]).astype(u_out_ref.dtype)


def _proj_res_norm(xs, w_stack, layer, h, g, tm, name, feature_major=False):
    m, d = h.shape
    n_in = len(xs)
    if feature_major:
        kx, seq = xs[0].shape[1:]
        tiles = seq // tm
        in_specs = [pl.BlockSpec((None, kx, tm), lambda i: (i // tiles, 0, i % tiles)) for _ in xs]
    else:
        kx = xs[0].shape[1]
        in_specs = [pl.BlockSpec((tm, kx), lambda i: (i, 0)) for _ in xs]
    in_specs += [pl.BlockSpec((None, kx, d), lambda i, r=r: (layer, r, 0)) for r in range(n_in)]
    in_specs += [pl.BlockSpec((tm, d), lambda i: (i, 0)), pl.BlockSpec((1, d), lambda i: (0, 0))]
    return pl.pallas_call(
        functools.partial(_proj_res_norm_kernel, n_in=n_in, feature_major=feature_major),
        grid=(m // tm,),
        in_specs=in_specs,
        out_specs=[pl.BlockSpec((tm, d), lambda i: (i, 0)), pl.BlockSpec((tm, d), lambda i: (i, 0))],
        out_shape=[jax.ShapeDtypeStruct((m, d), _F32), jax.ShapeDtypeStruct((m, d), _BF16)],
        compiler_params=_params(("parallel",)),
        name=name,
    )(*xs, *([w_stack] * n_in), h, g.reshape(1, d))


def _mlp_kernel(u_ref, wu_ref, wd_ref, h_ref, g_ref, h_out_ref, y_out_ref):
    f = pl.program_id(1)

    @pl.when(f == 0)
    def _():
        h_out_ref[...] = h_ref[...]

    up = jnp.dot(u_ref[...], wu_ref[...], preferred_element_type=_F32)
    a = jnp.square(jnp.maximum(up, 0.0)).astype(_BF16)
    h_out_ref[...] += jnp.dot(a, wd_ref[...], preferred_element_type=_F32)

    @pl.when(f == pl.num_programs(1) - 1)
    def _():
        y_out_ref[...] = _rms(h_out_ref[...], g_ref[...]).astype(y_out_ref.dtype)


def _mlp(u, w_up_stack, w_down_stack, layer, h, g, y_dtype, tm, tf):
    m, d = h.shape
    dff = w_up_stack.shape[2]
    return pl.pallas_call(
        _mlp_kernel,
        grid=(m // tm, dff // tf),
        in_specs=[
            pl.BlockSpec((tm, d), lambda i, f: (i, 0)),
            pl.BlockSpec((None, d, tf), lambda i, f: (layer, 0, f)),
            pl.BlockSpec((None, tf, d), lambda i, f: (layer, f, 0)),
            pl.BlockSpec((tm, d), lambda i, f: (i, 0)),
            pl.BlockSpec((1, d), lambda i, f: (0, 0)),
        ],
        out_specs=[pl.BlockSpec((tm, d), lambda i, f: (i, 0)), pl.BlockSpec((tm, d), lambda i, f: (i, 0))],
        out_shape=[jax.ShapeDtypeStruct((m, d), _F32), jax.ShapeDtypeStruct((m, d), y_dtype)],
        compiler_params=_params(("parallel", "arbitrary")),
        name="mlp_relu2",
    )(u, w_up_stack, w_down_stack, h, g.reshape(1, d))


def _diff_attn_kernel(nfull_ref, nvis_ref, slopes_ref,
                      q_ref, k_ref, v_ref, pq_ref, pk_ref, lq1_ref, lk1_ref, lq2_ref, lk2_ref, g_ref,
                      *rest, lambda_init, tq, tk, hp, n_cast):
    cast_in, (o_ref, *cast_out), (m_s, acc_s, s_s) = rest[:n_cast], rest[n_cast:2 * n_cast + 1], rest[2 * n_cast + 1:]
    for w_ref, w_bf_ref in zip(cast_in, cast_out):
        w_bf_ref[...] = w_ref[...].astype(w_bf_ref.dtype)

    b, hg, i = pl.program_id(0), pl.program_id(1), pl.program_id(2)
    nq = pl.num_programs(2)
    n_full = nfull_ref[b * nq + i]
    n_vis = nvis_ref[b * nq + i]
    n_lane_tiles = tq // LANES
    heads = [slice(hh * HEAD_COLS, (hh + 1) * HEAD_COLS) for hh in range(hp)]

    lane = lax.broadcasted_iota(jnp.int32, (tq, HEAD_COLS), 1)
    zero = jnp.zeros((tq, HEAD_COLS), _BF16)
    q_maps = []
    for cols in heads:
        q = q_ref[:, cols]
        q_maps += [jnp.where(lane < DIFF_QK_DIM, q, zero), jnp.where(lane >= DIFF_QK_DIM, q, zero)]
    slope2 = [slopes_ref[hg * hp + hh] * LOG2E for hh in range(hp)]
    pq = pq_ref[0]
    pq0 = pq[:, 0:1]

    m_s[...] = jnp.full(m_s.shape, NEG_INF, _F32)
    acc_s[...] = jnp.zeros(acc_s.shape, _F32)
    ones_rows = jnp.ones((ONES_ROWS, tk), _BF16)

    n_chains = 2 * hp

    def softmax_pv(ci, j):
        v = v_ref[pl.ds(pl.multiple_of(j * tk, tk), tk), heads[ci // 2]]
        v_t = jnp.concatenate([v.T, ones_rows], axis=0)
        s = s_s[ci]
        m_prev = m_s[ci]
        m_new = jnp.maximum(m_prev, jnp.max(s, axis=0, keepdims=True))
        alpha = jnp.exp2(m_prev - m_new)
        p = jnp.exp2(s - m_new).astype(_BF16)
        acc_s[ci] = alpha * acc_s[ci] + jnp.dot(v_t, p, preferred_element_type=_F32)
        m_s[ci] = m_new

    def step(j, masked, first):
        start = pl.multiple_of(j * tk, tk)
        pk = pk_ref[pl.ds(start, tk), :]
        rel = pk - pq0
        bias = [jnp.concatenate([slope2[hh] * rel] * n_lane_tiles, axis=1) for hh in range(hp)]
        if masked:
            penalty = jnp.concatenate([jnp.where(pq_c >= pk, 0.0, NEG_INF) for pq_c in _lane_tiles(pq)], axis=1)
            bias = [b_h + penalty for b_h in bias]

        def scores(ci):
            k = k_ref[pl.ds(start, tk), heads[ci // 2]]
            s_s[ci] = lax.dot_general(k, q_maps[ci], _NT, preferred_element_type=_F32) + bias[ci // 2]

        scores(0)
        if not first:
            softmax_pv(n_chains - 1, j - 1)
        for ci in range(n_chains - 1):
            scores(ci + 1)
            softmax_pv(ci, j)

    @pl.when(n_full > 0)
    def _():
        step(0, masked=False, first=True)

    @pl.when(n_full == 0)
    def _():
        step(0, masked=True, first=True)

    def full_body(j, c):
        step(j, masked=False, first=False)
        return c

    def masked_body(j, c):
        step(j, masked=True, first=False)
        return c

    lax.fori_loop(1, n_full, full_body, 0)
    lax.fori_loop(jnp.maximum(n_full, 1), n_vis, masked_body, 0)
    softmax_pv(n_chains - 1, n_vis - 1)

    lam = (jnp.exp(jnp.sum(lq1_ref[...] * lk1_ref[...], axis=-1, keepdims=True))
           - jnp.exp(jnp.sum(lq2_ref[...] * lk2_ref[...], axis=-1, keepdims=True)) + lambda_init)
    gain = _lane_tile_repeat(g_ref[...] * (1.0 - lambda_init), tq)
    dv = DIFF_V_DIM
    for hh, cols in enumerate(heads):
        acc1, acc2 = acc_s[2 * hh], acc_s[2 * hh + 1]
        o_t = acc1[:dv] / acc1[dv:dv + 1] - lam * (acc2[:dv] / acc2[dv:dv + 1])
        o_ref[cols, :] = (_rms_over_rows(o_t) * gain).astype(o_ref.dtype)


def _diff_attention(qkv, pos_row, pos_rep, n_full, n_vis, lam_params, g, lambda_init, batch, seq, layer, to_bf16):
    tq, tk, hp = DIFF_TQ, DIFF_TK, DIFF_HEADS_PER_STEP
    nq = seq // tq
    n_hg = DIFF_HEADS // hp
    n_steps = batch * n_hg * nq
    slopes = jnp.asarray(np.array([2.0 ** (-8.0 * (i + 1) / DIFF_HEADS) for i in range(DIFF_HEADS)], np.float32))
    small = lambda b, h, i, *_: (0, 0)
    step_id = lambda b, h, i: (b * n_hg + h) * nq + i
    cast_in_specs, cast_out_specs, cast_out_shapes = [], [], []
    for w in to_bf16:
        rows, cols = w.shape[1] // n_steps, w.shape[2]
        assert rows * n_steps == w.shape[1] and rows % 16 == 0
        cast_in_specs.append(pl.BlockSpec((None, rows, cols), lambda b, h, i, *_: (layer, step_id(b, h, i), 0)))
        cast_out_specs.append(pl.BlockSpec((rows, cols), lambda b, h, i, *_: (step_id(b, h, i), 0)))
        cast_out_shapes.append(jax.ShapeDtypeStruct(w.shape[1:], _BF16))
    grid_spec = pltpu.PrefetchScalarGridSpec(
        num_scalar_prefetch=3,
        grid=(batch, n_hg, nq),
        in_specs=[
            pl.BlockSpec((tq, hp * HEAD_COLS), lambda b, h, i, *_: (b * nq + i, DIFF_Q_BLK // hp + h)),
            pl.BlockSpec((seq, hp * HEAD_COLS), lambda b, h, i, *_: (b, DIFF_K_BLK // hp + h)),
            pl.BlockSpec((seq, hp * HEAD_COLS), lambda b, h, i, *_: (b, DIFF_V_BLK // hp + h)),
            pl.BlockSpec((1, 1, tq), lambda b, h, i, *_: (b * nq + i, 0, 0)),
            pl.BlockSpec((seq, LANES), lambda b, h, i, *_: (b, 0)),
            pl.BlockSpec((1, DIFF_QK_DIM), small), pl.BlockSpec((1, DIFF_QK_DIM), small),
            pl.BlockSpec((1, DIFF_QK_DIM), small), pl.BlockSpec((1, DIFF_QK_DIM), small),
            pl.BlockSpec((DIFF_V_DIM, LANES), small),
        ] + cast_in_specs,
        out_specs=[pl.BlockSpec((None, hp * HEAD_COLS, tq), lambda b, h, i, *_: (b, h, i))] + cast_out_specs,
        scratch_shapes=[pltpu.VMEM((2 * hp, 1, tq), _F32),
                        pltpu.VMEM((2 * hp, DIFF_V_DIM + ONES_ROWS, tq), _F32), pltpu.VMEM((2 * hp, tk, tq), _F32)],
    )
    outs = pl.pallas_call(
        functools.partial(_diff_attn_kernel, lambda_init=lambda_init, tq=tq, tk=tk, hp=hp, n_cast=len(to_bf16)),
        grid_spec=grid_spec,
        out_shape=[jax.ShapeDtypeStruct((batch, DIFF_HEADS * DIFF_V_DIM, seq), _BF16)] + cast_out_shapes,
        compiler_params=_params(("parallel", "parallel", "arbitrary")),
        name="diff_attention",
    )(n_full, n_vis, slopes, qkv, qkv, qkv, pos_row.reshape(batch * nq, 1, tq), pos_rep,
      *[p.reshape(1, DIFF_QK_DIM) for p in lam_params], _lane_replicated_column(g), *to_bf16)
    return outs[0], outs[1:]


def _sb_attn_kernel(nfull_ref, nvis_ref, q_ref, k_ref, v_ref, pq_ref, pk_ref, ntri_ref, g_ref,
                    o_ref, c_s, w_s, acc_s, z_s, e_s, *, tq, tk, hp):
    b, i = pl.program_id(0), pl.program_id(2)
    nq = pl.num_programs(2)
    n_full = nfull_ref[b * nq + i]
    n_vis = nvis_ref[b * nq + i]
    heads = [slice(hh * HEAD_COLS, (hh + 1) * HEAD_COLS) for hh in range(hp)]
    qs = [q_ref[:, cols] for cols in heads]
    pq = pq_ref[0]
    ntri = ntri_ref[...]

    c_s[...] = jnp.zeros(c_s.shape, _F32)
    acc_s[...] = jnp.zeros(acc_s.shape, _F32)

    def step(blocks):
        starts, keeps, penalties = [], [], []
        for j, masked in blocks:
            start = pl.multiple_of(j * tk, tk)
            starts.append(start)
            if masked:
                pk = pk_ref[pl.ds(start, tk), :]
                strict = [pq_c > pk for pq_c in _lane_tiles(pq)]
                keeps.append(jnp.concatenate([jnp.where(m, 1.0, 0.0) for m in strict], axis=1))
                penalties.append(jnp.concatenate([jnp.where(m, 0.0, NEG_INF) for m in strict], axis=1))
            else:
                keeps.append(None)
                penalties.append(None)

        def logits(c):
            blk, hh = divmod(c, hp)
            k = k_ref[pl.ds(starts[blk], tk), heads[hh]]
            z_s[c] = lax.dot_general(k, qs[hh], _NT, preferred_element_type=_F32)

        def log_weights(c):
            blk, hh = divmod(c, hp)
            z = z_s[c]
            softplus = jnp.maximum(z, 0.0) + jnp.log2(1.0 + jnp.exp2(-jnp.abs(z)))
            log_sig = z - softplus
            if keeps[blk] is not None:
                softplus = softplus * keeps[blk]
            terms = softplus.astype(_BF16)
            later = jnp.dot(ntri, terms, preferred_element_type=_F32)
            e = log_sig + later
            if penalties[blk] is not None:
                e = e + penalties[blk]
            e_s[c] = e
            carry = c_s[hh]
            w_s[c] = jnp.exp2(carry)
            c_s[hh] = carry + (later[0:1, :] - terms[0:1, :].astype(_F32))

        def weighted_values(c):
            blk, hh = divmod(c, hp)
            v = v_ref[pl.ds(starts[blk], tk), heads[hh]]
            a = jnp.exp2(e_s[c]).astype(_BF16)
            acc_s[hh] += w_s[c] * lax.dot_general(v, a, _TN, preferred_element_type=_F32)

        _software_pipeline((logits, log_weights, weighted_values), hp * len(blocks))

    n_masked = n_vis - n_full

    def masked_body(t, c):
        step([(n_vis - 1 - t, True)])
        return c

    lax.fori_loop(0, n_masked - 1, masked_body, 0)
    fuse = jnp.logical_and(n_masked > 0, n_full > 0)

    @pl.when(fuse)
    def _():
        step([(n_full, True), (n_full - 1, False)])

    @pl.when(jnp.logical_and(n_masked > 0, n_full == 0))
    def _():
        step([(n_full, True)])

    def any_weight_left():
        return jnp.max(c_s[...]) > DEAD_CARRY

    def full_cond(state):
        t, go = state
        return jnp.logical_and(t < n_full, go)

    def full_body(state):
        t, _ = state
        step([(n_full - 1 - t, False)])
        return t + 1, any_weight_left()

    lax.while_loop(full_cond, full_body, (fuse.astype(jnp.int32), any_weight_left()))
    gain = _lane_tile_repeat(g_ref[...], tq)
    for hh, cols in enumerate(heads):
        o_ref[cols, :] = (_rms_over_rows(acc_s[hh]) * gain).astype(o_ref.dtype)


def _sb_attention(qkv, pos_row, pos_rep, n_full, n_vis, g, batch, seq):
    tq, tk, hp = SB_TQ, SB_TK, SB_HEADS_PER_STEP
    nq = seq // tq
    idx = np.arange(tk)
    ntri = jnp.asarray(-(idx[None, :] > idx[:, None]).astype(np.float32), _BF16)
    grid_spec = pltpu.PrefetchScalarGridSpec(
        num_scalar_prefetch=2,
        grid=(batch, SB_HEADS // hp, nq),
        in_specs=[
            pl.BlockSpec((tq, hp * HEAD_COLS), lambda b, h, i, *_: (b * nq + i, SB_Q_BLK // hp + h)),
            pl.BlockSpec((seq, hp * HEAD_COLS), lambda b, h, i, *_: (b, SB_K_BLK // hp + h)),
            pl.BlockSpec((seq, hp * HEAD_COLS), lambda b, h, i, *_: (b, SB_V_BLK // hp + h)),
            pl.BlockSpec((1, 1, tq), lambda b, h, i, *_: (b * nq + i, 0, 0)),
            pl.BlockSpec((seq, LANES), lambda b, h, i, *_: (b, 0)),
            pl.BlockSpec((tk, tk), lambda b, h, i, *_: (0, 0)),
            pl.BlockSpec((SB_DIM, LANES), lambda b, h, i, *_: (0, 0)),
        ],
        out_specs=pl.BlockSpec((None, hp * HEAD_COLS, tq), lambda b, h, i, *_: (b, h, i)),
        scratch_shapes=[pltpu.VMEM((hp, 1, tq), _F32), pltpu.VMEM((2 * hp, 1, tq), _F32),
                        pltpu.VMEM((hp, SB_DIM, tq), _F32),
                        pltpu.VMEM((2 * hp, tk, tq), _F32), pltpu.VMEM((2 * hp, tk, tq), _F32)],
    )
    return pl.pallas_call(
        functools.partial(_sb_attn_kernel, tq=tq, tk=tk, hp=hp),
        grid_spec=grid_spec,
        out_shape=jax.ShapeDtypeStruct((batch, SB_HEADS * SB_DIM, seq), _BF16),
        compiler_params=_params(("parallel", "parallel", "arbitrary")),
        name="stick_breaking_attention",
    )(n_full, n_vis, qkv, qkv, qkv, pos_row.reshape(batch * nq, 1, tq), pos_rep, ntri, _lane_replicated_column(g))


def _xattn_block_kernel(u_ref, wq_ref, k_ref, v_ref, wo_ref, h_ref, g_ref, h_out_ref, u_out_ref):
    q = jnp.dot(u_ref[...], wq_ref[...], preferred_element_type=_F32) * (X_DIM ** -0.5 * LOG2E)
    q = q.astype(_BF16)
    heads_out = []
    for hd in range(X_HEADS):
        cols = slice(hd * X_DIM, (hd + 1) * X_DIM)
        s = lax.dot_general(q[:, cols], k_ref[:, cols], _NT, preferred_element_type=_F32)
        p = jnp.exp2(s - jnp.max(s, axis=-1, keepdims=True))
        o = jnp.dot(p.astype(_BF16), v_ref[:, cols], preferred_element_type=_F32)
        heads_out.append((o / jnp.sum(p, axis=-1, keepdims=True)).astype(_BF16))
    o_all = jnp.concatenate(heads_out, axis=1)
    acc = h_ref[...] + jnp.dot(o_all, wo_ref[...], preferred_element_type=_F32)
    h_out_ref[...] = acc
    u_out_ref[...] = _rms(acc, g_ref[...]).astype(u_out_ref.dtype)


def _xattn_block(u, wq_stack, kv, wo_stack, layer, h, g, batch, seq, tm):
    m, d = h.shape
    tiles = seq // tm
    resident = functools.partial(pl.BlockSpec, (None, d, d), lambda i: (layer, 0, 0), pipeline_mode=pl.Buffered(1))
    return pl.pallas_call(
        _xattn_block_kernel,
        grid=(m // tm,),
        in_specs=[
            pl.BlockSpec((tm, d), lambda i: (i, 0)),
            resident(),
            pl.BlockSpec((N_MEM, d), lambda i: (i // tiles, 0)),
            pl.BlockSpec((N_MEM, d), lambda i: (i // tiles, 1)),
            resident(),
            pl.BlockSpec((tm, d), lambda i: (i, 0)),
            pl.BlockSpec((1, d), lambda i: (0, 0)),
        ],
        out_specs=[pl.BlockSpec((tm, d), lambda i: (i, 0)), pl.BlockSpec((tm, d), lambda i: (i, 0))],
        out_shape=[jax.ShapeDtypeStruct((m, d), _F32), jax.ShapeDtypeStruct((m, d), _BF16)],
        compiler_params=_params(("parallel",)),
        name="xattn_block",
    )(u, wq_stack, kv, kv, wo_stack, h, g.reshape(1, d))


def _visible_block_counts(positions, tq, tk, strict):
    batch, seq = positions.shape
    qmin = positions.reshape(batch, seq // tq, tq).min(-1)[:, :, None]
    qmax = positions.reshape(batch, seq // tq, tq).max(-1)[:, :, None]
    kmin = positions.reshape(batch, seq // tk, tk).min(-1)[:, None, :]
    kmax = positions.reshape(batch, seq // tk, tk).max(-1)[:, None, :]
    full = (kmax < qmin) if strict else (kmax <= qmin)
    some = (kmin < qmax) if strict else (kmin <= qmax)
    n_full = jnp.sum(full, axis=-1).astype(jnp.int32).reshape(-1)
    n_vis = jnp.sum(some, axis=-1).astype(jnp.int32).reshape(-1)
    return n_full, jnp.maximum(n_vis, n_full)


def _in_proj_col_scale():
    scale = np.ones((1, IN_COLS), np.float32)
    scale[:, DIFF_Q_BLK * HEAD_COLS:(DIFF_Q_BLK + DIFF_HEADS) * HEAD_COLS] = DIFF_QK_DIM ** -0.5 * LOG2E
    scale[:, SB_Q_BLK * HEAD_COLS:(SB_Q_BLK + SB_HEADS) * HEAD_COLS] = SB_DIM ** -0.5 * LOG2E
    return jnp.asarray(scale)


def kernel(x, mem, positions, norm_mix, w_in, lam_q1, lam_k1, lam_q2, lam_k2, subln_diff, subln_sb, w_out,
           norm_x, norm_mem, wq_x, wkv_x, wo_x, norm_mlp, w_up, w_down, norm_final):
    batch, seq, d = x.shape
    depth = w_in.shape[0]
    tokens = batch * seq
    assert d == D_MODEL and w_in.shape[2] == IN_COLS
    assert seq % DIFF_TQ == 0 and seq % DIFF_TK == 0 and seq % SB_TQ == 0 and seq % SB_TK == 0

    pos_f = positions.astype(_F32)
    pos_row = pos_f.reshape(batch, seq)
    pos_rep = jnp.broadcast_to(pos_f.reshape(tokens, 1), (tokens, LANES))
    diff_counts = _visible_block_counts(positions, DIFF_TQ, DIFF_TK, strict=False)
    sb_counts = _visible_block_counts(positions, SB_TQ, SB_TK, strict=True)
    in_scale = _in_proj_col_scale()

    h = x.reshape(tokens, d)
    mem2 = mem.reshape(batch * N_MEM, d)
    u = _rmsnorm_bf16(h, norm_mix[0], tm=512)
    out = None
    for l in range(depth):
        lambda_init = 0.8 - 0.6 * math.exp(-0.3 * l)
        qkv = _proj(u, w_in, l, in_scale, tm=1024, tn=1024, name="in_proj")
        o_diff, layer_bf16 = _diff_attention(qkv, pos_row, pos_rep, *diff_counts,
                                             (lam_q1[l], lam_k1[l], lam_q2[l], lam_k2[l]), subln_diff[l],
                                             lambda_init, batch, seq, l, (w_out, wq_x, wo_x, w_up, w_down))
        w_out_bf, wq_x_bf, wo_x_bf, w_up_bf, w_down_bf = (w[None] for w in layer_bf16)
        o_sb = _sb_attention(qkv, pos_row, pos_rep, *sb_counts, subln_sb[l], batch, seq)
        h, ux = _proj_res_norm([o_diff, o_sb], w_out_bf, 0, h, norm_x[l], tm=512, name="out_proj",
                               feature_major=True)

        mem_n = _rmsnorm_bf16(mem2, norm_mem[l], tm=256)
        kv = _proj(mem_n, wkv_x, l, None, tm=1024, tn=1024, name="xattn_kv_proj")
        h, um = _xattn_block(ux, wq_x_bf, kv, wo_x_bf, 0, h, norm_mlp[l], batch, seq, tm=512)

        last = l == depth - 1
        g_next = norm_final if last else norm_mix[l + 1]
        h, y = _mlp(um, w_up_bf, w_down_bf, 0, h, g_next, _F32 if last else _BF16, tm=512, tf=1024)
        if last:
            out = y
        else:
            u = y
    return out.reshape(batch, seq, d)
```

```python
import functools
import math

import jax
import jax.numpy as jnp
import numpy as np
from jax import lax
from jax.experimental import pallas as pl
from jax.experimental.pallas import tpu as pltpu

D_MODEL = 2048
N_MEM = 256
DIFF_V_DIM = 128
DIFF_QK_DIM = 64
DIFF_HEADS = 8
SB_DIM = 128
SB_HEADS = 8
X_HEADS = 4
X_DIM = D_MODEL // X_HEADS
D_FF = 4 * D_MODEL
EPS = 1e-6
NEG_INF = -1e30
DEAD_CARRY = -160.0
LOG2E = math.log2(math.e)

LANES = 128
ONES_ROWS = 16
HEAD_COLS = 128
DIFF_Q_BLK = 0
DIFF_K_BLK = DIFF_HEADS
DIFF_V_BLK = 2 * DIFF_HEADS
SB_Q_BLK = 3 * DIFF_HEADS
SB_K_BLK = SB_Q_BLK + SB_HEADS
SB_V_BLK = SB_K_BLK + SB_HEADS
IN_COLS = (3 * DIFF_HEADS + 3 * SB_HEADS) * HEAD_COLS

DIFF_TQ, DIFF_TK = 512, 512
SB_TQ, SB_TK = 256, 256
MLP_TF = 1024
DIFF_HEADS_PER_STEP = 4
SB_HEADS_PER_STEP = 8
VMEM_LIMIT = 56 * 1024 * 1024

_BF16 = jnp.bfloat16
_F32 = jnp.float32
_NT = (((1,), (1,)), ((), ()))
_TN = (((0,), (0,)), ((), ()))


def _params(semantics):
    return pltpu.CompilerParams(dimension_semantics=semantics, vmem_limit_bytes=VMEM_LIMIT)


def _rms(x, g):
    return x * lax.rsqrt(jnp.mean(x * x, axis=-1, keepdims=True) + EPS) * g


def _lane_tiles(x):
    return [x[:, c * LANES:(c + 1) * LANES] for c in range(x.shape[1] // LANES)]


def _lane_tile_repeat(tile, width):
    return jnp.concatenate([tile] * (width // LANES), axis=1)


def _lane_replicated_column(vec):
    return jnp.broadcast_to(vec.astype(_F32).reshape(-1, 1), (vec.shape[0], LANES))


def _rms_over_rows(x_t):
    return x_t * lax.rsqrt(jnp.mean(x_t * x_t, axis=0, keepdims=True) + EPS)


def _software_pipeline(stages, n_chains):
    for t in range(n_chains + len(stages) - 1):
        for s, stage in enumerate(stages):
            if 0 <= t - s < n_chains:
                stage(t - s)


def _rmsnorm_kernel(x_ref, g_ref, o_ref):
    o_ref[...] = _rms(x_ref[...], g_ref[...]).astype(o_ref.dtype)


def _rmsnorm_bf16(x, g, tm):
    m, d = x.shape
    return pl.pallas_call(
        _rmsnorm_kernel,
        grid=(m // tm,),
        in_specs=[pl.BlockSpec((tm, d), lambda i: (i, 0)), pl.BlockSpec((1, d), lambda i: (0, 0))],
        out_specs=pl.BlockSpec((tm, d), lambda i: (i, 0)),
        out_shape=jax.ShapeDtypeStruct((m, d), _BF16),
        compiler_params=_params(("parallel",)),
        name="rmsnorm_bf16",
    )(x, g.reshape(1, d))


def _proj_kernel(x_ref, w_ref, *rest):
    o_ref = rest[-1]
    acc = jnp.dot(x_ref[...], w_ref[...].astype(_BF16), preferred_element_type=_F32)
    if len(rest) == 2:
        acc = acc * rest[0][...]
    o_ref[...] = acc.astype(o_ref.dtype)


def _proj(x, w_stack, layer, col_scale, tm, tn, name):
    m, k = x.shape
    n = w_stack.shape[2]
    in_specs = [pl.BlockSpec((tm, k), lambda j, i: (i, 0)), pl.BlockSpec((None, k, tn), lambda j, i: (layer, 0, j))]
    args = [x, w_stack]
    if col_scale is not None:
        in_specs.append(pl.BlockSpec((1, tn), lambda j, i: (0, j)))
        args.append(col_scale)
    return pl.pallas_call(
        _proj_kernel,
        grid=(n // tn, m // tm),
        in_specs=in_specs,
        out_specs=pl.BlockSpec((tm, tn), lambda j, i: (i, j)),
        out_shape=jax.ShapeDtypeStruct((m, n), _BF16),
        compiler_params=_params(("parallel", "arbitrary")),
        name=name,
    )(*args)


def _mlp_kernel(u_ref, wu_ref, wd_ref, h_ref, g_ref, h_out_ref, y_out_ref):
    f = pl.program_id(1)

    @pl.when(f == 0)
    def _():
        h_out_ref[...] = h_ref[...]

    up = jnp.dot(u_ref[...], wu_ref[...], preferred_element_type=_F32)
    a = jnp.square(jnp.maximum(up, 0.0)).astype(_BF16)
    h_out_ref[...] += jnp.dot(a, wd_ref[...], preferred_element_type=_F32)

    @pl.when(f == pl.num_programs(1) - 1)
    def _():
        y_out_ref[...] = _rms(h_out_ref[...], g_ref[...]).astype(y_out_ref.dtype)


def _mlp(u, w_up_tiled, w_down, h, g, y_dtype, tm):
    m, d = h.shape
    n_f, _, tf = w_up_tiled.shape
    return pl.pallas_call(
        _mlp_kernel,
        grid=(m // tm, n_f),
        in_specs=[
            pl.BlockSpec((tm, d), lambda i, f: (i, 0)),
            pl.BlockSpec((None, d, tf), lambda i, f: (f, 0, 0)),
            pl.BlockSpec((tf, d), lambda i, f: (f, 0)),
            pl.BlockSpec((tm, d), lambda i, f: (i, 0)),
            pl.BlockSpec((1, d), lambda i, f: (0, 0)),
        ],
        out_specs=[pl.BlockSpec((tm, d), lambda i, f: (i, 0)), pl.BlockSpec((tm, d), lambda i, f: (i, 0))],
        out_shape=[jax.ShapeDtypeStruct((m, d), _F32), jax.ShapeDtypeStruct((m, d), y_dtype)],
        compiler_params=_params(("parallel", "arbitrary")),
        name="mlp_relu2",
    )(u, w_up_tiled, w_down, h, g.reshape(1, d))


def _diff_attn_kernel(nfull_ref, nvis_ref, slopes_ref,
                      q_ref, k_ref, v_ref, pq_ref, pk_ref, lq1_ref, lk1_ref, lq2_ref, lk2_ref, g_ref,
                      *rest, lambda_init, tq, tk, hp, n_cast):
    cast_in, (o_ref, *cast_out), (m_s, acc_s, s_s) = rest[:n_cast], rest[n_cast:2 * n_cast + 1], rest[2 * n_cast + 1:]
    for w_ref, w_bf_ref in zip(cast_in, cast_out):
        w_bf_ref[...] = w_ref[...].astype(w_bf_ref.dtype)

    b, hg, i = pl.program_id(0), pl.program_id(1), pl.program_id(2)
    nq = pl.num_programs(2)
    n_full = nfull_ref[b * nq + i]
    n_vis = nvis_ref[b * nq + i]
    n_lane_tiles = tq // LANES
    heads = [slice(hh * HEAD_COLS, (hh + 1) * HEAD_COLS) for hh in range(hp)]

    lane = lax.broadcasted_iota(jnp.int32, (tq, HEAD_COLS), 1)
    zero = jnp.zeros((tq, HEAD_COLS), _BF16)
    q_maps = []
    for cols in heads:
        q = q_ref[:, cols]
        q_maps += [jnp.where(lane < DIFF_QK_DIM, q, zero), jnp.where(lane >= DIFF_QK_DIM, q, zero)]
    slope2 = [slopes_ref[hg * hp + hh] * LOG2E for hh in range(hp)]
    pq = pq_ref[0]
    pq0 = pq[:, 0:1]

    m_s[...] = jnp.full(m_s.shape, NEG_INF, _F32)
    acc_s[...] = jnp.zeros(acc_s.shape, _F32)
    ones_rows = jnp.ones((ONES_ROWS, tk), _BF16)

    n_chains = 2 * hp

    def softmax_pv(ci, j):
        v = v_ref[pl.ds(pl.multiple_of(j * tk, tk), tk), heads[ci // 2]]
        v_t = jnp.concatenate([v.T, ones_rows], axis=0)
        s = s_s[ci]
        m_prev = m_s[ci]
        m_new = jnp.maximum(m_prev, jnp.max(s, axis=0, keepdims=True))
        alpha = jnp.exp2(m_prev - m_new)
        p = jnp.exp2(s - m_new).astype(_BF16)
        acc_s[ci] = alpha * acc_s[ci] + jnp.dot(v_t, p, preferred_element_type=_F32)
        m_s[ci] = m_new

    def step(j, masked, first):
        start = pl.multiple_of(j * tk, tk)
        pk = pk_ref[pl.ds(start, tk), :]
        rel = pk - pq0
        bias = [jnp.concatenate([slope2[hh] * rel] * n_lane_tiles, axis=1) for hh in range(hp)]
        if masked:
            penalty = jnp.concatenate([jnp.where(pq_c >= pk, 0.0, NEG_INF) for pq_c in _lane_tiles(pq)], axis=1)
            bias = [b_h + penalty for b_h in bias]

        def scores(ci):
            k = k_ref[pl.ds(start, tk), heads[ci // 2]]
            s_s[ci] = lax.dot_general(k, q_maps[ci], _NT, preferred_element_type=_F32) + bias[ci // 2]

        scores(0)
        if not first:
            softmax_pv(n_chains - 1, j - 1)
        for ci in range(n_chains - 1):
            scores(ci + 1)
            softmax_pv(ci, j)

    @pl.when(n_full > 0)
    def _():
        step(0, masked=False, first=True)

    @pl.when(n_full == 0)
    def _():
        step(0, masked=True, first=True)

    def full_body(j, c):
        step(j, masked=False, first=False)
        return c

    def masked_body(j, c):
        step(j, masked=True, first=False)
        return c

    lax.fori_loop(1, n_full, full_body, 0)
    lax.fori_loop(jnp.maximum(n_full, 1), n_vis, masked_body, 0)
    softmax_pv(n_chains - 1, n_vis - 1)

    lam = (jnp.exp(jnp.sum(lq1_ref[...] * lk1_ref[...], axis=-1, keepdims=True))
           - jnp.exp(jnp.sum(lq2_ref[...] * lk2_ref[...], axis=-1, keepdims=True)) + lambda_init)
    gain = _lane_tile_repeat(g_ref[...] * (1.0 - lambda_init), tq)
    dv = DIFF_V_DIM
    for hh, cols in enumerate(heads):
        acc1, acc2 = acc_s[2 * hh], acc_s[2 * hh + 1]
        o_t = acc1[:dv] / acc1[dv:dv + 1] - lam * (acc2[:dv] / acc2[dv:dv + 1])
        o_ref[cols, :] = (_rms_over_rows(o_t) * gain).astype(o_ref.dtype)


def _diff_attention(qkv, pos_row, pos_rep, n_full, n_vis, lam_params, g, lambda_init, batch, seq, layer, to_bf16):
    tq, tk, hp = DIFF_TQ, DIFF_TK, DIFF_HEADS_PER_STEP
    nq = seq // tq
    n_hg = DIFF_HEADS // hp
    n_steps = batch * n_hg * nq
    slopes = jnp.asarray(np.array([2.0 ** (-8.0 * (i + 1) / DIFF_HEADS) for i in range(DIFF_HEADS)], np.float32))
    small = lambda b, h, i, *_: (0, 0)
    step_id = lambda b, h, i: (b * n_hg + h) * nq + i
    cast_in_specs, cast_out_specs, cast_out_shapes = [], [], []
    for w, col_tile in to_bf16:
        k_dim, n_dim = w.shape[1:]
        if col_tile is None:
            rows = k_dim // n_steps
            assert rows * n_steps == k_dim and rows % 16 == 0
            cast_in_specs.append(pl.BlockSpec((None, rows, n_dim), lambda b, h, i, *_: (layer, step_id(b, h, i), 0)))
            cast_out_specs.append(pl.BlockSpec((rows, n_dim), lambda b, h, i, *_: (step_id(b, h, i), 0)))
            cast_out_shapes.append(jax.ShapeDtypeStruct((k_dim, n_dim), _BF16))
        else:
            n_tiles = n_dim // col_tile
            per_tile = n_steps // n_tiles
            rows = k_dim // per_tile
            assert n_tiles * col_tile == n_dim and per_tile * n_tiles == n_steps and rows * per_tile == k_dim
            cast_in_specs.append(pl.BlockSpec(
                (None, rows, col_tile),
                lambda b, h, i, *_, p=per_tile: (layer, step_id(b, h, i) % p, step_id(b, h, i) // p)))
            cast_out_specs.append(pl.BlockSpec(
                (None, rows, col_tile),
                lambda b, h, i, *_, p=per_tile: (step_id(b, h, i) // p, step_id(b, h, i) % p, 0)))
            cast_out_shapes.append(jax.ShapeDtypeStruct((n_tiles, k_dim, col_tile), _BF16))
    grid_spec = pltpu.PrefetchScalarGridSpec(
        num_scalar_prefetch=3,
        grid=(batch, n_hg, nq),
        in_specs=[
            pl.BlockSpec((tq, hp * HEAD_COLS), lambda b, h, i, *_: (b * nq + i, DIFF_Q_BLK // hp + h)),
            pl.BlockSpec((seq, hp * HEAD_COLS), lambda b, h, i, *_: (b, DIFF_K_BLK // hp + h)),
            pl.BlockSpec((seq, hp * HEAD_COLS), lambda b, h, i, *_: (b, DIFF_V_BLK // hp + h)),
            pl.BlockSpec((1, 1, tq), lambda b, h, i, *_: (b * nq + i, 0, 0)),
            pl.BlockSpec((seq, LANES), lambda b, h, i, *_: (b, 0)),
            pl.BlockSpec((1, DIFF_QK_DIM), small), pl.BlockSpec((1, DIFF_QK_DIM), small),
            pl.BlockSpec((1, DIFF_QK_DIM), small), pl.BlockSpec((1, DIFF_QK_DIM), small),
            pl.BlockSpec((DIFF_V_DIM, LANES), small),
        ] + cast_in_specs,
        out_specs=[pl.BlockSpec((None, hp * HEAD_COLS, tq), lambda b, h, i, *_: (b, h, i))] + cast_out_specs,
        scratch_shapes=[pltpu.VMEM((2 * hp, 1, tq), _F32),
                        pltpu.VMEM((2 * hp, DIFF_V_DIM + ONES_ROWS, tq), _F32), pltpu.VMEM((2 * hp, tk, tq), _F32)],
    )
    outs = pl.pallas_call(
        functools.partial(_diff_attn_kernel, lambda_init=lambda_init, tq=tq, tk=tk, hp=hp, n_cast=len(to_bf16)),
        grid_spec=grid_spec,
        out_shape=[jax.ShapeDtypeStruct((batch, DIFF_HEADS * DIFF_V_DIM, seq), _BF16)] + cast_out_shapes,
        compiler_params=_params(("parallel", "parallel", "arbitrary")),
        name="diff_attention",
    )(n_full, n_vis, slopes, qkv, qkv, qkv, pos_row.reshape(batch * nq, 1, tq), pos_rep,
      *[p.reshape(1, DIFF_QK_DIM) for p in lam_params], _lane_replicated_column(g), *[w for w, _ in to_bf16])
    return outs[0], outs[1:]


def _sb_attn_kernel(nfull_ref, nvis_ref, q_ref, k_ref, v_ref, pq_ref, pk_ref, ntri_ref, g_ref,
                    o_ref, c_s, w_s, acc_s, z_s, e_s, *, tq, tk, hp):
    b, i = pl.program_id(0), pl.program_id(2)
    nq = pl.num_programs(2)
    n_full = nfull_ref[b * nq + i]
    n_vis = nvis_ref[b * nq + i]
    heads = [slice(hh * HEAD_COLS, (hh + 1) * HEAD_COLS) for hh in range(hp)]
    qs = [q_ref[:, cols] for cols in heads]
    pq = pq_ref[0]
    ntri = ntri_ref[...]

    c_s[...] = jnp.zeros(c_s.shape, _F32)
    acc_s[...] = jnp.zeros(acc_s.shape, _F32)

    def step(blocks):
        starts, keeps, penalties = [], [], []
        for j, masked in blocks:
            start = pl.multiple_of(j * tk, tk)
            starts.append(start)
            if masked:
                pk = pk_ref[pl.ds(start, tk), :]
                strict = [pq_c > pk for pq_c in _lane_tiles(pq)]
                keeps.append(jnp.concatenate([jnp.where(m, 1.0, 0.0) for m in strict], axis=1))
                penalties.append(jnp.concatenate([jnp.where(m, 0.0, NEG_INF) for m in strict], axis=1))
            else:
                keeps.append(None)
                penalties.append(None)

        def logits(c):
            blk, hh = divmod(c, hp)
            k = k_ref[pl.ds(starts[blk], tk), heads[hh]]
            z_s[c] = lax.dot_general(k, qs[hh], _NT, preferred_element_type=_F32)

        def log_weights(c):
            blk, hh = divmod(c, hp)
            z = z_s[c]
            softplus = jnp.maximum(z, 0.0) + jnp.log2(1.0 + jnp.exp2(-jnp.abs(z)))
            log_sig = z - softplus
            if keeps[blk] is not None:
                softplus = softplus * keeps[blk]
            terms = softplus.astype(_BF16)
            later = jnp.dot(ntri, terms, preferred_element_type=_F32)
            e = log_sig + later
            if penalties[blk] is not None:
                e = e + penalties[blk]
            e_s[c] = e
            carry = c_s[hh]
            w_s[c] = jnp.exp2(carry)
            c_s[hh] = carry + (later[0:1, :] - terms[0:1, :].astype(_F32))

        def weighted_values(c):
            blk, hh = divmod(c, hp)
            v = v_ref[pl.ds(starts[blk], tk), heads[hh]]
            a = jnp.exp2(e_s[c]).astype(_BF16)
            acc_s[hh] += w_s[c] * lax.dot_general(v, a, _TN, preferred_element_type=_F32)

        _software_pipeline((logits, log_weights, weighted_values), hp * len(blocks))

    n_masked = n_vis - n_full

    def masked_body(t, c):
        step([(n_vis - 1 - t, True)])
        return c

    lax.fori_loop(0, n_masked - 1, masked_body, 0)
    fuse = jnp.logical_and(n_masked > 0, n_full > 0)

    @pl.when(fuse)
    def _():
        step([(n_full, True), (n_full - 1, False)])

    @pl.when(jnp.logical_and(n_masked > 0, n_full == 0))
    def _():
        step([(n_full, True)])

    def any_weight_left():
        return jnp.max(c_s[...]) > DEAD_CARRY

    def full_cond(state):
        t, go = state
        return jnp.logical_and(t < n_full, go)

    def full_body(state):
        t, _ = state
        step([(n_full - 1 - t, False)])
        return t + 1, any_weight_left()

    lax.while_loop(full_cond, full_body, (fuse.astype(jnp.int32), any_weight_left()))
    gain = _lane_tile_repeat(g_ref[...], tq)
    for hh, cols in enumerate(heads):
        o_ref[cols, :] = (_rms_over_rows(acc_s[hh]) * gain).astype(o_ref.dtype)


def _sb_attention(qkv, pos_row, pos_rep, n_full, n_vis, g, batch, seq):
    tq, tk, hp = SB_TQ, SB_TK, SB_HEADS_PER_STEP
    nq = seq // tq
    idx = np.arange(tk)
    ntri = jnp.asarray(-(idx[None, :] > idx[:, None]).astype(np.float32), _BF16)
    grid_spec = pltpu.PrefetchScalarGridSpec(
        num_scalar_prefetch=2,
        grid=(batch, SB_HEADS // hp, nq),
        in_specs=[
            pl.BlockSpec((tq, hp * HEAD_COLS), lambda b, h, i, *_: (b * nq + i, SB_Q_BLK // hp + h)),
            pl.BlockSpec((seq, hp * HEAD_COLS), lambda b, h, i, *_: (b, SB_K_BLK // hp + h)),
            pl.BlockSpec((seq, hp * HEAD_COLS), lambda b, h, i, *_: (b, SB_V_BLK // hp + h)),
            pl.BlockSpec((1, 1, tq), lambda b, h, i, *_: (b * nq + i, 0, 0)),
            pl.BlockSpec((seq, LANES), lambda b, h, i, *_: (b, 0)),
            pl.BlockSpec((tk, tk), lambda b, h, i, *_: (0, 0)),
            pl.BlockSpec((SB_DIM, LANES), lambda b, h, i, *_: (0, 0)),
        ],
        out_specs=pl.BlockSpec((None, hp * HEAD_COLS, tq), lambda b, h, i, *_: (b, h, i)),
        scratch_shapes=[pltpu.VMEM((hp, 1, tq), _F32), pltpu.VMEM((2 * hp, 1, tq), _F32),
                        pltpu.VMEM((hp, SB_DIM, tq), _F32),
                        pltpu.VMEM((2 * hp, tk, tq), _F32), pltpu.VMEM((2 * hp, tk, tq), _F32)],
    )
    return pl.pallas_call(
        functools.partial(_sb_attn_kernel, tq=tq, tk=tk, hp=hp),
        grid_spec=grid_spec,
        out_shape=jax.ShapeDtypeStruct((batch, SB_HEADS * SB_DIM, seq), _BF16),
        compiler_params=_params(("parallel", "parallel", "arbitrary")),
        name="stick_breaking_attention",
    )(n_full, n_vis, qkv, qkv, qkv, pos_row.reshape(batch * nq, 1, tq), pos_rep, ntri, _lane_replicated_column(g))


def _mix_xattn_block_kernel(od_ref, os_ref, wod_ref, wos_ref, gx_ref, wq_ref, k_ref, v_ref, wo_ref, h_ref, g_ref,
                            h_out_ref, u_out_ref):
    h1 = (h_ref[...] + lax.dot_general(od_ref[...], wod_ref[...], _TN, preferred_element_type=_F32)
          + lax.dot_general(os_ref[...], wos_ref[...], _TN, preferred_element_type=_F32))
    u = _rms(h1, gx_ref[...]).astype(_BF16)
    q = jnp.dot(u, wq_ref[...], preferred_element_type=_F32) * (X_DIM ** -0.5 * LOG2E)
    q = q.astype(_BF16)
    heads_out = []
    for hd in range(X_HEADS):
        cols = slice(hd * X_DIM, (hd + 1) * X_DIM)
        s = lax.dot_general(q[:, cols], k_ref[:, cols], _NT, preferred_element_type=_F32)
        p = jnp.exp2(s - jnp.max(s, axis=-1, keepdims=True))
        o = jnp.dot(p.astype(_BF16), v_ref[:, cols], preferred_element_type=_F32)
        heads_out.append((o / jnp.sum(p, axis=-1, keepdims=True)).astype(_BF16))
    o_all = jnp.concatenate(heads_out, axis=1)
    acc = h1 + jnp.dot(o_all, wo_ref[...], preferred_element_type=_F32)
    h_out_ref[...] = acc
    u_out_ref[...] = _rms(acc, g_ref[...]).astype(u_out_ref.dtype)


def _mix_xattn_block(o_diff, o_sb, w_out, gx, wq, kv, wo, h, g, tm):
    m, d = h.shape
    half, seq = o_diff.shape[1:]
    tiles = seq // tm
    once = pl.Buffered(1)
    row_tile = pl.BlockSpec((tm, d), lambda i: (i, 0))
    vec = pl.BlockSpec((1, d), lambda i: (0, 0))
    return pl.pallas_call(
        _mix_xattn_block_kernel,
        grid=(m // tm,),
        in_specs=[
            pl.BlockSpec((None, half, tm), lambda i: (i // tiles, 0, i % tiles)),
            pl.BlockSpec((None, half, tm), lambda i: (i // tiles, 0, i % tiles)),
            pl.BlockSpec((half, d), lambda i: (0, 0), pipeline_mode=once),
            pl.BlockSpec((half, d), lambda i: (1, 0), pipeline_mode=once),
            vec,
            pl.BlockSpec((d, d), lambda i: (0, 0), pipeline_mode=once),
            pl.BlockSpec((N_MEM, d), lambda i: (i // tiles, 0)),
            pl.BlockSpec((N_MEM, d), lambda i: (i // tiles, 1)),
            pl.BlockSpec((d, d), lambda i: (0, 0), pipeline_mode=once),
            row_tile,
            vec,
        ],
        out_specs=[row_tile, row_tile],
        out_shape=[jax.ShapeDtypeStruct((m, d), _F32), jax.ShapeDtypeStruct((m, d), _BF16)],
        compiler_params=_params(("parallel",)),
        name="mix_xattn_block",
    )(o_diff, o_sb, w_out, w_out, gx.reshape(1, d), wq, kv, kv, wo, h, g.reshape(1, d))


def _visible_block_counts(positions, tq, tk, strict):
    batch, seq = positions.shape
    qmin = positions.reshape(batch, seq // tq, tq).min(-1)[:, :, None]
    qmax = positions.reshape(batch, seq // tq, tq).max(-1)[:, :, None]
    kmin = positions.reshape(batch, seq // tk, tk).min(-1)[:, None, :]
    kmax = positions.reshape(batch, seq // tk, tk).max(-1)[:, None, :]
    full = (kmax < qmin) if strict else (kmax <= qmin)
    some = (kmin < qmax) if strict else (kmin <= qmax)
    n_full = jnp.sum(full, axis=-1).astype(jnp.int32).reshape(-1)
    n_vis = jnp.sum(some, axis=-1).astype(jnp.int32).reshape(-1)
    return n_full, jnp.maximum(n_vis, n_full)


def _in_proj_col_scale():
    scale = np.ones((1, IN_COLS), np.float32)
    scale[:, DIFF_Q_BLK * HEAD_COLS:(DIFF_Q_BLK + DIFF_HEADS) * HEAD_COLS] = DIFF_QK_DIM ** -0.5 * LOG2E
    scale[:, SB_Q_BLK * HEAD_COLS:(SB_Q_BLK + SB_HEADS) * HEAD_COLS] = SB_DIM ** -0.5 * LOG2E
    return jnp.asarray(scale)


def kernel(x, mem, positions, norm_mix, w_in, lam_q1, lam_k1, lam_q2, lam_k2, subln_diff, subln_sb, w_out,
           norm_x, norm_mem, wq_x, wkv_x, wo_x, norm_mlp, w_up, w_down, norm_final):
    batch, seq, d = x.shape
    depth = w_in.shape[0]
    tokens = batch * seq
    assert d == D_MODEL and w_in.shape[2] == IN_COLS
    assert seq % DIFF_TQ == 0 and seq % DIFF_TK == 0 and seq % SB_TQ == 0 and seq % SB_TK == 0

    pos_f = positions.astype(_F32)
    pos_row = pos_f.reshape(batch, seq)
    pos_rep = jnp.broadcast_to(pos_f.reshape(tokens, 1), (tokens, LANES))
    diff_counts = _visible_block_counts(positions, DIFF_TQ, DIFF_TK, strict=False)
    sb_counts = _visible_block_counts(positions, SB_TQ, SB_TK, strict=True)
    in_scale = _in_proj_col_scale()

    h = x.reshape(tokens, d)
    mem2 = mem.reshape(batch * N_MEM, d)
    u = _rmsnorm_bf16(h, norm_mix[0], tm=512)
    out = None
    for l in range(depth):
        lambda_init = 0.8 - 0.6 * math.exp(-0.3 * l)
        qkv = _proj(u, w_in, l, in_scale, tm=1024, tn=1024, name="in_proj")
        o_diff, layer_bf16 = _diff_attention(qkv, pos_row, pos_rep, *diff_counts,
                                             (lam_q1[l], lam_k1[l], lam_q2[l], lam_k2[l]), subln_diff[l],
                                             lambda_init, batch, seq, l,
                                             [(w_out, None), (wq_x, None), (wo_x, None), (w_up, MLP_TF), (w_down, None)])
        w_out_bf, wq_x_bf, wo_x_bf, w_up_tiled, w_down_bf = layer_bf16
        o_sb = _sb_attention(qkv, pos_row, pos_rep, *sb_counts, subln_sb[l], batch, seq)
        mem_n = _rmsnorm_bf16(mem2, norm_mem[l], tm=256)
        kv = _proj(mem_n, wkv_x, l, None, tm=1024, tn=1024, name="xattn_kv_proj")
        h, um = _mix_xattn_block(o_diff, o_sb, w_out_bf, norm_x[l], wq_x_bf, kv, wo_x_bf, h, norm_mlp[l], tm=256)

        last = l == depth - 1
        g_next = norm_final if last else norm_mix[l + 1]
        h, y = _mlp(um, w_up_tiled, w_down_bf, h, g_next, _F32 if last else _BF16, tm=512)
        if last:
            out = y
        else:
            u = y
    return out.reshape(batch, seq, d)
```

```python
import functools
import math

import jax
import jax.numpy as jnp
import numpy as np
from jax import lax
from jax.experimental import pallas as pl
from jax.experimental.pallas import tpu as pltpu

D_MODEL = 2048
N_MEM = 256
DIFF_V_DIM = 128
DIFF_QK_DIM = 64
DIFF_HEADS = 8
SB_DIM = 128
SB_HEADS = 8
X_HEADS = 4
X_DIM = D_MODEL // X_HEADS
D_FF = 4 * D_MODEL
EPS = 1e-6
NEG_INF = -1e30
DEAD_CARRY = -160.0
LOG2E = math.log2(math.e)

LANES = 128
ONES_ROWS = 16
HEAD_COLS = 128
DIFF_Q_BLK = 0
DIFF_K_BLK = DIFF_HEADS
DIFF_V_BLK = 2 * DIFF_HEADS
SB_Q_BLK = 3 * DIFF_HEADS
SB_K_BLK = SB_Q_BLK + SB_HEADS
SB_V_BLK = SB_K_BLK + SB_HEADS
IN_COLS = (3 * DIFF_HEADS + 3 * SB_HEADS) * HEAD_COLS

DIFF_TQ, DIFF_TK = 512, 512
SB_TQ, SB_TK = 256, 256
MLP_TF = 1024
DIFF_HEADS_PER_STEP = 4
SB_HEADS_PER_STEP = 8
VMEM_LIMIT = 56 * 1024 * 1024

_BF16 = jnp.bfloat16
_F32 = jnp.float32
_NT = (((1,), (1,)), ((), ()))
_TN = (((0,), (0,)), ((), ()))


def _params(semantics):
    return pltpu.CompilerParams(dimension_semantics=semantics, vmem_limit_bytes=VMEM_LIMIT)


def _rms(x, g):
    return x * lax.rsqrt(jnp.mean(x * x, axis=-1, keepdims=True) + EPS) * g


def _lane_tiles(x):
    return [x[:, c * LANES:(c + 1) * LANES] for c in range(x.shape[1] // LANES)]


def _lane_tile_repeat(tile, width):
    return jnp.concatenate([tile] * (width // LANES), axis=1)


def _lane_replicated_column(vec):
    return jnp.broadcast_to(vec.astype(_F32).reshape(-1, 1), (vec.shape[0], LANES))


def _rms_over_rows(x_t):
    return x_t * lax.rsqrt(jnp.mean(x_t * x_t, axis=0, keepdims=True) + EPS)


def _software_pipeline(stages, n_chains):
    for t in range(n_chains + len(stages) - 1):
        for s, stage in enumerate(stages):
            if 0 <= t - s < n_chains:
                stage(t - s)


def _proj_kernel(*refs, pre_norm, scaled):
    x_ref, w_ref, o_ref = refs[0], refs[1], refs[-1]
    x = x_ref[...]
    if pre_norm:
        x = _rms(x, refs[2][...]).astype(_BF16)
    acc = jnp.dot(x, w_ref[...].astype(_BF16), preferred_element_type=_F32)
    if scaled:
        acc = acc * refs[-2][...]
    o_ref[...] = acc.astype(o_ref.dtype)


def _proj(x, w_stack, layer, col_scale, tm, tn, name, norm_gain=None):
    m, k = x.shape
    n = w_stack.shape[2]
    in_specs = [pl.BlockSpec((tm, k), lambda j, i: (i, 0)), pl.BlockSpec((None, k, tn), lambda j, i: (layer, 0, j))]
    args = [x, w_stack]
    if norm_gain is not None:
        in_specs.append(pl.BlockSpec((1, k), lambda j, i: (0, 0)))
        args.append(norm_gain.reshape(1, k))
    if col_scale is not None:
        in_specs.append(pl.BlockSpec((1, tn), lambda j, i: (0, j)))
        args.append(col_scale)
    return pl.pallas_call(
        functools.partial(_proj_kernel, pre_norm=norm_gain is not None, scaled=col_scale is not None),
        grid=(n // tn, m // tm),
        in_specs=in_specs,
        out_specs=pl.BlockSpec((tm, tn), lambda j, i: (i, j)),
        out_shape=jax.ShapeDtypeStruct((m, n), _BF16),
        compiler_params=_params(("parallel", "arbitrary")),
        name=name,
    )(*args)


def _mlp_kernel(u_ref, wu_ref, wd_ref, h_ref, g_ref, h_out_ref, y_out_ref):
    f = pl.program_id(1)

    @pl.when(f == 0)
    def _():
        h_out_ref[...] = h_ref[...]

    up = jnp.dot(u_ref[...], wu_ref[...], preferred_element_type=_F32)
    a = jnp.square(jnp.maximum(up, 0.0)).astype(_BF16)
    h_out_ref[...] += jnp.dot(a, wd_ref[...], preferred_element_type=_F32)

    @pl.when(f == pl.num_programs(1) - 1)
    def _():
        y_out_ref[...] = _rms(h_out_ref[...], g_ref[...]).astype(y_out_ref.dtype)


def _mlp(u, w_up_tiled, w_down, h, g, y_dtype, tm):
    m, d = h.shape
    n_f, _, tf = w_up_tiled.shape
    return pl.pallas_call(
        _mlp_kernel,
        grid=(m // tm, n_f),
        in_specs=[
            pl.BlockSpec((tm, d), lambda i, f: (i, 0)),
            pl.BlockSpec((None, d, tf), lambda i, f: (f, 0, 0)),
            pl.BlockSpec((tf, d), lambda i, f: (f, 0)),
            pl.BlockSpec((tm, d), lambda i, f: (i, 0)),
            pl.BlockSpec((1, d), lambda i, f: (0, 0)),
        ],
        out_specs=[pl.BlockSpec((tm, d), lambda i, f: (i, 0)), pl.BlockSpec((tm, d), lambda i, f: (i, 0))],
        out_shape=[jax.ShapeDtypeStruct((m, d), _F32), jax.ShapeDtypeStruct((m, d), y_dtype)],
        compiler_params=_params(("parallel", "arbitrary")),
        name="mlp_relu2",
    )(u, w_up_tiled, w_down, h, g.reshape(1, d))


def _diff_attn_kernel(nfull_ref, nvis_ref, slopes_ref,
                      q_ref, k_ref, v_ref, pq_ref, pk_ref, lq1_ref, lk1_ref, lq2_ref, lk2_ref, g_ref,
                      *rest, lambda_init, tq, tk, hp, n_cast):
    cast_in, (o_ref, *cast_out), (m_s, acc_s, s_s) = rest[:n_cast], rest[n_cast:2 * n_cast + 1], rest[2 * n_cast + 1:]
    for w_ref, w_bf_ref in zip(cast_in, cast_out):
        w_bf_ref[...] = w_ref[...].astype(w_bf_ref.dtype)

    b, hg, i = pl.program_id(0), pl.program_id(1), pl.program_id(2)
    nq = pl.num_programs(2)
    n_full = nfull_ref[b * nq + i]
    n_vis = nvis_ref[b * nq + i]
    n_lane_tiles = tq // LANES
    heads = [slice(hh * HEAD_COLS, (hh + 1) * HEAD_COLS) for hh in range(hp)]

    lane = lax.broadcasted_iota(jnp.int32, (tq, HEAD_COLS), 1)
    zero = jnp.zeros((tq, HEAD_COLS), _BF16)
    q_maps = []
    for cols in heads:
        q = q_ref[:, cols]
        q_maps += [jnp.where(lane < DIFF_QK_DIM, q, zero), jnp.where(lane >= DIFF_QK_DIM, q, zero)]
    slope2 = [slopes_ref[hg * hp + hh] * LOG2E for hh in range(hp)]
    pq = pq_ref[0]
    pq0 = pq[:, 0:1]

    m_s[...] = jnp.full(m_s.shape, NEG_INF, _F32)
    acc_s[...] = jnp.zeros(acc_s.shape, _F32)
    ones_rows = jnp.ones((ONES_ROWS, tk), _BF16)

    n_chains = 2 * hp

    def softmax_pv(ci, j):
        v = v_ref[pl.ds(pl.multiple_of(j * tk, tk), tk), heads[ci // 2]]
        v_t = jnp.concatenate([v.T, ones_rows], axis=0)
        s = s_s[ci]
        m_prev = m_s[ci]
        m_new = jnp.maximum(m_prev, jnp.max(s, axis=0, keepdims=True))
        alpha = jnp.exp2(m_prev - m_new)
        p = jnp.exp2(s - m_new).astype(_BF16)
        acc_s[ci] = alpha * acc_s[ci] + jnp.dot(v_t, p, preferred_element_type=_F32)
        m_s[ci] = m_new

    def step(j, masked, first):
        start = pl.multiple_of(j * tk, tk)
        pk = pk_ref[pl.ds(start, tk), :]
        rel = pk - pq0
        bias = [jnp.concatenate([slope2[hh] * rel] * n_lane_tiles, axis=1) for hh in range(hp)]
        if masked:
            penalty = jnp.concatenate([jnp.where(pq_c >= pk, 0.0, NEG_INF) for pq_c in _lane_tiles(pq)], axis=1)
            bias = [b_h + penalty for b_h in bias]

        def scores(ci):
            k = k_ref[pl.ds(start, tk), heads[ci // 2]]
            s_s[ci] = lax.dot_general(k, q_maps[ci], _NT, preferred_element_type=_F32) + bias[ci // 2]

        scores(0)
        if not first:
            softmax_pv(n_chains - 1, j - 1)
        for ci in range(n_chains - 1):
            scores(ci + 1)
            softmax_pv(ci, j)

    @pl.when(n_full > 0)
    def _():
        step(0, masked=False, first=True)

    @pl.when(n_full == 0)
    def _():
        step(0, masked=True, first=True)

    def full_body(j, c):
        step(j, masked=False, first=False)
        return c

    def masked_body(j, c):
        step(j, masked=True, first=False)
        return c

    lax.fori_loop(1, n_full, full_body, 0)
    lax.fori_loop(jnp.maximum(n_full, 1), n_vis, masked_body, 0)
    softmax_pv(n_chains - 1, n_vis - 1)

    lam = (jnp.exp(jnp.sum(lq1_ref[...] * lk1_ref[...], axis=-1, keepdims=True))
           - jnp.exp(jnp.sum(lq2_ref[...] * lk2_ref[...], axis=-1, keepdims=True)) + lambda_init)
    gain = _lane_tile_repeat(g_ref[...] * (1.0 - lambda_init), tq)
    dv = DIFF_V_DIM
    for hh, cols in enumerate(heads):
        acc1, acc2 = acc_s[2 * hh], acc_s[2 * hh + 1]
        o_t = acc1[:dv] / acc1[dv:dv + 1] - lam * (acc2[:dv] / acc2[dv:dv + 1])
        o_ref[cols, :] = (_rms_over_rows(o_t) * gain).astype(o_ref.dtype)


def _diff_attention(qkv, pos_row, pos_rep, n_full, n_vis, lam_params, g, lambda_init, batch, seq, layer, to_bf16):
    tq, tk, hp = DIFF_TQ, DIFF_TK, DIFF_HEADS_PER_STEP
    nq = seq // tq
    n_hg = DIFF_HEADS // hp
    n_steps = batch * n_hg * nq
    slopes = jnp.asarray(np.array([2.0 ** (-8.0 * (i + 1) / DIFF_HEADS) for i in range(DIFF_HEADS)], np.float32))
    small = lambda b, h, i, *_: (0, 0)
    step_id = lambda b, h, i: (b * n_hg + h) * nq + i
    cast_in_specs, cast_out_specs, cast_out_shapes = [], [], []
    for w, col_tile in to_bf16:
        k_dim, n_dim = w.shape[1:]
        if col_tile is None:
            rows = k_dim // n_steps
            assert rows * n_steps == k_dim and rows % 16 == 0
            cast_in_specs.append(pl.BlockSpec((None, rows, n_dim), lambda b, h, i, *_: (layer, step_id(b, h, i), 0)))
            cast_out_specs.append(pl.BlockSpec((rows, n_dim), lambda b, h, i, *_: (step_id(b, h, i), 0)))
            cast_out_shapes.append(jax.ShapeDtypeStruct((k_dim, n_dim), _BF16))
        else:
            n_tiles = n_dim // col_tile
            per_tile = n_steps // n_tiles
            rows = k_dim // per_tile
            assert n_tiles * col_tile == n_dim and per_tile * n_tiles == n_steps and rows * per_tile == k_dim
            cast_in_specs.append(pl.BlockSpec(
                (None, rows, col_tile),
                lambda b, h, i, *_, p=per_tile: (layer, step_id(b, h, i) % p, step_id(b, h, i) // p)))
            cast_out_specs.append(pl.BlockSpec(
                (None, rows, col_tile),
                lambda b, h, i, *_, p=per_tile: (step_id(b, h, i) // p, step_id(b, h, i) % p, 0)))
            cast_out_shapes.append(jax.ShapeDtypeStruct((n_tiles, k_dim, col_tile), _BF16))
    grid_spec = pltpu.PrefetchScalarGridSpec(
        num_scalar_prefetch=3,
        grid=(batch, n_hg, nq),
        in_specs=[
            pl.BlockSpec((tq, hp * HEAD_COLS), lambda b, h, i, *_: (b * nq + i, DIFF_Q_BLK // hp + h)),
            pl.BlockSpec((seq, hp * HEAD_COLS), lambda b, h, i, *_: (b, DIFF_K_BLK // hp + h)),
            pl.BlockSpec((seq, hp * HEAD_COLS), lambda b, h, i, *_: (b, DIFF_V_BLK // hp + h)),
            pl.BlockSpec((1, 1, tq), lambda b, h, i, *_: (b * nq + i, 0, 0)),
            pl.BlockSpec((seq, LANES), lambda b, h, i, *_: (b, 0)),
            pl.BlockSpec((1, DIFF_QK_DIM), small), pl.BlockSpec((1, DIFF_QK_DIM), small),
            pl.BlockSpec((1, DIFF_QK_DIM), small), pl.BlockSpec((1, DIFF_QK_DIM), small),
            pl.BlockSpec((DIFF_V_DIM, LANES), small),
        ] + cast_in_specs,
        out_specs=[pl.BlockSpec((None, hp * HEAD_COLS, tq), lambda b, h, i, *_: (b, h, i))] + cast_out_specs,
        scratch_shapes=[pltpu.VMEM((2 * hp, 1, tq), _F32),
                        pltpu.VMEM((2 * hp, DIFF_V_DIM + ONES_ROWS, tq), _F32), pltpu.VMEM((2 * hp, tk, tq), _F32)],
    )
    outs = pl.pallas_call(
        functools.partial(_diff_attn_kernel, lambda_init=lambda_init, tq=tq, tk=tk, hp=hp, n_cast=len(to_bf16)),
        grid_spec=grid_spec,
        out_shape=[jax.ShapeDtypeStruct((batch, DIFF_HEADS * DIFF_V_DIM, seq), _BF16)] + cast_out_shapes,
        compiler_params=_params(("parallel", "parallel", "arbitrary")),
        name="diff_attention",
    )(n_full, n_vis, slopes, qkv, qkv, qkv, pos_row.reshape(batch * nq, 1, tq), pos_rep,
      *[p.reshape(1, DIFF_QK_DIM) for p in lam_params], _lane_replicated_column(g), *[w for w, _ in to_bf16])
    return outs[0], outs[1:]


def _sb_attn_kernel(nfull_ref, nvis_ref, q_ref, k_ref, v_ref, pq_ref, pk_ref, ntri_ref, g_ref,
                    o_ref, c_s, w_s, acc_s, z_s, e_s, *, tq, tk, hp):
    b, i = pl.program_id(0), pl.program_id(2)
    nq = pl.num_programs(2)
    n_full = nfull_ref[b * nq + i]
    n_vis = nvis_ref[b * nq + i]
    heads = [slice(hh * HEAD_COLS, (hh + 1) * HEAD_COLS) for hh in range(hp)]
    qs = [q_ref[:, cols] for cols in heads]
    pq = pq_ref[0]
    ntri = ntri_ref[...]

    c_s[...] = jnp.zeros(c_s.shape, _F32)
    acc_s[...] = jnp.zeros(acc_s.shape, _F32)

    def step(blocks):
        starts, keeps, penalties = [], [], []
        for j, masked in blocks:
            start = pl.multiple_of(j * tk, tk)
            starts.append(start)
            if masked:
                pk = pk_ref[pl.ds(start, tk), :]
                strict = [pq_c > pk for pq_c in _lane_tiles(pq)]
                keeps.append(jnp.concatenate([jnp.where(m, 1.0, 0.0) for m in strict], axis=1))
                penalties.append(jnp.concatenate([jnp.where(m, 0.0, NEG_INF) for m in strict], axis=1))
            else:
                keeps.append(None)
                penalties.append(None)

        def logits(c):
            blk, hh = divmod(c, hp)
            k = k_ref[pl.ds(starts[blk], tk), heads[hh]]
            z_s[c] = lax.dot_general(k, qs[hh], _NT, preferred_element_type=_F32)

        def log_weights(c):
            blk, hh = divmod(c, hp)
            z = z_s[c]
            softplus = jnp.maximum(z, 0.0) + jnp.log2(1.0 + jnp.exp2(-jnp.abs(z)))
            log_sig = z - softplus
            if keeps[blk] is not None:
                softplus = softplus * keeps[blk]
            terms = softplus.astype(_BF16)
            later = jnp.dot(ntri, terms, preferred_element_type=_F32)
            e = log_sig + later
            if penalties[blk] is not None:
                e = e + penalties[blk]
            e_s[c] = e
            carry = c_s[hh]
            w_s[c] = jnp.exp2(carry)
            c_s[hh] = carry + (later[0:1, :] - terms[0:1, :].astype(_F32))

        def weighted_values(c):
            blk, hh = divmod(c, hp)
            v = v_ref[pl.ds(starts[blk], tk), heads[hh]]
            a = jnp.exp2(e_s[c]).astype(_BF16)
            acc_s[hh] += w_s[c] * lax.dot_general(v, a, _TN, preferred_element_type=_F32)

        _software_pipeline((logits, log_weights, weighted_values), hp * len(blocks))

    n_masked = n_vis - n_full

    def masked_body(t, c):
        step([(n_vis - 1 - t, True)])
        return c

    lax.fori_loop(0, n_masked - 1, masked_body, 0)
    fuse = jnp.logical_and(n_masked > 0, n_full > 0)

    @pl.when(fuse)
    def _():
        step([(n_full, True), (n_full - 1, False)])

    @pl.when(jnp.logical_and(n_masked > 0, n_full == 0))
    def _():
        step([(n_full, True)])

    def any_weight_left():
        return jnp.max(c_s[...]) > DEAD_CARRY

    def full_cond(state):
        t, go = state
        return jnp.logical_and(t < n_full, go)

    def full_body(state):
        t, _ = state
        step([(n_full - 1 - t, False)])
        return t + 1, any_weight_left()

    lax.while_loop(full_cond, full_body, (fuse.astype(jnp.int32), any_weight_left()))
    gain = _lane_tile_repeat(g_ref[...], tq)
    for hh, cols in enumerate(heads):
        o_ref[cols, :] = (_rms_over_rows(acc_s[hh]) * gain).astype(o_ref.dtype)


def _sb_attention(qkv, pos_row, pos_rep, n_full, n_vis, g, batch, seq):
    tq, tk, hp = SB_TQ, SB_TK, SB_HEADS_PER_STEP
    nq = seq // tq
    idx = np.arange(tk)
    ntri = jnp.asarray(-(idx[None, :] > idx[:, None]).astype(np.float32), _BF16)
    grid_spec = pltpu.PrefetchScalarGridSpec(
        num_scalar_prefetch=2,
        grid=(batch, SB_HEADS // hp, nq),
        in_specs=[
            pl.BlockSpec((tq, hp * HEAD_COLS), lambda b, h, i, *_: (b * nq + i, SB_Q_BLK // hp + h)),
            pl.BlockSpec((seq, hp * HEAD_COLS), lambda b, h, i, *_: (b, SB_K_BLK // hp + h)),
            pl.BlockSpec((seq, hp * HEAD_COLS), lambda b, h, i, *_: (b, SB_V_BLK // hp + h)),
            pl.BlockSpec((1, 1, tq), lambda b, h, i, *_: (b * nq + i, 0, 0)),
            pl.BlockSpec((seq, LANES), lambda b, h, i, *_: (b, 0)),
            pl.BlockSpec((tk, tk), lambda b, h, i, *_: (0, 0)),
            pl.BlockSpec((SB_DIM, LANES), lambda b, h, i, *_: (0, 0)),
        ],
        out_specs=pl.BlockSpec((None, hp * HEAD_COLS, tq), lambda b, h, i, *_: (b, h, i)),
        scratch_shapes=[pltpu.VMEM((hp, 1, tq), _F32), pltpu.VMEM((2 * hp, 1, tq), _F32),
                        pltpu.VMEM((hp, SB_DIM, tq), _F32),
                        pltpu.VMEM((2 * hp, tk, tq), _F32), pltpu.VMEM((2 * hp, tk, tq), _F32)],
    )
    return pl.pallas_call(
        functools.partial(_sb_attn_kernel, tq=tq, tk=tk, hp=hp),
        grid_spec=grid_spec,
        out_shape=jax.ShapeDtypeStruct((batch, SB_HEADS * SB_DIM, seq), _BF16),
        compiler_params=_params(("parallel", "parallel", "arbitrary")),
        name="stick_breaking_attention",
    )(n_full, n_vis, qkv, qkv, qkv, pos_row.reshape(batch * nq, 1, tq), pos_rep, ntri, _lane_replicated_column(g))


def _mix_xattn_block_kernel(od_ref, os_ref, wod_ref, wos_ref, gx_ref, wq_ref, k_ref, v_ref, wo_ref, h_ref, g_ref,
                            h_out_ref, u_out_ref):
    h1 = (h_ref[...] + lax.dot_general(od_ref[...], wod_ref[...], _TN, preferred_element_type=_F32)
          + lax.dot_general(os_ref[...], wos_ref[...], _TN, preferred_element_type=_F32))
    u = _rms(h1, gx_ref[...]).astype(_BF16)
    q = jnp.dot(u, wq_ref[...], preferred_element_type=_F32) * (X_DIM ** -0.5 * LOG2E)
    q = q.astype(_BF16)
    heads_out = []
    for hd in range(X_HEADS):
        cols = slice(hd * X_DIM, (hd + 1) * X_DIM)
        s = lax.dot_general(q[:, cols], k_ref[:, cols], _NT, preferred_element_type=_F32)
        p = jnp.exp2(s - jnp.max(s, axis=-1, keepdims=True))
        o = jnp.dot(p.astype(_BF16), v_ref[:, cols], preferred_element_type=_F32)
        heads_out.append((o / jnp.sum(p, axis=-1, keepdims=True)).astype(_BF16))
    o_all = jnp.concatenate(heads_out, axis=1)
    acc = h1 + jnp.dot(o_all, wo_ref[...], preferred_element_type=_F32)
    h_out_ref[...] = acc
    u_out_ref[...] = _rms(acc, g_ref[...]).astype(u_out_ref.dtype)


def _mix_xattn_block(o_diff, o_sb, w_out, gx, wq, kv, wo, h, g, tm):
    m, d = h.shape
    half, seq = o_diff.shape[1:]
    tiles = seq // tm
    once = pl.Buffered(1)
    row_tile = pl.BlockSpec((tm, d), lambda i: (i, 0))
    vec = pl.BlockSpec((1, d), lambda i: (0, 0))
    return pl.pallas_call(
        _mix_xattn_block_kernel,
        grid=(m // tm,),
        in_specs=[
            pl.BlockSpec((None, half, tm), lambda i: (i // tiles, 0, i % tiles)),
            pl.BlockSpec((None, half, tm), lambda i: (i // tiles, 0, i % tiles)),
            pl.BlockSpec((half, d), lambda i: (0, 0), pipeline_mode=once),
            pl.BlockSpec((half, d), lambda i: (1, 0), pipeline_mode=once),
            vec,
            pl.BlockSpec((d, d), lambda i: (0, 0), pipeline_mode=once),
            pl.BlockSpec((N_MEM, d), lambda i: (i // tiles, 0)),
            pl.BlockSpec((N_MEM, d), lambda i: (i // tiles, 1)),
            pl.BlockSpec((d, d), lambda i: (0, 0), pipeline_mode=once),
            row_tile,
            vec,
        ],
        out_specs=[row_tile, row_tile],
        out_shape=[jax.ShapeDtypeStruct((m, d), _F32), jax.ShapeDtypeStruct((m, d), _BF16)],
        compiler_params=_params(("parallel",)),
        name="mix_xattn_block",
    )(o_diff, o_sb, w_out, w_out, gx.reshape(1, d), wq, kv, kv, wo, h, g.reshape(1, d))


def _visible_block_counts(positions, tq, tk, strict):
    batch, seq = positions.shape
    qmin = positions.reshape(batch, seq // tq, tq).min(-1)[:, :, None]
    qmax = positions.reshape(batch, seq // tq, tq).max(-1)[:, :, None]
    kmin = positions.reshape(batch, seq // tk, tk).min(-1)[:, None, :]
    kmax = positions.reshape(batch, seq // tk, tk).max(-1)[:, None, :]
    full = (kmax < qmin) if strict else (kmax <= qmin)
    some = (kmin < qmax) if strict else (kmin <= qmax)
    n_full = jnp.sum(full, axis=-1).astype(jnp.int32).reshape(-1)
    n_vis = jnp.sum(some, axis=-1).astype(jnp.int32).reshape(-1)
    return n_full, jnp.maximum(n_vis, n_full)


def _in_proj_col_scale():
    scale = np.ones((1, IN_COLS), np.float32)
    scale[:, DIFF_Q_BLK * HEAD_COLS:(DIFF_Q_BLK + DIFF_HEADS) * HEAD_COLS] = DIFF_QK_DIM ** -0.5 * LOG2E
    scale[:, SB_Q_BLK * HEAD_COLS:(SB_Q_BLK + SB_HEADS) * HEAD_COLS] = SB_DIM ** -0.5 * LOG2E
    return jnp.asarray(scale)


def kernel(x, mem, positions, norm_mix, w_in, lam_q1, lam_k1, lam_q2, lam_k2, subln_diff, subln_sb, w_out,
           norm_x, norm_mem, wq_x, wkv_x, wo_x, norm_mlp, w_up, w_down, norm_final):
    batch, seq, d = x.shape
    depth = w_in.shape[0]
    tokens = batch * seq
    assert d == D_MODEL and w_in.shape[2] == IN_COLS
    assert seq % DIFF_TQ == 0 and seq % DIFF_TK == 0 and seq % SB_TQ == 0 and seq % SB_TK == 0

    pos_f = positions.astype(_F32)
    pos_row = pos_f.reshape(batch, seq)
    pos_rep = jnp.broadcast_to(pos_f.reshape(tokens, 1), (tokens, LANES))
    diff_counts = _visible_block_counts(positions, DIFF_TQ, DIFF_TK, strict=False)
    sb_counts = _visible_block_counts(positions, SB_TQ, SB_TK, strict=True)
    in_scale = _in_proj_col_scale()

    h = x.reshape(tokens, d)
    mem2 = mem.reshape(batch * N_MEM, d)
    u = None
    out = None
    for l in range(depth):
        lambda_init = 0.8 - 0.6 * math.exp(-0.3 * l)
        if u is None:
            qkv = _proj(h, w_in, l, in_scale, tm=1024, tn=1024, name="in_proj", norm_gain=norm_mix[l])
        else:
            qkv = _proj(u, w_in, l, in_scale, tm=1024, tn=1024, name="in_proj")
        o_diff, layer_bf16 = _diff_attention(qkv, pos_row, pos_rep, *diff_counts,
                                             (lam_q1[l], lam_k1[l], lam_q2[l], lam_k2[l]), subln_diff[l],
                                             lambda_init, batch, seq, l,
                                             [(w_out, None), (wq_x, None), (wo_x, None), (w_up, MLP_TF), (w_down, None)])
        w_out_bf, wq_x_bf, wo_x_bf, w_up_tiled, w_down_bf = layer_bf16
        o_sb = _sb_attention(qkv, pos_row, pos_rep, *sb_counts, subln_sb[l], batch, seq)
        kv = _proj(mem2, wkv_x, l, None, tm=1024, tn=1024, name="xattn_kv_proj", norm_gain=norm_mem[l])
        h, um = _mix_xattn_block(o_diff, o_sb, w_out_bf, norm_x[l], wq_x_bf, kv, wo_x_bf, h, norm_mlp[l], tm=256)

        last = l == depth - 1
        g_next = norm_final if last else norm_mix[l + 1]
        h, y = _mlp(um, w_up_tiled, w_down_bf, h, g_next, _F32 if last else _BF16, tm=512)
        if last:
            out = y
        else:
            u = y
    return out.reshape(batch, seq, d)
```

```python
import functools
import math

import jax
import jax.numpy as jnp
import numpy as np
from jax import lax
from jax.experimental import pallas as pl
from jax.experimental.pallas import tpu as pltpu

D_MODEL = 2048
N_MEM = 256
DIFF_V_DIM = 128
DIFF_QK_DIM = 64
DIFF_HEADS = 8
SB_DIM = 128
SB_HEADS = 8
X_HEADS = 4
X_DIM = D_MODEL // X_HEADS
D_FF = 4 * D_MODEL
EPS = 1e-6
NEG_INF = -1e30
DEAD_CARRY = -160.0
LOG2E = math.log2(math.e)

LANES = 128
SUBLANES = 8
ONES_ROWS = 16
HEAD_COLS = 128
DIFF_Q_BLK = 0
DIFF_K_BLK = DIFF_HEADS
DIFF_V_BLK = 2 * DIFF_HEADS
SB_Q_BLK = 3 * DIFF_HEADS
SB_K_BLK = SB_Q_BLK + SB_HEADS
SB_V_BLK = SB_K_BLK + SB_HEADS
IN_COLS = (3 * DIFF_HEADS + 3 * SB_HEADS) * HEAD_COLS

DIFF_TQ, DIFF_TK = 512, 512
SB_TQ, SB_TK = 256, 256
MLP_TF = 1024
DIFF_HEADS_PER_STEP = 4
SB_HEADS_PER_STEP = 8
VMEM_LIMIT = 56 * 1024 * 1024

_BF16 = jnp.bfloat16
_F32 = jnp.float32
_NT = (((1,), (1,)), ((), ()))
_TN = (((0,), (0,)), ((), ()))


def _params(semantics):
    return pltpu.CompilerParams(dimension_semantics=semantics, vmem_limit_bytes=VMEM_LIMIT)


def _rms(x, g):
    return x * lax.rsqrt(jnp.mean(x * x, axis=-1, keepdims=True) + EPS) * g


def _lane_tiles(x):
    return [x[:, c * LANES:(c + 1) * LANES] for c in range(x.shape[1] // LANES)]


def _lane_tile_repeat(tile, width):
    return jnp.concatenate([tile] * (width // LANES), axis=1)


def _lane_replicated_column(vec):
    return jnp.broadcast_to(vec.astype(_F32).reshape(-1, 1), (vec.shape[0], LANES))


def _rms_over_rows(x_t):
    return x_t * lax.rsqrt(jnp.mean(x_t * x_t, axis=0, keepdims=True) + EPS)


def _staging_scratch(n_chains, rows, cols):
    assert (rows // SUBLANES) % 2 == 0 and (cols // LANES) % 2 == 0
    return pltpu.VMEM((n_chains, rows + SUBLANES, cols + LANES), _F32)


def _software_pipeline(stages, n_chains):
    for t in range(n_chains + len(stages) - 1):
        for s, stage in enumerate(stages):
            if 0 <= t - s < n_chains:
                stage(t - s)


def _rmsnorm_kernel(x_ref, g_ref, o_ref):
    o_ref[...] = _rms(x_ref[...], g_ref[...]).astype(o_ref.dtype)


def _rmsnorm_bf16(x, g, tm):
    m, d = x.shape
    return pl.pallas_call(
        _rmsnorm_kernel,
        grid=(m // tm,),
        in_specs=[pl.BlockSpec((tm, d), lambda i: (i, 0)), pl.BlockSpec((1, d), lambda i: (0, 0))],
        out_specs=pl.BlockSpec((tm, d), lambda i: (i, 0)),
        out_shape=jax.ShapeDtypeStruct((m, d), _BF16),
        compiler_params=_params(("parallel",)),
        name="rmsnorm_bf16",
    )(x, g.reshape(1, d))


def _proj_kernel(x_ref, w_ref, *rest):
    o_ref = rest[-1]
    acc = jnp.dot(x_ref[...], w_ref[...].astype(_BF16), preferred_element_type=_F32)
    if len(rest) == 2:
        acc = acc * rest[0][...]
    o_ref[...] = acc.astype(o_ref.dtype)


def _proj(x, w_stack, layer, col_scale, tm, tn, name):
    m, k = x.shape
    n = w_stack.shape[2]
    in_specs = [pl.BlockSpec((tm, k), lambda j, i: (i, 0)), pl.BlockSpec((None, k, tn), lambda j, i: (layer, 0, j))]
    args = [x, w_stack]
    if col_scale is not None:
        in_specs.append(pl.BlockSpec((1, tn), lambda j, i: (0, j)))
        args.append(col_scale)
    return pl.pallas_call(
        _proj_kernel,
        grid=(n // tn, m // tm),
        in_specs=in_specs,
        out_specs=pl.BlockSpec((tm, tn), lambda j, i: (i, j)),
        out_shape=jax.ShapeDtypeStruct((m, n), _BF16),
        compiler_params=_params(("parallel", "arbitrary")),
        name=name,
    )(*args)


def _mlp_kernel(u_ref, wu_ref, wd_ref, h_ref, g_ref, h_out_ref, y_out_ref):
    f = pl.program_id(1)

    @pl.when(f == 0)
    def _():
        h_out_ref[...] = h_ref[...]

    up = jnp.dot(u_ref[...], wu_ref[...], preferred_element_type=_F32)
    a = jnp.square(jnp.maximum(up, 0.0)).astype(_BF16)
    h_out_ref[...] += jnp.dot(a, wd_ref[...], preferred_element_type=_F32)

    @pl.when(f == pl.num_programs(1) - 1)
    def _():
        y_out_ref[...] = _rms(h_out_ref[...], g_ref[...]).astype(y_out_ref.dtype)


def _mlp(u, w_up_tiled, w_down, h, g, y_dtype, tm):
    m, d = h.shape
    n_f, _, tf = w_up_tiled.shape
    return pl.pallas_call(
        _mlp_kernel,
        grid=(m // tm, n_f),
        in_specs=[
            pl.BlockSpec((tm, d), lambda i, f: (i, 0)),
            pl.BlockSpec((None, d, tf), lambda i, f: (f, 0, 0)),
            pl.BlockSpec((tf, d), lambda i, f: (f, 0)),
            pl.BlockSpec((tm, d), lambda i, f: (i, 0)),
            pl.BlockSpec((1, d), lambda i, f: (0, 0)),
        ],
        out_specs=[pl.BlockSpec((tm, d), lambda i, f: (i, 0)), pl.BlockSpec((tm, d), lambda i, f: (i, 0))],
        out_shape=[jax.ShapeDtypeStruct((m, d), _F32), jax.ShapeDtypeStruct((m, d), y_dtype)],
        compiler_params=_params(("parallel", "arbitrary")),
        name="mlp_relu2",
    )(u, w_up_tiled, w_down, h, g.reshape(1, d))


def _diff_attn_kernel(nfull_ref, nvis_ref, slopes_ref,
                      q_ref, k_ref, v_ref, pq_ref, pk_ref, lq1_ref, lk1_ref, lq2_ref, lk2_ref, g_ref,
                      *rest, lambda_init, tq, tk, hp, n_cast):
    cast_in, (o_ref, *cast_out), (m_s, acc_s, s_s) = rest[:n_cast], rest[n_cast:2 * n_cast + 1], rest[2 * n_cast + 1:]
    for w_ref, w_bf_ref in zip(cast_in, cast_out):
        w_bf_ref[...] = w_ref[...].astype(w_bf_ref.dtype)

    b, hg, i = pl.program_id(0), pl.program_id(1), pl.program_id(2)
    nq = pl.num_programs(2)
    n_full = nfull_ref[b * nq + i]
    n_vis = nvis_ref[b * nq + i]
    n_lane_tiles = tq // LANES
    heads = [slice(hh * HEAD_COLS, (hh + 1) * HEAD_COLS) for hh in range(hp)]

    lane = lax.broadcasted_iota(jnp.int32, (tq, HEAD_COLS), 1)
    zero = jnp.zeros((tq, HEAD_COLS), _BF16)
    q_maps = []
    for cols in heads:
        q = q_ref[:, cols]
        q_maps += [jnp.where(lane < DIFF_QK_DIM, q, zero), jnp.where(lane >= DIFF_QK_DIM, q, zero)]
    slope2 = [slopes_ref[hg * hp + hh] * LOG2E for hh in range(hp)]
    pq = pq_ref[0]
    pq0 = pq[:, 0:1]

    m_s[...] = jnp.full(m_s.shape, NEG_INF, _F32)
    acc_s[...] = jnp.zeros(acc_s.shape, _F32)
    ones_rows = jnp.ones((ONES_ROWS, tk), _BF16)

    n_chains = 2 * hp

    def softmax_pv(ci, j):
        v = v_ref[pl.ds(pl.multiple_of(j * tk, tk), tk), heads[ci // 2]]
        v_t = jnp.concatenate([v.T, ones_rows], axis=0)
        s = s_s[ci, :tk, :tq]
        m_prev = m_s[ci]
        m_new = jnp.maximum(m_prev, jnp.max(s, axis=0, keepdims=True))
        alpha = jnp.exp2(m_prev - m_new)
        p = jnp.exp2(s - m_new).astype(_BF16)
        acc_s[ci] = alpha * acc_s[ci] + jnp.dot(v_t, p, preferred_element_type=_F32)
        m_s[ci] = m_new

    def step(j, masked, first):
        start = pl.multiple_of(j * tk, tk)
        pk = pk_ref[pl.ds(start, tk), :]
        rel = pk - pq0
        bias = [jnp.concatenate([slope2[hh] * rel] * n_lane_tiles, axis=1) for hh in range(hp)]
        if masked:
            penalty = jnp.concatenate([jnp.where(pq_c >= pk, 0.0, NEG_INF) for pq_c in _lane_tiles(pq)], axis=1)
            bias = [b_h + penalty for b_h in bias]

        def scores(ci):
            k = k_ref[pl.ds(start, tk), heads[ci // 2]]
            s_s[ci, :tk, :tq] = (lax.dot_general(k, q_maps[ci], _NT, preferred_element_type=_F32)
                                 + bias[ci // 2])

        scores(0)
        if not first:
            softmax_pv(n_chains - 1, j - 1)
        for ci in range(n_chains - 1):
            scores(ci + 1)
            softmax_pv(ci, j)

    @pl.when(n_full > 0)
    def _():
        step(0, masked=False, first=True)

    @pl.when(n_full == 0)
    def _():
        step(0, masked=True, first=True)

    def full_body(j, c):
        step(j, masked=False, first=False)
        return c

    def masked_body(j, c):
        step(j, masked=True, first=False)
        return c

    lax.fori_loop(1, n_full, full_body, 0)
    lax.fori_loop(jnp.maximum(n_full, 1), n_vis, masked_body, 0)
    softmax_pv(n_chains - 1, n_vis - 1)

    lam = (jnp.exp(jnp.sum(lq1_ref[...] * lk1_ref[...], axis=-1, keepdims=True))
           - jnp.exp(jnp.sum(lq2_ref[...] * lk2_ref[...], axis=-1, keepdims=True)) + lambda_init)
    gain = _lane_tile_repeat(g_ref[...] * (1.0 - lambda_init), tq)
    dv = DIFF_V_DIM
    for hh, cols in enumerate(heads):
        acc1, acc2 = acc_s[2 * hh], acc_s[2 * hh + 1]
        o_t = acc1[:dv] / acc1[dv:dv + 1] - lam * (acc2[:dv] / acc2[dv:dv + 1])
        o_ref[cols, :] = (_rms_over_rows(o_t) * gain).astype(o_ref.dtype)


def _diff_attention(qkv, pos_row, pos_rep, n_full, n_vis, lam_params, g, lambda_init, batch, seq, layer, to_bf16):
    tq, tk, hp = DIFF_TQ, DIFF_TK, DIFF_HEADS_PER_STEP
    nq = seq // tq
    n_hg = DIFF_HEADS // hp
    n_steps = batch * n_hg * nq
    slopes = jnp.asarray(np.array([2.0 ** (-8.0 * (i + 1) / DIFF_HEADS) for i in range(DIFF_HEADS)], np.float32))
    small = lambda b, h, i, *_: (0, 0)
    step_id = lambda b, h, i: (b * n_hg + h) * nq + i
    cast_in_specs, cast_out_specs, cast_out_shapes = [], [], []
    for w, col_tile in to_bf16:
        k_dim, n_dim = w.shape[1:]
        if col_tile is None:
            rows = k_dim // n_steps
            assert rows * n_steps == k_dim and rows % 16 == 0
            cast_in_specs.append(pl.BlockSpec((None, rows, n_dim), lambda b, h, i, *_: (layer, step_id(b, h, i), 0)))
            cast_out_specs.append(pl.BlockSpec((rows, n_dim), lambda b, h, i, *_: (step_id(b, h, i), 0)))
            cast_out_shapes.append(jax.ShapeDtypeStruct((k_dim, n_dim), _BF16))
        else:
            n_tiles = n_dim // col_tile
            per_tile = n_steps // n_tiles
            rows = k_dim // per_tile
            assert n_tiles * col_tile == n_dim and per_tile * n_tiles == n_steps and rows * per_tile == k_dim
            cast_in_specs.append(pl.BlockSpec(
                (None, rows, col_tile),
                lambda b, h, i, *_, p=per_tile: (layer, step_id(b, h, i) % p, step_id(b, h, i) // p)))
            cast_out_specs.append(pl.BlockSpec(
                (None, rows, col_tile),
                lambda b, h, i, *_, p=per_tile: (step_id(b, h, i) // p, step_id(b, h, i) % p, 0)))
            cast_out_shapes.append(jax.ShapeDtypeStruct((n_tiles, k_dim, col_tile), _BF16))
    grid_spec = pltpu.PrefetchScalarGridSpec(
        num_scalar_prefetch=3,
        grid=(batch, n_hg, nq),
        in_specs=[
            pl.BlockSpec((tq, hp * HEAD_COLS), lambda b, h, i, *_: (b * nq + i, DIFF_Q_BLK // hp + h)),
            pl.BlockSpec((seq, hp * HEAD_COLS), lambda b, h, i, *_: (b, DIFF_K_BLK // hp + h)),
            pl.BlockSpec((seq, hp * HEAD_COLS), lambda b, h, i, *_: (b, DIFF_V_BLK // hp + h)),
            pl.BlockSpec((1, 1, tq), lambda b, h, i, *_: (b * nq + i, 0, 0)),
            pl.BlockSpec((seq, LANES), lambda b, h, i, *_: (b, 0)),
            pl.BlockSpec((1, DIFF_QK_DIM), small), pl.BlockSpec((1, DIFF_QK_DIM), small),
            pl.BlockSpec((1, DIFF_QK_DIM), small), pl.BlockSpec((1, DIFF_QK_DIM), small),
            pl.BlockSpec((DIFF_V_DIM, LANES), small),
        ] + cast_in_specs,
        out_specs=[pl.BlockSpec((None, hp * HEAD_COLS, tq), lambda b, h, i, *_: (b, h, i))] + cast_out_specs,
        scratch_shapes=[pltpu.VMEM((2 * hp, 1, tq), _F32),
                        pltpu.VMEM((2 * hp, DIFF_V_DIM + ONES_ROWS, tq), _F32), _staging_scratch(2 * hp, tk, tq)],
    )
    outs = pl.pallas_call(
        functools.partial(_diff_attn_kernel, lambda_init=lambda_init, tq=tq, tk=tk, hp=hp, n_cast=len(to_bf16)),
        grid_spec=grid_spec,
        out_shape=[jax.ShapeDtypeStruct((batch, DIFF_HEADS * DIFF_V_DIM, seq), _BF16)] + cast_out_shapes,
        compiler_params=_params(("parallel", "parallel", "arbitrary")),
        name="diff_attention",
    )(n_full, n_vis, slopes, qkv, qkv, qkv, pos_row.reshape(batch * nq, 1, tq), pos_rep,
      *[p.reshape(1, DIFF_QK_DIM) for p in lam_params], _lane_replicated_column(g), *[w for w, _ in to_bf16])
    return outs[0], outs[1:]


def _sb_attn_kernel(nfull_ref, nvis_ref, q_ref, k_ref, v_ref, pq_ref, pk_ref, ntri_ref, g_ref,
                    o_ref, c_s, w_s, acc_s, z_s, e_s, *, tq, tk, hp):
    b, i = pl.program_id(0), pl.program_id(2)
    nq = pl.num_programs(2)
    n_full = nfull_ref[b * nq + i]
    n_vis = nvis_ref[b * nq + i]
    heads = [slice(hh * HEAD_COLS, (hh + 1) * HEAD_COLS) for hh in range(hp)]
    qs = [q_ref[:, cols] for cols in heads]
    pq = pq_ref[0]
    ntri = ntri_ref[...]

    c_s[...] = jnp.zeros(c_s.shape, _F32)
    acc_s[...] = jnp.zeros(acc_s.shape, _F32)

    def step(blocks):
        starts, keeps, penalties = [], [], []
        for j, masked in blocks:
            start = pl.multiple_of(j * tk, tk)
            starts.append(start)
            if masked:
                pk = pk_ref[pl.ds(start, tk), :]
                strict = [pq_c > pk for pq_c in _lane_tiles(pq)]
                keeps.append(jnp.concatenate([jnp.where(m, 1.0, 0.0) for m in strict], axis=1))
                penalties.append(jnp.concatenate([jnp.where(m, 0.0, NEG_INF) for m in strict], axis=1))
            else:
                keeps.append(None)
                penalties.append(None)

        def logits(c):
            blk, hh = divmod(c, hp)
            k = k_ref[pl.ds(starts[blk], tk), heads[hh]]
            z_s[c, :tk, :tq] = lax.dot_general(k, qs[hh], _NT, preferred_element_type=_F32)

        def log_weights(c):
            blk, hh = divmod(c, hp)
            z = z_s[c, :tk, :tq]
            softplus = jnp.maximum(z, 0.0) + jnp.log2(1.0 + jnp.exp2(-jnp.abs(z)))
            log_sig = z - softplus
            if keeps[blk] is not None:
                softplus = softplus * keeps[blk]
            terms = softplus.astype(_BF16)
            later = jnp.dot(ntri, terms, preferred_element_type=_F32)
            e = log_sig + later
            if penalties[blk] is not None:
                e = e + penalties[blk]
            e_s[c, :tk, :tq] = e
            carry = c_s[hh]
            w_s[c] = jnp.exp2(carry)
            c_s[hh] = carry + (later[0:1, :] - terms[0:1, :].astype(_F32))

        def weighted_values(c):
            blk, hh = divmod(c, hp)
            v = v_ref[pl.ds(starts[blk], tk), heads[hh]]
            a = jnp.exp2(e_s[c, :tk, :tq]).astype(_BF16)
            acc_s[hh] += w_s[c] * lax.dot_general(v, a, _TN, preferred_element_type=_F32)

        _software_pipeline((logits, log_weights, weighted_values), hp * len(blocks))

    n_masked = n_vis - n_full

    def masked_body(t, c):
        step([(n_vis - 1 - t, True)])
        return c

    lax.fori_loop(0, n_masked - 1, masked_body, 0)
    fuse = jnp.logical_and(n_masked > 0, n_full > 0)

    @pl.when(fuse)
    def _():
        step([(n_full, True), (n_full - 1, False)])

    @pl.when(jnp.logical_and(n_masked > 0, n_full == 0))
    def _():
        step([(n_full, True)])

    def any_weight_left():
        return jnp.max(c_s[...]) > DEAD_CARRY

    def full_cond(state):
        t, go = state
        return jnp.logical_and(t < n_full, go)

    def full_body(state):
        t, _ = state
        step([(n_full - 1 - t, False)])
        return t + 1, any_weight_left()

    lax.while_loop(full_cond, full_body, (fuse.astype(jnp.int32), any_weight_left()))
    gain = _lane_tile_repeat(g_ref[...], tq)
    for hh, cols in enumerate(heads):
        o_ref[cols, :] = (_rms_over_rows(acc_s[hh]) * gain).astype(o_ref.dtype)


def _sb_attention(qkv, pos_row, pos_rep, n_full, n_vis, g, batch, seq):
    tq, tk, hp = SB_TQ, SB_TK, SB_HEADS_PER_STEP
    nq = seq // tq
    idx = np.arange(tk)
    ntri = jnp.asarray(-(idx[None, :] > idx[:, None]).astype(np.float32), _BF16)
    grid_spec = pltpu.PrefetchScalarGridSpec(
        num_scalar_prefetch=2,
        grid=(batch, SB_HEADS // hp, nq),
        in_specs=[
            pl.BlockSpec((tq, hp * HEAD_COLS), lambda b, h, i, *_: (b * nq + i, SB_Q_BLK // hp + h)),
            pl.BlockSpec((seq, hp * HEAD_COLS), lambda b, h, i, *_: (b, SB_K_BLK // hp + h)),
            pl.BlockSpec((seq, hp * HEAD_COLS), lambda b, h, i, *_: (b, SB_V_BLK // hp + h)),
            pl.BlockSpec((1, 1, tq), lambda b, h, i, *_: (b * nq + i, 0, 0)),
            pl.BlockSpec((seq, LANES), lambda b, h, i, *_: (b, 0)),
            pl.BlockSpec((tk, tk), lambda b, h, i, *_: (0, 0)),
            pl.BlockSpec((SB_DIM, LANES), lambda b, h, i, *_: (0, 0)),
        ],
        out_specs=pl.BlockSpec((None, hp * HEAD_COLS, tq), lambda b, h, i, *_: (b, h, i)),
        scratch_shapes=[pltpu.VMEM((hp, 1, tq), _F32), pltpu.VMEM((2 * hp, 1, tq), _F32),
                        pltpu.VMEM((hp, SB_DIM, tq), _F32),
                        _staging_scratch(2 * hp, tk, tq), _staging_scratch(2 * hp, tk, tq)],
    )
    return pl.pallas_call(
        functools.partial(_sb_attn_kernel, tq=tq, tk=tk, hp=hp),
        grid_spec=grid_spec,
        out_shape=jax.ShapeDtypeStruct((batch, SB_HEADS * SB_DIM, seq), _BF16),
        compiler_params=_params(("parallel", "parallel", "arbitrary")),
        name="stick_breaking_attention",
    )(n_full, n_vis, qkv, qkv, qkv, pos_row.reshape(batch * nq, 1, tq), pos_rep, ntri, _lane_replicated_column(g))


def _mix_xattn_block_kernel(od_ref, os_ref, wod_ref, wos_ref, gx_ref, wq_ref, k_ref, v_ref, wo_ref, h_ref, g_ref,
                            h_out_ref, u_out_ref):
    h1 = (h_ref[...] + lax.dot_general(od_ref[...], wod_ref[...], _TN, preferred_element_type=_F32)
          + lax.dot_general(os_ref[...], wos_ref[...], _TN, preferred_element_type=_F32))
    u = _rms(h1, gx_ref[...]).astype(_BF16)
    q = jnp.dot(u, wq_ref[...], preferred_element_type=_F32) * (X_DIM ** -0.5 * LOG2E)
    q = q.astype(_BF16)
    heads_out = []
    for hd in range(X_HEADS):
        cols = slice(hd * X_DIM, (hd + 1) * X_DIM)
        s = lax.dot_general(q[:, cols], k_ref[:, cols], _NT, preferred_element_type=_F32)
        p = jnp.exp2(s - jnp.max(s, axis=-1, keepdims=True))
        o = jnp.dot(p.astype(_BF16), v_ref[:, cols], preferred_element_type=_F32)
        heads_out.append((o / jnp.sum(p, axis=-1, keepdims=True)).astype(_BF16))
    o_all = jnp.concatenate(heads_out, axis=1)
    acc = h1 + jnp.dot(o_all, wo_ref[...], preferred_element_type=_F32)
    h_out_ref[...] = acc
    u_out_ref[...] = _rms(acc, g_ref[...]).astype(u_out_ref.dtype)


def _mix_xattn_block(o_diff, o_sb, w_out, gx, wq, kv, wo, h, g, tm):
    m, d = h.shape
    half, seq = o_diff.shape[1:]
    tiles = seq // tm
    once = pl.Buffered(1)
    row_tile = pl.BlockSpec((tm, d), lambda i: (i, 0))
    vec = pl.BlockSpec((1, d), lambda i: (0, 0))
    return pl.pallas_call(
        _mix_xattn_block_kernel,
        grid=(m // tm,),
        in_specs=[
            pl.BlockSpec((None, half, tm), lambda i: (i // tiles, 0, i % tiles)),
            pl.BlockSpec((None, half, tm), lambda i: (i // tiles, 0, i % tiles)),
            pl.BlockSpec((half, d), lambda i: (0, 0), pipeline_mode=once),
            pl.BlockSpec((half, d), lambda i: (1, 0), pipeline_mode=once),
            vec,
            pl.BlockSpec((d, d), lambda i: (0, 0), pipeline_mode=once),
            pl.BlockSpec((N_MEM, d), lambda i: (i // tiles, 0)),
            pl.BlockSpec((N_MEM, d), lambda i: (i // tiles, 1)),
            pl.BlockSpec((d, d), lambda i: (0, 0), pipeline_mode=once),
            row_tile,
            vec,
        ],
        out_specs=[row_tile, row_tile],
        out_shape=[jax.ShapeDtypeStruct((m, d), _F32), jax.ShapeDtypeStruct((m, d), _BF16)],
        compiler_params=_params(("parallel",)),
        name="mix_xattn_block",
    )(o_diff, o_sb, w_out, w_out, gx.reshape(1, d), wq, kv, kv, wo, h, g.reshape(1, d))


def _visible_block_counts(positions, tq, tk, strict):
    batch, seq = positions.shape
    qmin = positions.reshape(batch, seq // tq, tq).min(-1)[:, :, None]
    qmax = positions.reshape(batch, seq // tq, tq).max(-1)[:, :, None]
    kmin = positions.reshape(batch, seq // tk, tk).min(-1)[:, None, :]
    kmax = positions.reshape(batch, seq // tk, tk).max(-1)[:, None, :]
    full = (kmax < qmin) if strict else (kmax <= qmin)
    some = (kmin < qmax) if strict else (kmin <= qmax)
    n_full = jnp.sum(full, axis=-1).astype(jnp.int32).reshape(-1)
    n_vis = jnp.sum(some, axis=-1).astype(jnp.int32).reshape(-1)
    return n_full, jnp.maximum(n_vis, n_full)


def _in_proj_col_scale():
    scale = np.ones((1, IN_COLS), np.float32)
    scale[:, DIFF_Q_BLK * HEAD_COLS:(DIFF_Q_BLK + DIFF_HEADS) * HEAD_COLS] = DIFF_QK_DIM ** -0.5 * LOG2E
    scale[:, SB_Q_BLK * HEAD_COLS:(SB_Q_BLK + SB_HEADS) * HEAD_COLS] = SB_DIM ** -0.5 * LOG2E
    return jnp.asarray(scale)


def kernel(x, mem, positions, norm_mix, w_in, lam_q1, lam_k1, lam_q2, lam_k2, subln_diff, subln_sb, w_out,
           norm_x, norm_mem, wq_x, wkv_x, wo_x, norm_mlp, w_up, w_down, norm_final):
    batch, seq, d = x.shape
    depth = w_in.shape[0]
    tokens = batch * seq
    assert d == D_MODEL and w_in.shape[2] == IN_COLS
    assert seq % DIFF_TQ == 0 and seq % DIFF_TK == 0 and seq % SB_TQ == 0 and seq % SB_TK == 0

    pos_f = positions.astype(_F32)
    pos_row = pos_f.reshape(batch, seq)
    pos_rep = jnp.broadcast_to(pos_f.reshape(tokens, 1), (tokens, LANES))
    diff_counts = _visible_block_counts(positions, DIFF_TQ, DIFF_TK, strict=False)
    sb_counts = _visible_block_counts(positions, SB_TQ, SB_TK, strict=True)
    in_scale = _in_proj_col_scale()

    h = x.reshape(tokens, d)
    mem2 = mem.reshape(batch * N_MEM, d)
    u = _rmsnorm_bf16(h, norm_mix[0], tm=512)
    out = None
    for l in range(depth):
        lambda_init = 0.8 - 0.6 * math.exp(-0.3 * l)
        qkv = _proj(u, w_in, l, in_scale, tm=1024, tn=1024, name="in_proj")
        o_diff, layer_bf16 = _diff_attention(qkv, pos_row, pos_rep, *diff_counts,
                                             (lam_q1[l], lam_k1[l], lam_q2[l], lam_k2[l]), subln_diff[l],
                                             lambda_init, batch, seq, l,
                                             [(w_out, None), (wq_x, None), (wo_x, None), (w_up, MLP_TF), (w_down, None)])
        w_out_bf, wq_x_bf, wo_x_bf, w_up_tiled, w_down_bf = layer_bf16
        o_sb = _sb_attention(qkv, pos_row, pos_rep, *sb_counts, subln_sb[l], batch, seq)
        mem_n = _rmsnorm_bf16(mem2, norm_mem[l], tm=256)
        kv = _proj(mem_n, wkv_x, l, None, tm=1024, tn=1024, name="xattn_kv_proj")
        h, um = _mix_xattn_block(o_diff, o_sb, w_out_bf, norm_x[l], wq_x_bf, kv, wo_x_bf, h, norm_mlp[l], tm=256)

        last = l == depth - 1
        g_next = norm_final if last else norm_mix[l + 1]
        h, y = _mlp(um, w_up_tiled, w_down_bf, h, g_next, _F32 if last else _BF16, tm=512)
        if last:
            out = y
        else:
            u = y
    return out.reshape(batch, seq, d)
```

```python
import functools
import math

import jax
import jax.numpy as jnp
import numpy as np
from jax import lax
from jax.experimental import pallas as pl
from jax.experimental.pallas import tpu as pltpu

D_MODEL = 2048
N_MEM = 256
DIFF_V_DIM = 128
DIFF_QK_DIM = 64
DIFF_HEADS = 8
SB_DIM = 128
SB_HEADS = 8
X_HEADS = 4
X_DIM = D_MODEL // X_HEADS
D_FF = 4 * D_MODEL
EPS = 1e-6
NEG_INF = -1e30
DEAD_CARRY = -160.0
LOG2E = math.log2(math.e)

LANES = 128
ONES_ROWS = 16
HEAD_COLS = 128
DIFF_Q_BLK = 0
DIFF_K_BLK = DIFF_HEADS
DIFF_V_BLK = 2 * DIFF_HEADS
SB_Q_BLK = 3 * DIFF_HEADS
SB_K_BLK = SB_Q_BLK + SB_HEADS
SB_V_BLK = SB_K_BLK + SB_HEADS
IN_COLS = (3 * DIFF_HEADS + 3 * SB_HEADS) * HEAD_COLS

DIFF_TQ, DIFF_TK = 512, 512
SB_TQ, SB_TK = 256, 256
MLP_TF = 1024
DIFF_HEADS_PER_STEP = 4
SB_HEADS_PER_STEP = 8
VMEM_LIMIT = 56 * 1024 * 1024

_BF16 = jnp.bfloat16
_F32 = jnp.float32
_NT = (((1,), (1,)), ((), ()))
_TN = (((0,), (0,)), ((), ()))


def _params(semantics):
    return pltpu.CompilerParams(dimension_semantics=semantics, vmem_limit_bytes=VMEM_LIMIT)


def _rms(x, g):
    return x * lax.rsqrt(jnp.mean(x * x, axis=-1, keepdims=True) + EPS) * g


def _lane_tiles(x):
    return [x[:, c * LANES:(c + 1) * LANES] for c in range(x.shape[1] // LANES)]


def _lane_tile_repeat(tile, width):
    return jnp.concatenate([tile] * (width // LANES), axis=1)


def _lane_replicated_column(vec):
    return jnp.broadcast_to(vec.astype(_F32).reshape(-1, 1), (vec.shape[0], LANES))


def _rms_over_rows(x_t):
    return x_t * lax.rsqrt(jnp.mean(x_t * x_t, axis=0, keepdims=True) + EPS)


def _software_pipeline(stages, n_chains):
    for t in range(n_chains + len(stages) - 1):
        for s, stage in enumerate(stages):
            if 0 <= t - s < n_chains:
                stage(t - s)


def _rmsnorm_kernel(x_ref, g_ref, o_ref):
    o_ref[...] = _rms(x_ref[...], g_ref[...]).astype(o_ref.dtype)


def _rmsnorm_bf16(x, g, tm):
    m, d = x.shape
    return pl.pallas_call(
        _rmsnorm_kernel,
        grid=(m // tm,),
        in_specs=[pl.BlockSpec((tm, d), lambda i: (i, 0)), pl.BlockSpec((1, d), lambda i: (0, 0))],
        out_specs=pl.BlockSpec((tm, d), lambda i: (i, 0)),
        out_shape=jax.ShapeDtypeStruct((m, d), _BF16),
        compiler_params=_params(("parallel",)),
        name="rmsnorm_bf16",
    )(x, g.reshape(1, d))


def _proj_kernel(x_ref, w_ref, *rest):
    o_ref = rest[-1]
    acc = jnp.dot(x_ref[...], w_ref[...].astype(_BF16), preferred_element_type=_F32)
    if len(rest) == 2:
        acc = acc * rest[0][...]
    o_ref[...] = acc.astype(o_ref.dtype)


def _proj(x, w_stack, layer, col_scale, tm, tn, name):
    m, k = x.shape
    n = w_stack.shape[2]
    in_specs = [pl.BlockSpec((tm, k), lambda j, i: (i, 0)), pl.BlockSpec((None, k, tn), lambda j, i: (layer, 0, j))]
    args = [x, w_stack]
    if col_scale is not None:
        in_specs.append(pl.BlockSpec((1, tn), lambda j, i: (0, j)))
        args.append(col_scale)
    return pl.pallas_call(
        _proj_kernel,
        grid=(n // tn, m // tm),
        in_specs=in_specs,
        out_specs=pl.BlockSpec((tm, tn), lambda j, i: (i, j)),
        out_shape=jax.ShapeDtypeStruct((m, n), _BF16),
        compiler_params=_params(("parallel", "arbitrary")),
        name=name,
    )(*args)


def _mlp_kernel(u_ref, wu_ref, wd_ref, h_ref, g_ref, h_out_ref, y_out_ref):
    f = pl.program_id(1)

    @pl.when(f == 0)
    def _():
        h_out_ref[...] = h_ref[...]

    up = jnp.dot(u_ref[...], wu_ref[...], preferred_element_type=_F32)
    a = jnp.square(jnp.maximum(up, 0.0)).astype(_BF16)
    h_out_ref[...] += jnp.dot(a, wd_ref[...], preferred_element_type=_F32)

    @pl.when(f == pl.num_programs(1) - 1)
    def _():
        y_out_ref[...] = _rms(h_out_ref[...], g_ref[...]).astype(y_out_ref.dtype)


def _mlp(u, w_up_tiled, w_down, h, g, y_dtype, tm):
    m, d = h.shape
    n_f, _, tf = w_up_tiled.shape
    row_tile = pl.BlockSpec((tm, d), lambda i, f: (i, 0))
    pipeline = pltpu.emit_pipeline(
        _mlp_kernel,
        grid=(m // tm, n_f),
        in_specs=[
            row_tile,
            pl.BlockSpec((None, d, tf), lambda i, f: (f, 0, 0), pipeline_mode=pl.Buffered(3)),
            pl.BlockSpec((tf, d), lambda i, f: (f, 0), pipeline_mode=pl.Buffered(3)),
            row_tile,
            pl.BlockSpec((1, d), lambda i, f: (0, 0)),
        ],
        out_specs=[row_tile, row_tile],
    )
    hbm = pl.BlockSpec(memory_space=pl.ANY)
    return pl.pallas_call(
        lambda *refs: pipeline(*refs),
        in_specs=[hbm] * 5,
        out_specs=[hbm] * 2,
        out_shape=[jax.ShapeDtypeStruct((m, d), _F32), jax.ShapeDtypeStruct((m, d), y_dtype)],
        compiler_params=pltpu.CompilerParams(vmem_limit_bytes=VMEM_LIMIT),
        name="mlp_relu2",
    )(u, w_up_tiled, w_down, h, g.reshape(1, d))


def _diff_attn_kernel(nfull_ref, nvis_ref, slopes_ref,
                      q_ref, k_ref, v_ref, pq_ref, pk_ref, lq1_ref, lk1_ref, lq2_ref, lk2_ref, g_ref,
                      *rest, lambda_init, tq, tk, hp, n_cast):
    cast_in, (o_ref, *cast_out), (m_s, acc_s, s_s) = rest[:n_cast], rest[n_cast:2 * n_cast + 1], rest[2 * n_cast + 1:]
    for w_ref, w_bf_ref in zip(cast_in, cast_out):
        w_bf_ref[...] = w_ref[...].astype(w_bf_ref.dtype)

    b, hg, i = pl.program_id(0), pl.program_id(1), pl.program_id(2)
    nq = pl.num_programs(2)
    n_full = nfull_ref[b * nq + i]
    n_vis = nvis_ref[b * nq + i]
    n_lane_tiles = tq // LANES
    heads = [slice(hh * HEAD_COLS, (hh + 1) * HEAD_COLS) for hh in range(hp)]

    lane = lax.broadcasted_iota(jnp.int32, (tq, HEAD_COLS), 1)
    zero = jnp.zeros((tq, HEAD_COLS), _BF16)
    q_maps = []
    for cols in heads:
        q = q_ref[:, cols]
        q_maps += [jnp.where(lane < DIFF_QK_DIM, q, zero), jnp.where(lane >= DIFF_QK_DIM, q, zero)]
    slope2 = [slopes_ref[hg * hp + hh] * LOG2E for hh in range(hp)]
    pq = pq_ref[0]
    pq0 = pq[:, 0:1]

    m_s[...] = jnp.full(m_s.shape, NEG_INF, _F32)
    acc_s[...] = jnp.zeros(acc_s.shape, _F32)
    ones_rows = jnp.ones((ONES_ROWS, tk), _BF16)

    n_chains = 2 * hp

    def softmax_pv(ci, j):
        v = v_ref[pl.ds(pl.multiple_of(j * tk, tk), tk), heads[ci // 2]]
        v_t = jnp.concatenate([v.T, ones_rows], axis=0)
        s = s_s[ci]
        m_prev = m_s[ci]
        m_new = jnp.maximum(m_prev, jnp.max(s, axis=0, keepdims=True))
        alpha = jnp.exp2(m_prev - m_new)
        p = jnp.exp2(s - m_new).astype(_BF16)
        acc_s[ci] = alpha * acc_s[ci] + jnp.dot(v_t, p, preferred_element_type=_F32)
        m_s[ci] = m_new

    def step(j, masked, first):
        start = pl.multiple_of(j * tk, tk)
        pk = pk_ref[pl.ds(start, tk), :]
        rel = pk - pq0
        bias = [jnp.concatenate([slope2[hh] * rel] * n_lane_tiles, axis=1) for hh in range(hp)]
        if masked:
            penalty = jnp.concatenate([jnp.where(pq_c >= pk, 0.0, NEG_INF) for pq_c in _lane_tiles(pq)], axis=1)
            bias = [b_h + penalty for b_h in bias]

        def scores(ci):
            k = k_ref[pl.ds(start, tk), heads[ci // 2]]
            s_s[ci] = lax.dot_general(k, q_maps[ci], _NT, preferred_element_type=_F32) + bias[ci // 2]

        scores(0)
        if not first:
            softmax_pv(n_chains - 1, j - 1)
        for ci in range(n_chains - 1):
            scores(ci + 1)
            softmax_pv(ci, j)

    @pl.when(n_full > 0)
    def _():
        step(0, masked=False, first=True)

    @pl.when(n_full == 0)
    def _():
        step(0, masked=True, first=True)

    def full_body(j, c):
        step(j, masked=False, first=False)
        return c

    def masked_body(j, c):
        step(j, masked=True, first=False)
        return c

    lax.fori_loop(1, n_full, full_body, 0)
    lax.fori_loop(jnp.maximum(n_full, 1), n_vis, masked_body, 0)
    softmax_pv(n_chains - 1, n_vis - 1)

    lam = (jnp.exp(jnp.sum(lq1_ref[...] * lk1_ref[...], axis=-1, keepdims=True))
           - jnp.exp(jnp.sum(lq2_ref[...] * lk2_ref[...], axis=-1, keepdims=True)) + lambda_init)
    gain = _lane_tile_repeat(g_ref[...] * (1.0 - lambda_init), tq)
    dv = DIFF_V_DIM
    for hh, cols in enumerate(heads):
        acc1, acc2 = acc_s[2 * hh], acc_s[2 * hh + 1]
        o_t = acc1[:dv] / acc1[dv:dv + 1] - lam * (acc2[:dv] / acc2[dv:dv + 1])
        o_ref[cols, :] = (_rms_over_rows(o_t) * gain).astype(o_ref.dtype)


def _diff_attention(qkv, pos_row, pos_rep, n_full, n_vis, lam_params, g, lambda_init, batch, seq, layer, to_bf16):
    tq, tk, hp = DIFF_TQ, DIFF_TK, DIFF_HEADS_PER_STEP
    nq = seq // tq
    n_hg = DIFF_HEADS // hp
    n_steps = batch * n_hg * nq
    slopes = jnp.asarray(np.array([2.0 ** (-8.0 * (i + 1) / DIFF_HEADS) for i in range(DIFF_HEADS)], np.float32))
    small = lambda b, h, i, *_: (0, 0)
    step_id = lambda b, h, i: (b * n_hg + h) * nq + i
    cast_in_specs, cast_out_specs, cast_out_shapes = [], [], []
    for w, col_tile in to_bf16:
        k_dim, n_dim = w.shape[1:]
        if col_tile is None:
            rows = k_dim // n_steps
            assert rows * n_steps == k_dim and rows % 16 == 0
            cast_in_specs.append(pl.BlockSpec((None, rows, n_dim), lambda b, h, i, *_: (layer, step_id(b, h, i), 0)))
            cast_out_specs.append(pl.BlockSpec((rows, n_dim), lambda b, h, i, *_: (step_id(b, h, i), 0)))
            cast_out_shapes.append(jax.ShapeDtypeStruct((k_dim, n_dim), _BF16))
        else:
            n_tiles = n_dim // col_tile
            per_tile = n_steps // n_tiles
            rows = k_dim // per_tile
            assert n_tiles * col_tile == n_dim and per_tile * n_tiles == n_steps and rows * per_tile == k_dim
            cast_in_specs.append(pl.BlockSpec(
                (None, rows, col_tile),
                lambda b, h, i, *_, p=per_tile: (layer, step_id(b, h, i) % p, step_id(b, h, i) // p)))
            cast_out_specs.append(pl.BlockSpec(
                (None, rows, col_tile),
                lambda b, h, i, *_, p=per_tile: (step_id(b, h, i) // p, step_id(b, h, i) % p, 0)))
            cast_out_shapes.append(jax.ShapeDtypeStruct((n_tiles, k_dim, col_tile), _BF16))
    grid_spec = pltpu.PrefetchScalarGridSpec(
        num_scalar_prefetch=3,
        grid=(batch, n_hg, nq),
        in_specs=[
            pl.BlockSpec((tq, hp * HEAD_COLS), lambda b, h, i, *_: (b * nq + i, DIFF_Q_BLK // hp + h)),
            pl.BlockSpec((seq, hp * HEAD_COLS), lambda b, h, i, *_: (b, DIFF_K_BLK // hp + h)),
            pl.BlockSpec((seq, hp * HEAD_COLS), lambda b, h, i, *_: (b, DIFF_V_BLK // hp + h)),
            pl.BlockSpec((1, 1, tq), lambda b, h, i, *_: (b * nq + i, 0, 0)),
            pl.BlockSpec((seq, LANES), lambda b, h, i, *_: (b, 0)),
            pl.BlockSpec((1, DIFF_QK_DIM), small), pl.BlockSpec((1, DIFF_QK_DIM), small),
            pl.BlockSpec((1, DIFF_QK_DIM), small), pl.BlockSpec((1, DIFF_QK_DIM), small),
            pl.BlockSpec((DIFF_V_DIM, LANES), small),
        ] + cast_in_specs,
        out_specs=[pl.BlockSpec((None, hp * HEAD_COLS, tq), lambda b, h, i, *_: (b, h, i))] + cast_out_specs,
        scratch_shapes=[pltpu.VMEM((2 * hp, 1, tq), _F32),
                        pltpu.VMEM((2 * hp, DIFF_V_DIM + ONES_ROWS, tq), _F32), pltpu.VMEM((2 * hp, tk, tq), _F32)],
    )
    outs = pl.pallas_call(
        functools.partial(_diff_attn_kernel, lambda_init=lambda_init, tq=tq, tk=tk, hp=hp, n_cast=len(to_bf16)),
        grid_spec=grid_spec,
        out_shape=[jax.ShapeDtypeStruct((batch, DIFF_HEADS * DIFF_V_DIM, seq), _BF16)] + cast_out_shapes,
        compiler_params=_params(("parallel", "parallel", "arbitrary")),
        name="diff_attention",
    )(n_full, n_vis, slopes, qkv, qkv, qkv, pos_row.reshape(batch * nq, 1, tq), pos_rep,
      *[p.reshape(1, DIFF_QK_DIM) for p in lam_params], _lane_replicated_column(g), *[w for w, _ in to_bf16])
    return outs[0], outs[1:]


def _sb_attn_kernel(nfull_ref, nvis_ref, q_ref, k_ref, v_ref, pq_ref, pk_ref, ntri_ref, g_ref,
                    o_ref, c_s, w_s, acc_s, z_s, e_s, *, tq, tk, hp):
    b, i = pl.program_id(0), pl.program_id(2)
    nq = pl.num_programs(2)
    n_full = nfull_ref[b * nq + i]
    n_vis = nvis_ref[b * nq + i]
    heads = [slice(hh * HEAD_COLS, (hh + 1) * HEAD_COLS) for hh in range(hp)]
    qs = [q_ref[:, cols] for cols in heads]
    pq = pq_ref[0]
    ntri = ntri_ref[...]

    c_s[...] = jnp.zeros(c_s.shape, _F32)
    acc_s[...] = jnp.zeros(acc_s.shape, _F32)

    def step(blocks):
        starts, keeps, penalties = [], [], []
        for j, masked in blocks:
            start = pl.multiple_of(j * tk, tk)
            starts.append(start)
            if masked:
                pk = pk_ref[pl.ds(start, tk), :]
                strict = [pq_c > pk for pq_c in _lane_tiles(pq)]
                keeps.append(jnp.concatenate([jnp.where(m, 1.0, 0.0) for m in strict], axis=1))
                penalties.append(jnp.concatenate([jnp.where(m, 0.0, NEG_INF) for m in strict], axis=1))
            else:
                keeps.append(None)
                penalties.append(None)

        def logits(c):
            blk, hh = divmod(c, hp)
            k = k_ref[pl.ds(starts[blk], tk), heads[hh]]
            z_s[c] = lax.dot_general(k, qs[hh], _NT, preferred_element_type=_F32)

        def log_weights(c):
            blk, hh = divmod(c, hp)
            z = z_s[c]
            softplus = jnp.maximum(z, 0.0) + jnp.log2(1.0 + jnp.exp2(-jnp.abs(z)))
            log_sig = z - softplus
            if keeps[blk] is not None:
                softplus = softplus * keeps[blk]
            terms = softplus.astype(_BF16)
            later = jnp.dot(ntri, terms, preferred_element_type=_F32)
            e = log_sig + later
            if penalties[blk] is not None:
                e = e + penalties[blk]
            e_s[c] = e
            carry = c_s[hh]
            w_s[c] = jnp.exp2(carry)
            c_s[hh] = carry + (later[0:1, :] - terms[0:1, :].astype(_F32))

        def weighted_values(c):
            blk, hh = divmod(c, hp)
            v = v_ref[pl.ds(starts[blk], tk), heads[hh]]
            a = jnp.exp2(e_s[c]).astype(_BF16)
            acc_s[hh] += w_s[c] * lax.dot_general(v, a, _TN, preferred_element_type=_F32)

        _software_pipeline((logits, log_weights, weighted_values), hp * len(blocks))

    n_masked = n_vis - n_full

    def masked_body(t, c):
        step([(n_vis - 1 - t, True)])
        return c

    lax.fori_loop(0, n_masked - 1, masked_body, 0)
    fuse = jnp.logical_and(n_masked > 0, n_full > 0)

    @pl.when(fuse)
    def _():
        step([(n_full, True), (n_full - 1, False)])

    @pl.when(jnp.logical_and(n_masked > 0, n_full == 0))
    def _():
        step([(n_full, True)])

    def any_weight_left():
        return jnp.max(c_s[...]) > DEAD_CARRY

    def full_cond(state):
        t, go = state
        return jnp.logical_and(t < n_full, go)

    def full_body(state):
        t, _ = state
        step([(n_full - 1 - t, False)])
        return t + 1, any_weight_left()

    lax.while_loop(full_cond, full_body, (fuse.astype(jnp.int32), any_weight_left()))
    gain = _lane_tile_repeat(g_ref[...], tq)
    for hh, cols in enumerate(heads):
        o_ref[cols, :] = (_rms_over_rows(acc_s[hh]) * gain).astype(o_ref.dtype)


def _sb_attention(qkv, pos_row, pos_rep, n_full, n_vis, g, batch, seq):
    tq, tk, hp = SB_TQ, SB_TK, SB_HEADS_PER_STEP
    nq = seq // tq
    idx = np.arange(tk)
    ntri = jnp.asarray(-(idx[None, :] > idx[:, None]).astype(np.float32), _BF16)
    grid_spec = pltpu.PrefetchScalarGridSpec(
        num_scalar_prefetch=2,
        grid=(batch, SB_HEADS // hp, nq),
        in_specs=[
            pl.BlockSpec((tq, hp * HEAD_COLS), lambda b, h, i, *_: (b * nq + i, SB_Q_BLK // hp + h)),
            pl.BlockSpec((seq, hp * HEAD_COLS), lambda b, h, i, *_: (b, SB_K_BLK // hp + h)),
            pl.BlockSpec((seq, hp * HEAD_COLS), lambda b, h, i, *_: (b, SB_V_BLK // hp + h)),
            pl.BlockSpec((1, 1, tq), lambda b, h, i, *_: (b * nq + i, 0, 0)),
            pl.BlockSpec((seq, LANES), lambda b, h, i, *_: (b, 0)),
            pl.BlockSpec((tk, tk), lambda b, h, i, *_: (0, 0)),
            pl.BlockSpec((SB_DIM, LANES), lambda b, h, i, *_: (0, 0)),
        ],
        out_specs=pl.BlockSpec((None, hp * HEAD_COLS, tq), lambda b, h, i, *_: (b, h, i)),
        scratch_shapes=[pltpu.VMEM((hp, 1, tq), _F32), pltpu.VMEM((2 * hp, 1, tq), _F32),
                        pltpu.VMEM((hp, SB_DIM, tq), _F32),
                        pltpu.VMEM((2 * hp, tk, tq), _F32), pltpu.VMEM((2 * hp, tk, tq), _F32)],
    )
    return pl.pallas_call(
        functools.partial(_sb_attn_kernel, tq=tq, tk=tk, hp=hp),
        grid_spec=grid_spec,
        out_shape=jax.ShapeDtypeStruct((batch, SB_HEADS * SB_DIM, seq), _BF16),
        compiler_params=_params(("parallel", "parallel", "arbitrary")),
        name="stick_breaking_attention",
    )(n_full, n_vis, qkv, qkv, qkv, pos_row.reshape(batch * nq, 1, tq), pos_rep, ntri, _lane_replicated_column(g))


def _mix_xattn_block_kernel(od_ref, os_ref, wod_ref, wos_ref, gx_ref, wq_ref, k_ref, v_ref, wo_ref, h_ref, g_ref,
                            h_out_ref, u_out_ref):
    h1 = (h_ref[...] + lax.dot_general(od_ref[...], wod_ref[...], _TN, preferred_element_type=_F32)
          + lax.dot_general(os_ref[...], wos_ref[...], _TN, preferred_element_type=_F32))
    u = _rms(h1, gx_ref[...]).astype(_BF16)
    q = jnp.dot(u, wq_ref[...], preferred_element_type=_F32) * (X_DIM ** -0.5 * LOG2E)
    q = q.astype(_BF16)
    heads_out = []
    for hd in range(X_HEADS):
        cols = slice(hd * X_DIM, (hd + 1) * X_DIM)
        s = lax.dot_general(q[:, cols], k_ref[:, cols], _NT, preferred_element_type=_F32)
        p = jnp.exp2(s - jnp.max(s, axis=-1, keepdims=True))
        o = jnp.dot(p.astype(_BF16), v_ref[:, cols], preferred_element_type=_F32)
        heads_out.append((o / jnp.sum(p, axis=-1, keepdims=True)).astype(_BF16))
    o_all = jnp.concatenate(heads_out, axis=1)
    acc = h1 + jnp.dot(o_all, wo_ref[...], preferred_element_type=_F32)
    h_out_ref[...] = acc
    u_out_ref[...] = _rms(acc, g_ref[...]).astype(u_out_ref.dtype)


def _mix_xattn_block(o_diff, o_sb, w_out, gx, wq, kv, wo, h, g, tm):
    m, d = h.shape
    half, seq = o_diff.shape[1:]
    tiles = seq // tm
    once = pl.Buffered(1)
    row_tile = pl.BlockSpec((tm, d), lambda i: (i, 0))
    vec = pl.BlockSpec((1, d), lambda i: (0, 0))
    return pl.pallas_call(
        _mix_xattn_block_kernel,
        grid=(m // tm,),
        in_specs=[
            pl.BlockSpec((None, half, tm), lambda i: (i // tiles, 0, i % tiles)),
            pl.BlockSpec((None, half, tm), lambda i: (i // tiles, 0, i % tiles)),
            pl.BlockSpec((half, d), lambda i: (0, 0), pipeline_mode=once),
            pl.BlockSpec((half, d), lambda i: (1, 0), pipeline_mode=once),
            vec,
            pl.BlockSpec((d, d), lambda i: (0, 0), pipeline_mode=once),
            pl.BlockSpec((N_MEM, d), lambda i: (i // tiles, 0)),
            pl.BlockSpec((N_MEM, d), lambda i: (i // tiles, 1)),
            pl.BlockSpec((d, d), lambda i: (0, 0), pipeline_mode=once),
            row_tile,
            vec,
        ],
        out_specs=[row_tile, row_tile],
        out_shape=[jax.ShapeDtypeStruct((m, d), _F32), jax.ShapeDtypeStruct((m, d), _BF16)],
        compiler_params=_params(("parallel",)),
        name="mix_xattn_block",
    )(o_diff, o_sb, w_out, w_out, gx.reshape(1, d), wq, kv, kv, wo, h, g.reshape(1, d))


def _visible_block_counts(positions, tq, tk, strict):
    batch, seq = positions.shape
    qmin = positions.reshape(batch, seq // tq, tq).min(-1)[:, :, None]
    qmax = positions.reshape(batch, seq // tq, tq).max(-1)[:, :, None]
    kmin = positions.reshape(batch, seq // tk, tk).min(-1)[:, None, :]
    kmax = positions.reshape(batch, seq // tk, tk).max(-1)[:, None, :]
    full = (kmax < qmin) if strict else (kmax <= qmin)
    some = (kmin < qmax) if strict else (kmin <= qmax)
    n_full = jnp.sum(full, axis=-1).astype(jnp.int32).reshape(-1)
    n_vis = jnp.sum(some, axis=-1).astype(jnp.int32).reshape(-1)
    return n_full, jnp.maximum(n_vis, n_full)


def _in_proj_col_scale():
    scale = np.ones((1, IN_COLS), np.float32)
    scale[:, DIFF_Q_BLK * HEAD_COLS:(DIFF_Q_BLK + DIFF_HEADS) * HEAD_COLS] = DIFF_QK_DIM ** -0.5 * LOG2E
    scale[:, SB_Q_BLK * HEAD_COLS:(SB_Q_BLK + SB_HEADS) * HEAD_COLS] = SB_DIM ** -0.5 * LOG2E
    return jnp.asarray(scale)


def kernel(x, mem, positions, norm_mix, w_in, lam_q1, lam_k1, lam_q2, lam_k2, subln_diff, subln_sb, w_out,
           norm_x, norm_mem, wq_x, wkv_x, wo_x, norm_mlp, w_up, w_down, norm_final):
    batch, seq, d = x.shape
    depth = w_in.shape[0]
    tokens = batch * seq
    assert d == D_MODEL and w_in.shape[2] == IN_COLS
    assert seq % DIFF_TQ == 0 and seq % DIFF_TK == 0 and seq % SB_TQ == 0 and seq % SB_TK == 0

    pos_f = positions.astype(_F32)
    pos_row = pos_f.reshape(batch, seq)
    pos_rep = jnp.broadcast_to(pos_f.reshape(tokens, 1), (tokens, LANES))
    diff_counts = _visible_block_counts(positions, DIFF_TQ, DIFF_TK, strict=False)
    sb_counts = _visible_block_counts(positions, SB_TQ, SB_TK, strict=True)
    in_scale = _in_proj_col_scale()

    h = x.reshape(tokens, d)
    mem2 = mem.reshape(batch * N_MEM, d)
    u = _rmsnorm_bf16(h, norm_mix[0], tm=512)
    out = None
    for l in range(depth):
        lambda_init = 0.8 - 0.6 * math.exp(-0.3 * l)
        qkv = _proj(u, w_in, l, in_scale, tm=1024, tn=1024, name="in_proj")
        o_diff, layer_bf16 = _diff_attention(qkv, pos_row, pos_rep, *diff_counts,
                                             (lam_q1[l], lam_k1[l], lam_q2[l], lam_k2[l]), subln_diff[l],
                                             lambda_init, batch, seq, l,
                                             [(w_out, None), (wq_x, None), (wo_x, None), (w_up, MLP_TF), (w_down, None)])
        w_out_bf, wq_x_bf, wo_x_bf, w_up_tiled, w_down_bf = layer_bf16
        o_sb = _sb_attention(qkv, pos_row, pos_rep, *sb_counts, subln_sb[l], batch, seq)
        mem_n = _rmsnorm_bf16(mem2, norm_mem[l], tm=256)
        kv = _proj(mem_n, wkv_x, l, None, tm=1024, tn=1024, name="xattn_kv_proj")
        h, um = _mix_xattn_block(o_diff, o_sb, w_out_bf, norm_x[l], wq_x_bf, kv, wo_x_bf, h, norm_mlp[l], tm=256)

        last = l == depth - 1
        g_next = norm_final if last else norm_mix[l + 1]
        h, y = _mlp(um, w_up_tiled, w_down_bf, h, g_next, _F32 if last else _BF16, tm=512)
        if last:
            out = y
        else:
            u = y
    return out.reshape(batch, seq, d)
```

```python
import functools
import math

import jax
import jax.numpy as jnp
import numpy as np
from jax import lax
from jax.experimental import pallas as pl
from jax.experimental.pallas import tpu as pltpu

D_MODEL = 2048
N_MEM = 256
DIFF_V_DIM = 128
DIFF_QK_DIM = 64
DIFF_HEADS = 8
SB_DIM = 128
SB_HEADS = 8
X_HEADS = 4
X_DIM = D_MODEL // X_HEADS
D_FF = 4 * D_MODEL
EPS = 1e-6
NEG_INF = -1e30
DEAD_CARRY = -160.0
LOG2E = math.log2(math.e)

LANES = 128
ONES_ROWS = 16
HEAD_COLS = 128
DIFF_Q_BLK = 0
DIFF_K_BLK = DIFF_HEADS
DIFF_V_BLK = 2 * DIFF_HEADS
SB_Q_BLK = 3 * DIFF_HEADS
SB_K_BLK = SB_Q_BLK + SB_HEADS
SB_V_BLK = SB_K_BLK + SB_HEADS
IN_COLS = (3 * DIFF_HEADS + 3 * SB_HEADS) * HEAD_COLS

DIFF_TQ, DIFF_TK = 512, 512
SB_TQ, SB_TK = 256, 256
MLP_TF = 1024
DIFF_HEADS_PER_STEP = 4
SB_HEADS_PER_STEP = 8
VMEM_LIMIT = 56 * 1024 * 1024

_BF16 = jnp.bfloat16
_F32 = jnp.float32
_NT = (((1,), (1,)), ((), ()))
_TN = (((0,), (0,)), ((), ()))


def _params(semantics):
    return pltpu.CompilerParams(dimension_semantics=semantics, vmem_limit_bytes=VMEM_LIMIT)


def _rms(x, g):
    return x * lax.rsqrt(jnp.mean(x * x, axis=-1, keepdims=True) + EPS) * g


def _lane_tiles(x):
    return [x[:, c * LANES:(c + 1) * LANES] for c in range(x.shape[1] // LANES)]


def _lane_tile_repeat(tile, width):
    return jnp.concatenate([tile] * (width // LANES), axis=1)


def _lane_replicated_column(vec):
    return jnp.broadcast_to(vec.astype(_F32).reshape(-1, 1), (vec.shape[0], LANES))


def _rms_over_rows(x_t):
    return x_t * lax.rsqrt(jnp.mean(x_t * x_t, axis=0, keepdims=True) + EPS)


def _software_pipeline(stages, n_chains):
    for t in range(n_chains + len(stages) - 1):
        for s, stage in enumerate(stages):
            if 0 <= t - s < n_chains:
                stage(t - s)


def _rmsnorm_kernel(x_ref, g_ref, o_ref):
    o_ref[...] = _rms(x_ref[...], g_ref[...]).astype(o_ref.dtype)


def _rmsnorm_bf16(x, g, tm):
    m, d = x.shape
    return pl.pallas_call(
        _rmsnorm_kernel,
        grid=(m // tm,),
        in_specs=[pl.BlockSpec((tm, d), lambda i: (i, 0)), pl.BlockSpec((1, d), lambda i: (0, 0))],
        out_specs=pl.BlockSpec((tm, d), lambda i: (i, 0)),
        out_shape=jax.ShapeDtypeStruct((m, d), _BF16),
        compiler_params=_params(("parallel",)),
        name="rmsnorm_bf16",
    )(x, g.reshape(1, d))


def _proj_kernel(x_ref, w_ref, *rest):
    o_ref = rest[-1]
    acc = jnp.dot(x_ref[...], w_ref[...].astype(_BF16), preferred_element_type=_F32)
    if len(rest) == 2:
        acc = acc * rest[0][...]
    o_ref[...] = acc.astype(o_ref.dtype)


def _proj(x, w_stack, layer, col_scale, tm, tn, name):
    m, k = x.shape
    n = w_stack.shape[2]
    in_specs = [pl.BlockSpec((tm, k), lambda j, i: (i, 0)), pl.BlockSpec((None, k, tn), lambda j, i: (layer, 0, j))]
    args = [x, w_stack]
    if col_scale is not None:
        in_specs.append(pl.BlockSpec((1, tn), lambda j, i: (0, j)))
        args.append(col_scale)
    return pl.pallas_call(
        _proj_kernel,
        grid=(n // tn, m // tm),
        in_specs=in_specs,
        out_specs=pl.BlockSpec((tm, tn), lambda j, i: (i, j)),
        out_shape=jax.ShapeDtypeStruct((m, n), _BF16),
        compiler_params=_params(("parallel", "arbitrary")),
        name=name,
    )(*args)


def _mlp_kernel(u_ref, wu_ref, wd_ref, h_ref, g_ref, h_out_ref, y_out_ref):
    f = pl.program_id(1)

    @pl.when(f == 0)
    def _():
        h_out_ref[...] = h_ref[...]

    up = jnp.dot(u_ref[...], wu_ref[...], preferred_element_type=_F32)
    a = jnp.square(jnp.maximum(up, 0.0)).astype(_BF16)
    h_out_ref[...] += jnp.dot(a, wd_ref[...], preferred_element_type=_F32)

    @pl.when(f == pl.num_programs(1) - 1)
    def _():
        y_out_ref[...] = _rms(h_out_ref[...], g_ref[...]).astype(y_out_ref.dtype)


def _mlp(u, w_up_tiled, w_down, h, g, y_dtype, tm):
    m, d = h.shape
    n_f, _, tf = w_up_tiled.shape
    row_tile = pl.BlockSpec((tm, d), lambda i, f: (i, 0))
    row_tile_in = pl.BlockSpec((tm, d), lambda i, f: (i, 0), pipeline_mode=pl.Buffered(2, use_lookahead=True))
    pipeline = pltpu.emit_pipeline(
        _mlp_kernel,
        grid=(m // tm, n_f),
        in_specs=[
            row_tile_in,
            pl.BlockSpec((None, d, tf), lambda i, f: (f, 0, 0), pipeline_mode=pl.Buffered(3)),
            pl.BlockSpec((tf, d), lambda i, f: (f, 0), pipeline_mode=pl.Buffered(3)),
            row_tile_in,
            pl.BlockSpec((1, d), lambda i, f: (0, 0)),
        ],
        out_specs=[row_tile, row_tile],
    )
    hbm = pl.BlockSpec(memory_space=pl.ANY)
    return pl.pallas_call(
        lambda *refs: pipeline(*refs),
        in_specs=[hbm] * 5,
        out_specs=[hbm] * 2,
        out_shape=[jax.ShapeDtypeStruct((m, d), _F32), jax.ShapeDtypeStruct((m, d), y_dtype)],
        compiler_params=pltpu.CompilerParams(vmem_limit_bytes=VMEM_LIMIT),
        name="mlp_relu2",
    )(u, w_up_tiled, w_down, h, g.reshape(1, d))


def _diff_attn_kernel(nfull_ref, nvis_ref, slopes_ref,
                      q_ref, k_ref, v_ref, pq_ref, pk_ref, lq1_ref, lk1_ref, lq2_ref, lk2_ref, g_ref,
                      *rest, lambda_init, tq, tk, hp, n_cast):
    cast_in, (o_ref, *cast_out), (m_s, acc_s, s_s) = rest[:n_cast], rest[n_cast:2 * n_cast + 1], rest[2 * n_cast + 1:]
    for w_ref, w_bf_ref in zip(cast_in, cast_out):
        w_bf_ref[...] = w_ref[...].astype(w_bf_ref.dtype)

    b, hg, i = pl.program_id(0), pl.program_id(1), pl.program_id(2)
    nq = pl.num_programs(2)
    n_full = nfull_ref[b * nq + i]
    n_vis = nvis_ref[b * nq + i]
    n_lane_tiles = tq // LANES
    heads = [slice(hh * HEAD_COLS, (hh + 1) * HEAD_COLS) for hh in range(hp)]

    lane = lax.broadcasted_iota(jnp.int32, (tq, HEAD_COLS), 1)
    zero = jnp.zeros((tq, HEAD_COLS), _BF16)
    q_maps = []
    for cols in heads:
        q = q_ref[:, cols]
        q_maps += [jnp.where(lane < DIFF_QK_DIM, q, zero), jnp.where(lane >= DIFF_QK_DIM, q, zero)]
    slope2 = [slopes_ref[hg * hp + hh] * LOG2E for hh in range(hp)]
    pq = pq_ref[0]
    pq0 = pq[:, 0:1]

    m_s[...] = jnp.full(m_s.shape, NEG_INF, _F32)
    acc_s[...] = jnp.zeros(acc_s.shape, _F32)
    ones_rows = jnp.ones((ONES_ROWS, tk), _BF16)

    n_chains = 2 * hp

    def softmax_pv(ci, j):
        v = v_ref[pl.ds(pl.multiple_of(j * tk, tk), tk), heads[ci // 2]]
        v_t = jnp.concatenate([v.T, ones_rows], axis=0)
        s = s_s[ci]
        m_prev = m_s[ci]
        m_new = jnp.maximum(m_prev, jnp.max(s, axis=0, keepdims=True))
        alpha = jnp.exp2(m_prev - m_new)
        p = jnp.exp2(s - m_new).astype(_BF16)
        acc_s[ci] = alpha * acc_s[ci] + jnp.dot(v_t, p, preferred_element_type=_F32)
        m_s[ci] = m_new

    def step(j, masked, first):
        start = pl.multiple_of(j * tk, tk)
        pk = pk_ref[pl.ds(start, tk), :]
        rel = pk - pq0
        bias = [jnp.concatenate([slope2[hh] * rel] * n_lane_tiles, axis=1) for hh in range(hp)]
        if masked:
            penalty = jnp.concatenate([jnp.where(pq_c >= pk, 0.0, NEG_INF) for pq_c in _lane_tiles(pq)], axis=1)
            bias = [b_h + penalty for b_h in bias]

        def scores(ci):
            k = k_ref[pl.ds(start, tk), heads[ci // 2]]
            s_s[ci] = lax.dot_general(k, q_maps[ci], _NT, preferred_element_type=_F32) + bias[ci // 2]

        scores(0)
        if not first:
            softmax_pv(n_chains - 1, j - 1)
        for ci in range(n_chains - 1):
            scores(ci + 1)
            softmax_pv(ci, j)

    @pl.when(n_full > 0)
    def _():
        step(0, masked=False, first=True)

    @pl.when(n_full == 0)
    def _():
        step(0, masked=True, first=True)

    def full_body(j, c):
        step(j, masked=False, first=False)
        return c

    def masked_body(j, c):
        step(j, masked=True, first=False)
        return c

    lax.fori_loop(1, n_full, full_body, 0)
    lax.fori_loop(jnp.maximum(n_full, 1), n_vis, masked_body, 0)
    softmax_pv(n_chains - 1, n_vis - 1)

    lam = (jnp.exp(jnp.sum(lq1_ref[...] * lk1_ref[...], axis=-1, keepdims=True))
           - jnp.exp(jnp.sum(lq2_ref[...] * lk2_ref[...], axis=-1, keepdims=True)) + lambda_init)
    gain = _lane_tile_repeat(g_ref[...] * (1.0 - lambda_init), tq)
    dv = DIFF_V_DIM
    for hh, cols in enumerate(heads):
        acc1, acc2 = acc_s[2 * hh], acc_s[2 * hh + 1]
        o_t = acc1[:dv] / acc1[dv:dv + 1] - lam * (acc2[:dv] / acc2[dv:dv + 1])
        o_ref[cols, :] = (_rms_over_rows(o_t) * gain).astype(o_ref.dtype)


def _diff_attention(qkv, pos_row, pos_rep, n_full, n_vis, lam_params, g, lambda_init, batch, seq, layer, to_bf16):
    tq, tk, hp = DIFF_TQ, DIFF_TK, DIFF_HEADS_PER_STEP
    nq = seq // tq
    n_hg = DIFF_HEADS // hp
    n_steps = batch * n_hg * nq
    slopes = jnp.asarray(np.array([2.0 ** (-8.0 * (i + 1) / DIFF_HEADS) for i in range(DIFF_HEADS)], np.float32))
    small = lambda b, h, i, *_: (0, 0)
    step_id = lambda b, h, i: (b * n_hg + h) * nq + i
    cast_in_specs, cast_out_specs, cast_out_shapes = [], [], []
    for w, col_tile in to_bf16:
        k_dim, n_dim = w.shape[1:]
        if col_tile is None:
            rows = k_dim // n_steps
            assert rows * n_steps == k_dim and rows % 16 == 0
            cast_in_specs.append(pl.BlockSpec((None, rows, n_dim), lambda b, h, i, *_: (layer, step_id(b, h, i), 0)))
            cast_out_specs.append(pl.BlockSpec((rows, n_dim), lambda b, h, i, *_: (step_id(b, h, i), 0)))
            cast_out_shapes.append(jax.ShapeDtypeStruct((k_dim, n_dim), _BF16))
        else:
            n_tiles = n_dim // col_tile
            per_tile = n_steps // n_tiles
            rows = k_dim // per_tile
            assert n_tiles * col_tile == n_dim and per_tile * n_tiles == n_steps and rows * per_tile == k_dim
            cast_in_specs.append(pl.BlockSpec(
                (None, rows, col_tile),
                lambda b, h, i, *_, p=per_tile: (layer, step_id(b, h, i) % p, step_id(b, h, i) // p)))
            cast_out_specs.append(pl.BlockSpec(
                (None, rows, col_tile),
                lambda b, h, i, *_, p=per_tile: (step_id(b, h, i) // p, step_id(b, h, i) % p, 0)))
            cast_out_shapes.append(jax.ShapeDtypeStruct((n_tiles, k_dim, col_tile), _BF16))
    grid_spec = pltpu.PrefetchScalarGridSpec(
        num_scalar_prefetch=3,
        grid=(batch, n_hg, nq),
        in_specs=[
            pl.BlockSpec((tq, hp * HEAD_COLS), lambda b, h, i, *_: (b * nq + i, DIFF_Q_BLK // hp + h)),
            pl.BlockSpec((seq, hp * HEAD_COLS), lambda b, h, i, *_: (b, DIFF_K_BLK // hp + h)),
            pl.BlockSpec((seq, hp * HEAD_COLS), lambda b, h, i, *_: (b, DIFF_V_BLK // hp + h)),
            pl.BlockSpec((1, 1, tq), lambda b, h, i, *_: (b * nq + i, 0, 0)),
            pl.BlockSpec((seq, LANES), lambda b, h, i, *_: (b, 0)),
            pl.BlockSpec((1, DIFF_QK_DIM), small), pl.BlockSpec((1, DIFF_QK_DIM), small),
            pl.BlockSpec((1, DIFF_QK_DIM), small), pl.BlockSpec((1, DIFF_QK_DIM), small),
            pl.BlockSpec((DIFF_V_DIM, LANES), small),
        ] + cast_in_specs,
        out_specs=[pl.BlockSpec((None, hp * HEAD_COLS, tq), lambda b, h, i, *_: (b, h, i))] + cast_out_specs,
        scratch_shapes=[pltpu.VMEM((2 * hp, 1, tq), _F32),
                        pltpu.VMEM((2 * hp, DIFF_V_DIM + ONES_ROWS, tq), _F32), pltpu.VMEM((2 * hp, tk, tq), _F32)],
    )
    outs = pl.pallas_call(
        functools.partial(_diff_attn_kernel, lambda_init=lambda_init, tq=tq, tk=tk, hp=hp, n_cast=len(to_bf16)),
        grid_spec=grid_spec,
        out_shape=[jax.ShapeDtypeStruct((batch, DIFF_HEADS * DIFF_V_DIM, seq), _BF16)] + cast_out_shapes,
        compiler_params=_params(("parallel", "parallel", "arbitrary")),
        name="diff_attention",
    )(n_full, n_vis, slopes, qkv, qkv, qkv, pos_row.reshape(batch * nq, 1, tq), pos_rep,
      *[p.reshape(1, DIFF_QK_DIM) for p in lam_params], _lane_replicated_column(g), *[w for w, _ in to_bf16])
    return outs[0], outs[1:]


def _sb_attn_kernel(nfull_ref, nvis_ref, q_ref, k_ref, v_ref, pq_ref, pk_ref, ntri_ref, g_ref,
                    o_ref, c_s, w_s, acc_s, z_s, e_s, *, tq, tk, hp):
    b, i = pl.program_id(0), pl.program_id(2)
    nq = pl.num_programs(2)
    n_full = nfull_ref[b * nq + i]
    n_vis = nvis_ref[b * nq + i]
    heads = [slice(hh * HEAD_COLS, (hh + 1) * HEAD_COLS) for hh in range(hp)]
    qs = [q_ref[:, cols] for cols in heads]
    pq = pq_ref[0]
    ntri = ntri_ref[...]

    c_s[...] = jnp.zeros(c_s.shape, _F32)
    acc_s[...] = jnp.zeros(acc_s.shape, _F32)

    def step(blocks):
        starts, keeps, penalties = [], [], []
        for j, masked in blocks:
            start = pl.multiple_of(j * tk, tk)
            starts.append(start)
            if masked:
                pk = pk_ref[pl.ds(start, tk), :]
                strict = [pq_c > pk for pq_c in _lane_tiles(pq)]
                keeps.append(jnp.concatenate([jnp.where(m, 1.0, 0.0) for m in strict], axis=1))
                penalties.append(jnp.concatenate([jnp.where(m, 0.0, NEG_INF) for m in strict], axis=1))
            else:
                keeps.append(None)
                penalties.append(None)

        def logits(c):
            blk, hh = divmod(c, hp)
            k = k_ref[pl.ds(starts[blk], tk), heads[hh]]
            z_s[c] = lax.dot_general(k, qs[hh], _NT, preferred_element_type=_F32)

        def log_weights(c):
            blk, hh = divmod(c, hp)
            z = z_s[c]
            softplus = jnp.maximum(z, 0.0) + jnp.log2(1.0 + jnp.exp2(-jnp.abs(z)))
            log_sig = z - softplus
            if keeps[blk] is not None:
                softplus = softplus * keeps[blk]
            terms = softplus.astype(_BF16)
            later = jnp.dot(ntri, terms, preferred_element_type=_F32)
            e = log_sig + later
            if penalties[blk] is not None:
                e = e + penalties[blk]
            e_s[c] = e
            carry = c_s[hh]
            w_s[c] = jnp.exp2(carry)
            c_s[hh] = carry + (later[0:1, :] - terms[0:1, :].astype(_F32))

        def weighted_values(c):
            blk, hh = divmod(c, hp)
            v = v_ref[pl.ds(starts[blk], tk), heads[hh]]
            a = jnp.exp2(e_s[c]).astype(_BF16)
            acc_s[hh] += w_s[c] * lax.dot_general(v, a, _TN, preferred_element_type=_F32)

        _software_pipeline((logits, log_weights, weighted_values), hp * len(blocks))

    n_masked = n_vis - n_full

    def masked_body(t, c):
        step([(n_vis - 1 - t, True)])
        return c

    lax.fori_loop(0, n_masked - 1, masked_body, 0)
    fuse = jnp.logical_and(n_masked > 0, n_full > 0)

    @pl.when(fuse)
    def _():
        step([(n_full, True), (n_full - 1, False)])

    @pl.when(jnp.logical_and(n_masked > 0, n_full == 0))
    def _():
        step([(n_full, True)])

    def any_weight_left():
        return jnp.max(c_s[...]) > DEAD_CARRY

    def full_cond(state):
        t, go = state
        return jnp.logical_and(t < n_full, go)

    def full_body(state):
        t, _ = state
        step([(n_full - 1 - t, False)])
        return t + 1, any_weight_left()

    lax.while_loop(full_cond, full_body, (fuse.astype(jnp.int32), any_weight_left()))
    gain = _lane_tile_repeat(g_ref[...], tq)
    for hh, cols in enumerate(heads):
        o_ref[cols, :] = (_rms_over_rows(acc_s[hh]) * gain).astype(o_ref.dtype)


def _sb_attention(qkv, pos_row, pos_rep, n_full, n_vis, g, batch, seq):
    tq, tk, hp = SB_TQ, SB_TK, SB_HEADS_PER_STEP
    nq = seq // tq
    idx = np.arange(tk)
    ntri = jnp.asarray(-(idx[None, :] > idx[:, None]).astype(np.float32), _BF16)
    grid_spec = pltpu.PrefetchScalarGridSpec(
        num_scalar_prefetch=2,
        grid=(batch, SB_HEADS // hp, nq),
        in_specs=[
            pl.BlockSpec((tq, hp * HEAD_COLS), lambda b, h, i, *_: (b * nq + i, SB_Q_BLK // hp + h)),
            pl.BlockSpec((seq, hp * HEAD_COLS), lambda b, h, i, *_: (b, SB_K_BLK // hp + h)),
            pl.BlockSpec((seq, hp * HEAD_COLS), lambda b, h, i, *_: (b, SB_V_BLK // hp + h)),
            pl.BlockSpec((1, 1, tq), lambda b, h, i, *_: (b * nq + i, 0, 0)),
            pl.BlockSpec((seq, LANES), lambda b, h, i, *_: (b, 0)),
            pl.BlockSpec((tk, tk), lambda b, h, i, *_: (0, 0)),
            pl.BlockSpec((SB_DIM, LANES), lambda b, h, i, *_: (0, 0)),
        ],
        out_specs=pl.BlockSpec((None, hp * HEAD_COLS, tq), lambda b, h, i, *_: (b, h, i)),
        scratch_shapes=[pltpu.VMEM((hp, 1, tq), _F32), pltpu.VMEM((2 * hp, 1, tq), _F32),
                        pltpu.VMEM((hp, SB_DIM, tq), _F32),
                        pltpu.VMEM((2 * hp, tk, tq), _F32), pltpu.VMEM((2 * hp, tk, tq), _F32)],
    )
    return pl.pallas_call(
        functools.partial(_sb_attn_kernel, tq=tq, tk=tk, hp=hp),
        grid_spec=grid_spec,
        out_shape=jax.ShapeDtypeStruct((batch, SB_HEADS * SB_DIM, seq), _BF16),
        compiler_params=_params(("parallel", "parallel", "arbitrary")),
        name="stick_breaking_attention",
    )(n_full, n_vis, qkv, qkv, qkv, pos_row.reshape(batch * nq, 1, tq), pos_rep, ntri, _lane_replicated_column(g))


def _mix_xattn_block_kernel(od_ref, os_ref, wod_ref, wos_ref, gx_ref, wq_ref, k_ref, v_ref, wo_ref, h_ref, g_ref,
                            h_out_ref, u_out_ref):
    h1 = (h_ref[...] + lax.dot_general(od_ref[...], wod_ref[...], _TN, preferred_element_type=_F32)
          + lax.dot_general(os_ref[...], wos_ref[...], _TN, preferred_element_type=_F32))
    u = _rms(h1, gx_ref[...]).astype(_BF16)
    q = jnp.dot(u, wq_ref[...], preferred_element_type=_F32) * (X_DIM ** -0.5 * LOG2E)
    q = q.astype(_BF16)
    heads_out = []
    for hd in range(X_HEADS):
        cols = slice(hd * X_DIM, (hd + 1) * X_DIM)
        s = lax.dot_general(q[:, cols], k_ref[:, cols], _NT, preferred_element_type=_F32)
        p = jnp.exp2(s - jnp.max(s, axis=-1, keepdims=True))
        o = jnp.dot(p.astype(_BF16), v_ref[:, cols], preferred_element_type=_F32)
        heads_out.append((o / jnp.sum(p, axis=-1, keepdims=True)).astype(_BF16))
    o_all = jnp.concatenate(heads_out, axis=1)
    acc = h1 + jnp.dot(o_all, wo_ref[...], preferred_element_type=_F32)
    h_out_ref[...] = acc
    u_out_ref[...] = _rms(acc, g_ref[...]).astype(u_out_ref.dtype)


def _mix_xattn_block(o_diff, o_sb, w_out, gx, wq, kv, wo, h, g, tm):
    m, d = h.shape
    half, seq = o_diff.shape[1:]
    tiles = seq // tm
    once = pl.Buffered(1)
    row_tile = pl.BlockSpec((tm, d), lambda i: (i, 0))
    vec = pl.BlockSpec((1, d), lambda i: (0, 0))
    return pl.pallas_call(
        _mix_xattn_block_kernel,
        grid=(m // tm,),
        in_specs=[
            pl.BlockSpec((None, half, tm), lambda i: (i // tiles, 0, i % tiles)),
            pl.BlockSpec((None, half, tm), lambda i: (i // tiles, 0, i % tiles)),
            pl.BlockSpec((half, d), lambda i: (0, 0), pipeline_mode=once),
            pl.BlockSpec((half, d), lambda i: (1, 0), pipeline_mode=once),
            vec,
            pl.BlockSpec((d, d), lambda i: (0, 0), pipeline_mode=once),
            pl.BlockSpec((N_MEM, d), lambda i: (i // tiles, 0)),
            pl.BlockSpec((N_MEM, d), lambda i: (i // tiles, 1)),
            pl.BlockSpec((d, d), lambda i: (0, 0), pipeline_mode=once),
            row_tile,
            vec,
        ],
        out_specs=[row_tile, row_tile],
        out_shape=[jax.ShapeDtypeStruct((m, d), _F32), jax.ShapeDtypeStruct((m, d), _BF16)],
        compiler_params=_params(("parallel",)),
        name="mix_xattn_block",
    )(o_diff, o_sb, w_out, w_out, gx.reshape(1, d), wq, kv, kv, wo, h, g.reshape(1, d))


def _visible_block_counts(positions, tq, tk, strict):
    batch, seq = positions.shape
    qmin = positions.reshape(batch, seq // tq, tq).min(-1)[:, :, None]
    qmax = positions.reshape(batch, seq // tq, tq).max(-1)[:, :, None]
    kmin = positions.reshape(batch, seq // tk, tk).min(-1)[:, None, :]
    kmax = positions.reshape(batch, seq // tk, tk).max(-1)[:, None, :]
    full = (kmax < qmin) if strict else (kmax <= qmin)
    some = (kmin < qmax) if strict else (kmin <= qmax)
    n_full = jnp.sum(full, axis=-1).astype(jnp.int32).reshape(-1)
    n_vis = jnp.sum(some, axis=-1).astype(jnp.int32).reshape(-1)
    return n_full, jnp.maximum(n_vis, n_full)


def _in_proj_col_scale():
    scale = np.ones((1, IN_COLS), np.float32)
    scale[:, DIFF_Q_BLK * HEAD_COLS:(DIFF_Q_BLK + DIFF_HEADS) * HEAD_COLS] = DIFF_QK_DIM ** -0.5 * LOG2E
    scale[:, SB_Q_BLK * HEAD_COLS:(SB_Q_BLK + SB_HEADS) * HEAD_COLS] = SB_DIM ** -0.5 * LOG2E
    return jnp.asarray(scale)


def kernel(x, mem, positions, norm_mix, w_in, lam_q1, lam_k1, lam_q2, lam_k2, subln_diff, subln_sb, w_out,
           norm_x, norm_mem, wq_x, wkv_x, wo_x, norm_mlp, w_up, w_down, norm_final):
    batch, seq, d = x.shape
    depth = w_in.shape[0]
    tokens = batch * seq
    assert d == D_MODEL and w_in.shape[2] == IN_COLS
    assert seq % DIFF_TQ == 0 and seq % DIFF_TK == 0 and seq % SB_TQ == 0 and seq % SB_TK == 0

    pos_f = positions.astype(_F32)
    pos_row = pos_f.reshape(batch, seq)
    pos_rep = jnp.broadcast_to(pos_f.reshape(tokens, 1), (tokens, LANES))
    diff_counts = _visible_block_counts(positions, DIFF_TQ, DIFF_TK, strict=False)
    sb_counts = _visible_block_counts(positions, SB_TQ, SB_TK, strict=True)
    in_scale = _in_proj_col_scale()

    h = x.reshape(tokens, d)
    mem2 = mem.reshape(batch * N_MEM, d)
    u = _rmsnorm_bf16(h, norm_mix[0], tm=512)
    out = None
    for l in range(depth):
        lambda_init = 0.8 - 0.6 * math.exp(-0.3 * l)
        qkv = _proj(u, w_in, l, in_scale, tm=1024, tn=1024, name="in_proj")
        o_diff, layer_bf16 = _diff_attention(qkv, pos_row, pos_rep, *diff_counts,
                                             (lam_q1[l], lam_k1[l], lam_q2[l], lam_k2[l]), subln_diff[l],
                                             lambda_init, batch, seq, l,
                                             [(w_out, None), (wq_x, None), (wo_x, None), (w_up, MLP_TF), (w_down, None)])
        w_out_bf, wq_x_bf, wo_x_bf, w_up_tiled, w_down_bf = layer_bf16
        o_sb = _sb_attention(qkv, pos_row, pos_rep, *sb_counts, subln_sb[l], batch, seq)
        mem_n = _rmsnorm_bf16(mem2, norm_mem[l], tm=256)
        kv = _proj(mem_n, wkv_x, l, None, tm=1024, tn=1024, name="xattn_kv_proj")
        h, um = _mix_xattn_block(o_diff, o_sb, w_out_bf, norm_x[l], wq_x_bf, kv, wo_x_bf, h, norm_mlp[l], tm=256)

        last = l == depth - 1
        g_next = norm_final if last else norm_mix[l + 1]
        h, y = _mlp(um, w_up_tiled, w_down_bf, h, g_next, _F32 if last else _BF16, tm=512)
        if last:
            out = y
        else:
            u = y
    return out.reshape(batch, seq, d)
```
